```python
import math
import jax, jax.numpy as jnp
from jax import lax
import numpy as np

D_MODEL = 1024
BATCH = 8
SEQ = 2048
DEPTH = 1
DEC_BATCH = 4
DEC_SEQ = 4096
PAST_LEN = 128

GLA_HEADS = 4
GLA_DK = 64
GLA_DV = 128
GLA_GATE_RANK = 16
GLA_TAU = 16.0
GLA_CHUNK = 64
DIFF_HEADS = 4
DIFF_DH = 64
DIFF_DV = 2 * DIFF_DH
Q_BLOCK = 128
ROPE_THETA = 10000.0
N_EXPERTS = 16
EC_FACTOR = 2
D_EXPERT = 2 * D_MODEL
RMS_EPS = 1e-6

GLA_QK = GLA_HEADS * GLA_DK
GLA_V = GLA_HEADS * GLA_DV
DIFF_QK = DIFF_HEADS * 2 * DIFF_DH
DIFF_V = DIFF_HEADS * DIFF_DV
MIX_WIDTH = GLA_V + DIFF_V
IN_SIZES = (GLA_QK, GLA_QK, GLA_V, GLA_V, GLA_GATE_RANK, GLA_GATE_RANK, DIFF_QK, DIFF_QK, DIFF_V)
IN_WIDTH = sum(IN_SIZES)

kernel_name = "hymba_gla_diffattn_ec_moe_encoder"


def rms_norm(x, gain):
    xf = x.astype(jnp.float32)
    y = xf * lax.rsqrt(jnp.mean(xf * xf, axis=-1, keepdims=True) + RMS_EPS)
    return y * gain.astype(jnp.float32)


def lambda_init(layer):
    return 0.8 - 0.6 * math.exp(-0.3 * layer)


def apply_rope(x):
    T, dh = x.shape[1], x.shape[-1]
    inv_freq = jnp.power(ROPE_THETA, -jnp.arange(0, dh, 2, dtype=jnp.float32) / dh)
    ang = jnp.arange(T, dtype=jnp.float32)[:, None] * inv_freq[None, :]
    cos = jnp.cos(ang)[None, :, None, None, :]
    sin = jnp.sin(ang)[None, :, None, None, :]
    x1, x2 = x[..., : dh // 2], x[..., dh // 2:]
    return jnp.concatenate([x1 * cos - x2 * sin, x2 * cos + x1 * sin], axis=-1)


def gla_chunked(q, k, v, log_a, strict):
    B, T, H, DK = q.shape
    DV = v.shape[-1]
    C = GLA_CHUNK
    N = T // C

    def blk(t):
        return t.reshape(B, N, C, H, t.shape[-1]).transpose(0, 3, 1, 2, 4).astype(jnp.float32)

    q, k, v, log_a = blk(q), blk(k), blk(v), blk(log_a)
    b = jnp.cumsum(log_a, axis=3)
    b_mid = b[:, :, :, C // 2 - 1: C // 2, :]
    b_last = b[:, :, :, C - 1: C, :]
    scores = jnp.einsum('bhncd,bhnsd->bhncs', q * jnp.exp(b - b_mid), k * jnp.exp(b_mid - b))
    idx = jnp.arange(C)
    mask = (idx[:, None] > idx[None, :]) if strict else (idx[:, None] >= idx[None, :])
    scores = jnp.where(mask, scores, 0.0)
    o_intra = jnp.einsum('bhncs,bhnse->bhnce', scores, v)
    u = jnp.einsum('bhncd,bhnce->nbhde', k * jnp.exp(b_last - b), v)
    decay = jnp.exp(b_last[:, :, :, 0, :]).transpose(2, 0, 1, 3)

    def step(s, inp):
        d, du = inp
        return d[..., None] * s + du, s

    s0 = jnp.zeros((B, H, DK, DV), jnp.float32)
    _, s_prev = lax.scan(step, s0, (decay, u))
    o_inter = jnp.einsum('bhncd,nbhde->bhnce', q * jnp.exp(b), s_prev)
    o = o_intra + o_inter
    return o.transpose(0, 2, 3, 1, 4).reshape(B, T, H, DV)


def gla_mixer(q, k, v, og, lr_f, lr_b, wg_f, bg_f, wg_b, bg_b, norm_g):
    B, T, _ = q.shape
    q = q.reshape(B, T, GLA_HEADS, GLA_DK).astype(jnp.float32) * (GLA_DK ** -0.5)
    k = k.reshape(B, T, GLA_HEADS, GLA_DK)
    v = v.reshape(B, T, GLA_HEADS, GLA_DV)
    z_f = (lr_f @ wg_f).astype(jnp.float32) + bg_f.astype(jnp.float32)
    z_b = (lr_b @ wg_b).astype(jnp.float32) + bg_b.astype(jnp.float32)
    la_f = (jax.nn.log_sigmoid(z_f) / GLA_TAU).reshape(B, T, GLA_HEADS, GLA_DK)
    la_b = (jax.nn.log_sigmoid(z_b) / GLA_TAU).reshape(B, T, GLA_HEADS, GLA_DK)
    o_f = gla_chunked(q, k, v, la_f, False)
    fl = lambda t: jnp.flip(t, axis=1)
    o_b = fl(gla_chunked(fl(q), fl(k), fl(v), fl(la_b), True))
    o = rms_norm(o_f + o_b, norm_g).reshape(B, T, GLA_V)
    return o * jax.nn.silu(og.astype(jnp.float32))


def diff_attention(q, k, v, qn_g, kn_g, lq1, lk1, lq2, lk2, norm_g, lam_init):
    B, T, _ = q.shape
    q = q.reshape(B, T, DIFF_HEADS, 2, DIFF_DH)
    k = k.reshape(B, T, DIFF_HEADS, 2, DIFF_DH)
    v = v.reshape(B, T, DIFF_HEADS, DIFF_DV).astype(jnp.float32)
    q = apply_rope(rms_norm(q, qn_g)) * (DIFF_DH ** -0.5)
    k = apply_rope(rms_norm(k, kn_g))
    f32 = jnp.float32
    lam = (jnp.exp(jnp.sum(lq1.astype(f32) * lk1.astype(f32)))
           - jnp.exp(jnp.sum(lq2.astype(f32) * lk2.astype(f32))) + lam_init)
    nb = T // Q_BLOCK
    qb = q.reshape(B, nb, Q_BLOCK, DIFF_HEADS, 2, DIFF_DH).transpose(1, 0, 2, 3, 4, 5)

    def block(qblk):
        s = jnp.einsum('bqhmd,bkhmd->bhmqk', qblk, k)
        p = jax.nn.softmax(s, axis=-1)
        w = p[:, :, 0] - lam * p[:, :, 1]
        return jnp.einsum('bhqk,bkhe->bqhe', w, v)

    o = lax.map(block, qb)
    o = o.transpose(1, 0, 2, 3, 4).reshape(B, T, DIFF_HEADS, DIFF_DV)
    o = rms_norm(o, norm_g) * (1.0 - lam_init)
    return o.reshape(B, T, DIFF_V)


def expert_choice_ffn(h, w_router, w_gate, w_up, w_down):
    B, T, D = h.shape
    n = B * T
    cap = max(1, EC_FACTOR * n // N_EXPERTS)
    xt = h.reshape(n, D)
    aff = jax.nn.softmax((xt @ w_router).astype(jnp.float32), axis=-1)
    gates, idx = lax.top_k(aff.T, cap)
    xe = xt[idx]
    a = jnp.einsum('ecd,edf->ecf', xe, w_gate)
    u = jnp.einsum('ecd,edf->ecf', xe, w_up)
    ye = jnp.einsum('ecf,efd->ecd', jax.nn.silu(a) * u, w_down).astype(jnp.float32)
    ye = ye * gates[..., None]
    out = jnp.zeros((n, D), jnp.float32).at[idx.reshape(-1)].add(ye.reshape(-1, D))
    return out.reshape(B, T, D)


def encoder_layer(x, layer, norm1_g, w_in, gla_wg_f, gla_bg_f, gla_wg_b, gla_bg_b, gla_norm_g,
                  qk_norm_q, qk_norm_k, lambda_q1, lambda_k1, lambda_q2, lambda_k2, diff_norm_g,
                  w_out, norm2_g, w_router, w_gate, w_up, w_down):
    dt = x.dtype
    h = rms_norm(x, norm1_g).astype(dt)
    proj = h @ w_in
    offs = [int(o) for o in np.cumsum(IN_SIZES)[:-1]]
    gq, gk, gv, gog, glr_f, glr_b, dq, dk, dv = jnp.split(proj, offs, axis=-1)
    a = gla_mixer(gq, gk, gv, gog, glr_f, glr_b, gla_wg_f, gla_bg_f, gla_wg_b, gla_bg_b, gla_norm_g)
    d = diff_attention(dq, dk, dv, qk_norm_q, qk_norm_k, lambda_q1, lambda_k1, lambda_q2,
                       lambda_k2, diff_norm_g, lambda_init(layer))
    mix = jnp.concatenate([a, d], axis=-1).astype(dt)
    x = x + (mix @ w_out).astype(dt)
    h2 = rms_norm(x, norm2_g).astype(dt)
    x = x + expert_choice_ffn(h2, w_router, w_gate, w_up, w_down).astype(dt)
    return x


def setup_inputs(seed: int = 0) -> dict:
    key = jax.random.key(seed)
    ks = jax.random.split(key, 24)
    f32 = jnp.float32
    nrm = lambda k, shape, scale: jax.random.normal(k, shape, f32) * scale
    gain = lambda k, n: 1.0 + 0.01 * jax.random.normal(k, (DEPTH, n), f32)
    return {
        "x_prompt": jax.random.normal(ks[0], (BATCH, SEQ, D_MODEL), f32),
        "x_sample": jax.random.normal(ks[1], (DEC_BATCH, DEC_SEQ, D_MODEL), f32),
        "norm1_g": gain(ks[2], D_MODEL),
        "w_in": nrm(ks[3], (DEPTH, D_MODEL, IN_WIDTH), D_MODEL ** -0.5),
        "gla_wg_f": nrm(ks[4], (DEPTH, GLA_GATE_RANK, GLA_QK), GLA_GATE_RANK ** -0.5),
        "gla_bg_f": nrm(ks[5], (DEPTH, GLA_QK), 0.1),
        "gla_wg_b": nrm(ks[6], (DEPTH, GLA_GATE_RANK, GLA_QK), GLA_GATE_RANK ** -0.5),
        "gla_bg_b": nrm(ks[7], (DEPTH, GLA_QK), 0.1),
        "gla_norm_g": gain(ks[8], GLA_DV),
        "qk_norm_q": gain(ks[9], DIFF_DH),
        "qk_norm_k": gain(ks[10], DIFF_DH),
        "lambda_q1": nrm(ks[11], (DEPTH, DIFF_DH), 0.1),
        "lambda_k1": nrm(ks[12], (DEPTH, DIFF_DH), 0.1),
        "lambda_q2": nrm(ks[13], (DEPTH, DIFF_DH), 0.1),
        "lambda_k2": nrm(ks[14], (DEPTH, DIFF_DH), 0.1),
        "diff_norm_g": gain(ks[15], DIFF_DV),
        "w_out": nrm(ks[16], (DEPTH, MIX_WIDTH, D_MODEL), MIX_WIDTH ** -0.5),
        "norm2_g": gain(ks[17], D_MODEL),
        "w_router": nrm(ks[18], (DEPTH, D_MODEL, N_EXPERTS), D_MODEL ** -0.5),
        "w_gate": nrm(ks[19], (DEPTH, N_EXPERTS, D_MODEL, D_EXPERT), D_MODEL ** -0.5),
        "w_up": nrm(ks[20], (DEPTH, N_EXPERTS, D_MODEL, D_EXPERT), D_MODEL ** -0.5),
        "w_down": nrm(ks[21], (DEPTH, N_EXPERTS, D_EXPERT, D_MODEL), D_EXPERT ** -0.5),
    }


def reference(x_prompt, x_sample, norm1_g, w_in, gla_wg_f, gla_bg_f, gla_wg_b, gla_bg_b,
              gla_norm_g, qk_norm_q, qk_norm_k, lambda_q1, lambda_k1, lambda_q2, lambda_k2,
              diff_norm_g, w_out, norm2_g, w_router, w_gate, w_up, w_down):
    y_prompt = x_prompt
    y_sample = x_sample
    for l in range(DEPTH):
        lw = (norm1_g[l], w_in[l], gla_wg_f[l], gla_bg_f[l], gla_wg_b[l], gla_bg_b[l],
              gla_norm_g[l], qk_norm_q[l], qk_norm_k[l], lambda_q1[l], lambda_k1[l],
              lambda_q2[l], lambda_k2[l], diff_norm_g[l], w_out[l], norm2_g[l],
              w_router[l], w_gate[l], w_up[l], w_down[l])
        y_prompt = encoder_layer(y_prompt, l, *lw)
        y_sample = encoder_layer(y_sample, l, *lw)
    return (y_prompt, y_sample)
```

```python
import functools
import math

import jax
import jax.numpy as jnp
from jax import lax
from jax.experimental import pallas as pl
from jax.experimental.pallas import tpu as pltpu

F32 = jnp.float32
BF16 = jnp.bfloat16
I32 = jnp.int32

LANES = 128
VMEM_LIMIT = 56 * 1024 * 1024

D_MODEL = 1024
GLA_HEADS = 4
GLA_DK = 64
GLA_DV = 128
GLA_RANK = 16
GLA_TAU = 16.0
GLA_CHUNK = 64
DIFF_HEADS = 4
DIFF_DH = 64
DIFF_DV = 128
ROPE_THETA = 10000.0
N_EXPERTS = 16
EC_FACTOR = 2
D_EXPERT = 2 * D_MODEL
RMS_EPS = 1e-6


def _lambda_init(layer):
    return 0.8 - 0.6 * math.exp(-0.3 * layer)

HEAD_W = 128
PROJ_GROUPS = 6
PROJ_W = PROJ_GROUPS * 4 * HEAD_W + LANES


def _dot(a, b):
    return jnp.dot(a, b, preferred_element_type=F32)


def _dot_nt(a, b):
    return lax.dot_general(a, b, (((1,), (1,)), ((), ())), preferred_element_type=F32)


def _dot_tn(a, b):
    return lax.dot_general(a, b, (((0,), (0,)), ((), ())), preferred_element_type=F32)


def _split3(x):
    a = x.astype(BF16)
    r = x - a.astype(F32)
    b = r.astype(BF16)
    c = (r - b.astype(F32)).astype(BF16)
    return a, b, c


def _params(sem):
    return pltpu.CompilerParams(dimension_semantics=sem, vmem_limit_bytes=VMEM_LIMIT)


def _proj_kernel(x_ref, g_ref, w_ref, gqk_ref, gv_ref, gog_ref, dq_ref, dk_ref, dv_ref, lr_ref):
    x = x_ref[...]
    ms = jnp.mean(x * x, axis=-1, keepdims=True)
    h = (x * lax.rsqrt(ms + RMS_EPS) * g_ref[...]).astype(BF16)
    gw = 4 * HEAD_W
    for i, o in enumerate((gqk_ref, gv_ref, gog_ref, dq_ref, dk_ref, dv_ref)):
        o[...] = _dot(h, w_ref[:, i * gw:(i + 1) * gw]).astype(o.dtype)
    lr_ref[...] = _dot(h, w_ref[:, PROJ_GROUPS * gw:])


def _proj(x2, g1, w_r, tm):
    n = x2.shape[0]
    gw = 4 * HEAD_W
    row = lambda i: (i, 0)
    outs = [jax.ShapeDtypeStruct((n, gw), BF16)] * PROJ_GROUPS + [jax.ShapeDtypeStruct((n, LANES), F32)]
    return pl.pallas_call(
        _proj_kernel,
        grid=(n // tm,),
        in_specs=[pl.BlockSpec((tm, D_MODEL), row),
                  pl.BlockSpec((1, D_MODEL), lambda i: (0, 0)),
                  pl.BlockSpec((D_MODEL, PROJ_W), lambda i: (0, 0))],
        out_specs=[pl.BlockSpec((tm, gw), row)] * PROJ_GROUPS + [pl.BlockSpec((tm, LANES), row)],
        out_shape=outs,
        compiler_params=_params(("arbitrary",)),
        name="proj",
    )(x2, g1, w_r)


GLA_BLK = 256


def _gla_kernel(gqk_ref, gv_ref, gog_ref, lr_ref, wg_ref, bg_ref, ng_ref, o_ref,
                qh_ref, u_ref, dec_ref, acc_ref, *, T):
    C = GLA_CHUNK
    nblk = T // GLA_BLK
    nch = T // C
    lane = lax.broadcasted_iota(I32, (1, HEAD_W), 1)
    fwd = lane < GLA_DK
    ri = lax.broadcasted_iota(I32, (GLA_BLK, GLA_BLK), 0)
    ci = lax.broadcasted_iota(I32, (GLA_BLK, GLA_BLK), 1)
    same = (ri // C) == (ci // C)
    tri_incl = jnp.where(same & (ci <= ri), 1.0, 0.0).astype(BF16)
    blk_ones = jnp.where(same, 1.0, 0.0).astype(BF16)
    m_f = same & (ci <= ri)
    m_b = same & (ci > ri)
    wg = wg_ref[...]
    bg = bg_ref[...]

    def block(i, carry):
        r0 = pl.multiple_of(i * GLA_BLK, GLA_BLK)
        rows = pl.ds(r0, GLA_BLK)
        z = _dot(lr_ref[rows, :].astype(BF16), wg) + bg
        la = (jnp.minimum(z, 0.0) - jnp.log1p(jnp.exp(-jnp.abs(z)))) * (1.0 / GLA_TAU)
        hi = la.astype(BF16)
        lo = (la - hi.astype(F32)).astype(BF16)
        pre = _dot(tri_incl, hi) + _dot(tri_incl, lo)
        tot = _dot(blk_ones, hi) + _dot(blk_ones, lo)
        b = jnp.where(fwd, pre, tot - pre + la)
        mid = 0.5 * tot
        blk = gqk_ref[rows, :].astype(F32)
        rot = pltpu.roll(blk, GLA_DK, 1)
        qq = jnp.where(fwd, blk, rot) * (GLA_DK ** -0.5)
        kk = jnp.where(fwd, rot, blk)
        qt = qq * jnp.exp(b - mid)
        kt = (kk * jnp.exp(mid - b)).astype(BF16)
        kh = (kk * jnp.exp(tot - b)).astype(BF16)
        qh_ref[rows, :] = (qq * jnp.exp(b)).astype(BF16)
        s_f = _dot_nt(jnp.where(fwd, qt, 0.0).astype(BF16), kt)
        s_b = _dot_nt(jnp.where(fwd, 0.0, qt).astype(BF16), kt)
        s = jnp.where(m_f, s_f, jnp.where(m_b, s_b, 0.0)).astype(BF16)
        v = gv_ref[rows, :]
        acc_ref[rows, :] = _dot(s, v)
        dec = jnp.exp(tot)
        for c in range(GLA_BLK // C):
            cr = slice(c * C, (c + 1) * C)
            u_ref[i * (GLA_BLK // C) + c] = _dot_tn(v[cr, :], kh[cr, :])
            dec_ref[i * (GLA_BLK // C) + c] = jnp.broadcast_to(dec[c * C:c * C + 1, :], (8, HEAD_W))
        return carry

    lax.fori_loop(0, nblk, block, 0)

    def step(t, st):
        nf = t
        nb = nch - 1 - t
        stb = st.astype(BF16)
        rf = pl.ds(pl.multiple_of(nf * C, C), C)
        rb = pl.ds(pl.multiple_of(nb * C, C), C)
        qf = qh_ref[rf, :]
        qb = qh_ref[rb, :]
        zero = jnp.zeros_like(qf)
        acc_ref[rf, :] += _dot_nt(jnp.where(fwd, qf, zero), stb)
        acc_ref[rb, :] += _dot_nt(jnp.where(fwd, zero, qb), stb)
        d = jnp.where(fwd, dec_ref[nf][0:1, :], dec_ref[nb][0:1, :])
        u = jnp.where(fwd, u_ref[nf], u_ref[nb])
        return st * d + u

    lax.fori_loop(0, nch, step, jnp.zeros((GLA_DV, HEAD_W), F32))

    ng = ng_ref[...]
    eb = min(T, 512)

    def epi(i, carry):
        rows = pl.ds(pl.multiple_of(i * eb, eb), eb)
        o = acc_ref[rows, :]
        y = o * lax.rsqrt(jnp.mean(o * o, axis=-1, keepdims=True) + RMS_EPS) * ng
        og = gog_ref[rows, :].astype(F32)
        o_ref[rows, :] = (y * (og * jax.nn.sigmoid(og))).astype(o_ref.dtype)
        return carry

    lax.fori_loop(0, T // eb, epi, 0)


def _gla(gqk, gv, gog, lr, wg, bg, ng, B, T):
    n = B * T
    head = pl.BlockSpec((T, HEAD_W), lambda b, h: (b, h))
    return pl.pallas_call(
        functools.partial(_gla_kernel, T=T),
        grid=(B, GLA_HEADS),
        in_specs=[head, head, head,
                  pl.BlockSpec((T, LANES), lambda b, h: (b, 0)),
                  pl.BlockSpec((None, LANES, HEAD_W), lambda b, h: (h, 0, 0)),
                  pl.BlockSpec((None, 1, HEAD_W), lambda b, h: (h, 0, 0)),
                  pl.BlockSpec((1, HEAD_W), lambda b, h: (0, 0))],
        out_specs=head,
        out_shape=jax.ShapeDtypeStruct((n, GLA_HEADS * HEAD_W), BF16),
        scratch_shapes=[pltpu.VMEM((T, HEAD_W), BF16),
                        pltpu.VMEM((T // GLA_CHUNK, GLA_DV, HEAD_W), F32),
                        pltpu.VMEM((T // GLA_CHUNK, 8, HEAD_W), F32),
                        pltpu.VMEM((T, GLA_DV), F32)],
        compiler_params=_params(("arbitrary", "arbitrary")),
        name="gla",
    )(gqk, gv, gog, lr, wg, bg, ng)


def _qk_prep(x, gain, cos, sin, lo_half):
    x2 = x * x
    s_lo = jnp.sum(jnp.where(lo_half, x2, 0.0), axis=-1, keepdims=True)
    s_hi = jnp.sum(jnp.where(lo_half, 0.0, x2), axis=-1, keepdims=True)
    ms = jnp.where(lo_half, s_lo, s_hi) * (1.0 / DIFF_DH)
    y = x * lax.rsqrt(ms + RMS_EPS) * gain
    lane = lax.broadcasted_iota(I32, (1, HEAD_W), 1)
    first = (lane & (DIFF_DH // 2)) == 0
    half = DIFF_DH // 2
    swapped = jnp.where(first, pltpu.roll(y, HEAD_W - half, 1), pltpu.roll(y, half, 1))
    return y * cos + swapped * sin


def _dattn_kernel(dq_ref, dk_ref, dv_ref, cos_ref, sin_ref, qg_ref, kg_ref, ng_ref, lam_ref, o_ref,
                  kp_ref, s0_ref, s1_ref, *, T, QB, TK, lam_init):
    qi = pl.program_id(2)
    lane = lax.broadcasted_iota(I32, (1, HEAD_W), 1)
    map0 = lane < DIFF_DH
    nk = T // TK

    @pl.when(qi == 0)
    def _():
        def kprep(i, carry):
            rows = pl.ds(pl.multiple_of(i * TK, TK), TK)
            k = _qk_prep(dk_ref[rows, :].astype(F32), kg_ref[...], cos_ref[rows, :], sin_ref[rows, :], map0)
            kp_ref[rows, :] = k.astype(BF16)
            return carry
        lax.fori_loop(0, nk, kprep, 0)

    qrows = pl.ds(pl.multiple_of(qi * QB, QB), QB)
    q = _qk_prep(dq_ref[...].astype(F32), qg_ref[...], cos_ref[qrows, :], sin_ref[qrows, :], map0)
    q = q * (DIFF_DH ** -0.5)
    q0 = jnp.where(map0, q, 0.0).astype(BF16)
    q1 = jnp.where(map0, 0.0, q).astype(BF16)

    def scores(i, carry):
        m0, m1 = carry
        cols = pl.ds(pl.multiple_of(i * TK, TK), TK)
        k = kp_ref[cols, :]
        a0 = _dot_nt(q0, k)
        a1 = _dot_nt(q1, k)
        s0_ref[:, cols] = a0
        s1_ref[:, cols] = a1
        return (jnp.maximum(m0, jnp.max(a0, axis=-1, keepdims=True)),
                jnp.maximum(m1, jnp.max(a1, axis=-1, keepdims=True)))

    neg = jnp.full((QB, 1), -jnp.inf, F32)
    m0, m1 = lax.fori_loop(0, nk, scores, (neg, neg))

    def accum(i, carry):
        l0, l1, acc0, acc1 = carry
        cols = pl.ds(pl.multiple_of(i * TK, TK), TK)
        v = dv_ref[cols, :]
        p0 = jnp.exp(s0_ref[:, cols] - m0)
        p1 = jnp.exp(s1_ref[:, cols] - m1)
        return (l0 + jnp.sum(p0, axis=-1, keepdims=True), l1 + jnp.sum(p1, axis=-1, keepdims=True),
                acc0 + _dot(p0.astype(BF16), v), acc1 + _dot(p1.astype(BF16), v))

    zc = jnp.zeros((QB, 1), F32)
    za = jnp.zeros((QB, DIFF_DV), F32)
    l0, l1, acc0, acc1 = lax.fori_loop(0, nk, accum, (zc, zc, za, za))
    lam = lam_ref[...]
    o = acc0 / l0 - lam * (acc1 / l1)
    y = o * lax.rsqrt(jnp.mean(o * o, axis=-1, keepdims=True) + RMS_EPS) * ng_ref[...]
    o_ref[...] = (y * (1.0 - lam_init)).astype(o_ref.dtype)


def _lambda_kernel(q1_ref, k1_ref, q2_ref, k2_ref, o_ref, *, lam_init):
    a = jnp.exp(jnp.sum(q1_ref[...] * k1_ref[...], axis=-1, keepdims=True))
    b = jnp.exp(jnp.sum(q2_ref[...] * k2_ref[...], axis=-1, keepdims=True))
    o_ref[...] = a - b + lam_init


def _dattn(dq, dk, dv, cos, sin, qg, kg, ng, lam, B, T, QB, lam_init):
    n = B * T
    TK = min(T, 512)
    nq = T // QB
    full = lambda shape: pl.BlockSpec(shape, lambda b, h, q: (0, 0))
    kv = pl.BlockSpec((T, HEAD_W), lambda b, h, q: (b, h))
    qo = pl.BlockSpec((QB, HEAD_W), lambda b, h, q: (b * nq + q, h))
    return pl.pallas_call(
        functools.partial(_dattn_kernel, T=T, QB=QB, TK=TK, lam_init=lam_init),
        grid=(B, DIFF_HEADS, nq),
        in_specs=[qo, kv, kv, full((T, HEAD_W)), full((T, HEAD_W)),
                  full((1, HEAD_W)), full((1, HEAD_W)), full((1, HEAD_W)), full((1, 1))],
        out_specs=qo,
        out_shape=jax.ShapeDtypeStruct((n, DIFF_HEADS * HEAD_W), BF16),
        scratch_shapes=[pltpu.VMEM((T, HEAD_W), BF16),
                        pltpu.VMEM((QB, T), F32),
                        pltpu.VMEM((QB, T), F32)],
        compiler_params=_params(("arbitrary", "arbitrary", "arbitrary")),
        name="dattn",
    )(dq, dk, dv, cos, sin, qg, kg, ng, lam)


def _mixout_kernel(a_ref, d_ref, x_ref, w_ref, g_ref, wr_ref, x1_ref, h2_ref, aff_ref):
    half = GLA_HEADS * HEAD_W
    y = _dot(a_ref[...], w_ref[:half, :]) + _dot(d_ref[...], w_ref[half:, :])
    x1 = x_ref[...] + y
    x1_ref[...] = x1
    h2 = x1 * lax.rsqrt(jnp.mean(x1 * x1, axis=-1, keepdims=True) + RMS_EPS) * g_ref[...]
    h2_ref[...] = h2
    logits = jnp.zeros((N_EXPERTS, h2.shape[0]), F32)
    hp = _split3(h2)
    wp = _split3(wr_ref[...])
    for i in range(3):
        for j in range(3 - i):
            logits = logits + _dot_nt(wp[i], hp[j])
    m = jnp.max(logits, axis=0, keepdims=True)
    e = jnp.exp(logits - m)
    aff_ref[...] = e / jnp.sum(e, axis=0, keepdims=True)


def _mixout(a, d, x2, w_out, g2, wr_t, tm):
    n = x2.shape[0]
    row = lambda i: (i, 0)
    half = GLA_HEADS * HEAD_W
    return pl.pallas_call(
        _mixout_kernel,
        grid=(n // tm,),
        in_specs=[pl.BlockSpec((tm, half), row), pl.BlockSpec((tm, half), row),
                  pl.BlockSpec((tm, D_MODEL), row),
                  pl.BlockSpec((D_MODEL, D_MODEL), lambda i: (0, 0)),
                  pl.BlockSpec((1, D_MODEL), lambda i: (0, 0)),
                  pl.BlockSpec((N_EXPERTS, D_MODEL), lambda i: (0, 0))],
        out_specs=[pl.BlockSpec((tm, D_MODEL), row), pl.BlockSpec((tm, D_MODEL), row),
                   pl.BlockSpec((N_EXPERTS, tm), lambda i: (0, i))],
        out_shape=[jax.ShapeDtypeStruct((n, D_MODEL), F32), jax.ShapeDtypeStruct((n, D_MODEL), F32),
                   jax.ShapeDtypeStruct((N_EXPERTS, n), F32)],
        compiler_params=_params(("arbitrary",)),
        name="mixout",
    )(a, d, x2, w_out, g2, wr_t)


def _route_kernel(aff_ref, idx_ref, gate_ref, *, n, cap):
    R = n // LANES
    E = N_EXPERTS
    aff = aff_ref[...]
    bits = pltpu.bitcast(aff, I32)

    def count(mask):
        c = jnp.sum(jnp.where(mask, 1.0, 0.0), axis=1, keepdims=True)
        return jnp.sum(c, axis=2, keepdims=True)

    capf = float(cap)

    def thr_step(i, thr):
        cand = thr | jnp.left_shift(jnp.int32(1), 30 - i)
        return jnp.where(count(bits >= cand) >= capf, cand, thr)

    thr = lax.fori_loop(0, 31, thr_step, jnp.zeros((E, 1, 1), I32))
    gt = bits > thr
    eq = bits == thr
    need = capf - count(gt)
    tok = (lax.broadcasted_iota(I32, (1, R, LANES), 1) * LANES
           + lax.broadcasted_iota(I32, (1, R, LANES), 2))
    nbits = max(1, (n - 1).bit_length())

    def tie_step(i, m0):
        cand = m0 | jnp.left_shift(jnp.int32(1), nbits - 1 - i)
        return jnp.where(count(eq & (tok < cand)) < need, cand, m0)

    m0 = lax.fori_loop(0, nbits, tie_step, jnp.zeros((E, 1, 1), I32))
    sel = gt | (eq & (tok <= m0))

    ci = lax.broadcasted_iota(I32, (LANES, LANES), 0)
    cj = lax.broadcasted_iota(I32, (LANES, LANES), 1)
    upper = jnp.where(ci <= cj, 1.0, 0.0).astype(BF16)
    ri = lax.broadcasted_iota(I32, (R, R), 0)
    rj = lax.broadcasted_iota(I32, (R, R), 1)
    lower = jnp.where(rj < ri, 1.0, 0.0).astype(BF16)
    jrow = lax.broadcasted_iota(I32, (1, cap), 1).astype(F32)
    rcol = lax.broadcasted_iota(I32, (R, 1), 0).astype(F32)
    ccol = lax.broadcasted_iota(I32, (LANES, 1), 0).astype(F32)

    for e in range(E):
        sel_e = sel[e]
        w = _dot(jnp.where(sel_e, 1.0, 0.0).astype(BF16), upper)
        rt = w[:, LANES - 1:LANES]
        rp = _dot(lower, jnp.broadcast_to(rt, (R, LANES)).astype(BF16))[:, 0:1]
        rows_t = jnp.where((rp <= jrow) & (jrow < rp + rt), 1.0, 0.0)
        k1 = jrow - jnp.sum(rows_t * rp, axis=0, keepdims=True) + 1.0
        rows_b = rows_t.astype(BF16)
        wrow = _dot(jnp.where(sel_e, w, 0.0).T.astype(BF16), rows_b)
        cols_t = jnp.where(wrow == k1, 1.0, 0.0)
        r_of = jnp.sum(rows_t * rcol, axis=0, keepdims=True)
        c_of = jnp.sum(cols_t * ccol, axis=0, keepdims=True)
        idx_ref[e] = (r_of * float(LANES) + c_of).astype(I32)
        a3 = _split3(aff[e].T)
        arow = _dot(a3[0], rows_b) + _dot(a3[1], rows_b) + _dot(a3[2], rows_b)
        gate_ref[e] = jnp.sum(cols_t * arow, axis=0, keepdims=True)


def _route(aff_t, n, cap):
    R = n // LANES
    return pl.pallas_call(
        functools.partial(_route_kernel, n=n, cap=cap),
        out_shape=[jax.ShapeDtypeStruct((N_EXPERTS, 1, cap), I32),
                   jax.ShapeDtypeStruct((N_EXPERTS, 1, cap), F32)],
        compiler_params=pltpu.CompilerParams(vmem_limit_bytes=VMEM_LIMIT),
        name="route",
    )(aff_t.reshape(N_EXPERTS, R, LANES))


FFN_FC = 512
FFN_TT = 512


def _ffn_kernel(idx_ref, gate_ref, h2_hbm, wg_ref, wu_ref, wd_ref, x1_hbm, out_hbm,
                stage_ref, xb_ref, acc_ref, orow_ref, wgb_ref, wub_ref, wdb_ref, sems, *, cap):
    del x1_hbm
    e = pl.program_id(0)
    fc = pl.program_id(1)
    ne = pl.num_programs(0)
    nfc = pl.num_programs(1)
    g_sem, o_sem, s_sem = sems.at[0], sems.at[1], sems.at[2]
    tt = min(FFN_TT, cap)

    def row_copy(src, s_row, dst, d_row, sem):
        return pltpu.make_async_copy(src.at[pl.ds(s_row, 1)], dst.at[pl.ds(d_row, 1)], sem)

    def issue_gather(ee, src, dst, sem):
        def body(j, c):
            row_copy(src, idx_ref[ee * cap + j], dst, j, sem).start()
            return c
        lax.fori_loop(0, cap, body, 0, unroll=8)

    def wait_rows(buf, sem):
        pltpu.make_async_copy(buf, buf, sem).wait()

    @pl.when(fc == 0)
    def _():
        @pl.when(e == 0)
        def _():
            issue_gather(0, h2_hbm, stage_ref, g_sem)
        wait_rows(stage_ref, g_sem)
        xb_ref[...] = stage_ref[...].astype(BF16)

        @pl.when(e + 1 < ne)
        def _():
            issue_gather(e + 1, h2_hbm, stage_ref, g_sem)

    @pl.when(fc == 1)
    def _():
        @pl.when(e > 0)
        def _():
            wait_rows(orow_ref, s_sem)
        issue_gather(e, out_hbm, orow_ref, o_sem)

    wgb_ref[...] = wg_ref[...].astype(BF16)
    wub_ref[...] = wu_ref[...].astype(BF16)
    wdb_ref[...] = wd_ref[...].astype(BF16)

    def tile(t, c):
        rows = pl.ds(pl.multiple_of(t * tt, tt), tt)
        x = xb_ref[rows, :]
        a = _dot(x, wgb_ref[...])
        u = _dot(x, wub_ref[...])
        hm = (a * jax.nn.sigmoid(a) * u).astype(BF16)
        y = _dot(hm, wdb_ref[...])

        @pl.when(fc == 0)
        def _():
            acc_ref[rows, :] = y

        @pl.when(fc > 0)
        def _():
            acc_ref[rows, :] += y
        return c

    lax.fori_loop(0, cap // tt, tile, 0)

    @pl.when(fc == nfc - 1)
    def _():
        wait_rows(orow_ref, o_sem)
        for b in range(cap // LANES):
            rows = slice(b * LANES, (b + 1) * LANES)
            g = gate_ref[:, rows]
            gcol = jnp.broadcast_to(g, (LANES, LANES)).T
            gfull = jnp.concatenate([gcol] * (D_MODEL // LANES), axis=1)
            orow_ref[rows, :] += acc_ref[rows, :] * gfull

        def body(j, c):
            row_copy(orow_ref, j, out_hbm, idx_ref[e * cap + j], s_sem).start()
            return c
        lax.fori_loop(0, cap, body, 0, unroll=8)

        @pl.when(e == ne - 1)
        def _():
            wait_rows(orow_ref, s_sem)


def _ffn(idx, gate, h2, x1, w_gate, w_up, w_down, cap):
    n = h2.shape[0]
    nfc = D_EXPERT // FFN_FC
    grid_spec = pltpu.PrefetchScalarGridSpec(
        num_scalar_prefetch=1,
        grid=(N_EXPERTS, nfc),
        in_specs=[pl.BlockSpec((None, 1, cap), lambda e, f, idx: (e, 0, 0)),
                  pl.BlockSpec(memory_space=pl.ANY),
                  pl.BlockSpec((None, D_MODEL, FFN_FC), lambda e, f, idx: (e, 0, f)),
                  pl.BlockSpec((None, D_MODEL, FFN_FC), lambda e, f, idx: (e, 0, f)),
                  pl.BlockSpec((None, FFN_FC, D_MODEL), lambda e, f, idx: (e, f, 0)),
                  pl.BlockSpec(memory_space=pl.ANY)],
        out_specs=pl.BlockSpec(memory_space=pl.ANY),
        scratch_shapes=[pltpu.VMEM((cap, D_MODEL), F32),
                        pltpu.VMEM((cap, D_MODEL), BF16),
                        pltpu.VMEM((cap, D_MODEL), F32),
                        pltpu.VMEM((cap, D_MODEL), F32),
                        pltpu.VMEM((D_MODEL, FFN_FC), BF16),
                        pltpu.VMEM((D_MODEL, FFN_FC), BF16),
                        pltpu.VMEM((FFN_FC, D_MODEL), BF16),
                        pltpu.SemaphoreType.DMA((3,))],
    )
    return pl.pallas_call(
        functools.partial(_ffn_kernel, cap=cap),
        grid_spec=grid_spec,
        out_shape=jax.ShapeDtypeStruct((n, D_MODEL), F32),
        input_output_aliases={6: 0},
        compiler_params=_params(("arbitrary", "arbitrary")),
        name="ffn",
    )(idx.reshape(-1), gate, h2, w_gate, w_up, w_down, x1)


def _rope_tables(T):
    half = DIFF_DH // 2
    inv_freq = jnp.power(ROPE_THETA, -jnp.arange(0, DIFF_DH, 2, dtype=F32) / DIFF_DH)
    ang = jnp.arange(T, dtype=F32)[:, None] * inv_freq[None, :]
    cos, sin = jnp.cos(ang), jnp.sin(ang)
    reps = HEAD_W // DIFF_DH
    return (jnp.concatenate([cos, cos] * reps, axis=-1),
            jnp.concatenate([-sin, sin] * reps, axis=-1))


def _layout_weights(w_in, gla_wg_f, gla_bg_f, gla_wg_b, gla_bg_b):
    q0, k0, v0, g0 = 0, 256, 512, 1024
    lr0, dq0 = 1536, 1568
    cols = []
    for h in range(GLA_HEADS):
        cols += [w_in[:, q0 + h * GLA_DK:q0 + (h + 1) * GLA_DK], w_in[:, k0 + h * GLA_DK:k0 + (h + 1) * GLA_DK]]
    cols += [w_in[:, v0:g0], w_in[:, g0:lr0], w_in[:, dq0:], w_in[:, lr0:dq0],
             jnp.zeros((D_MODEL, LANES - 2 * GLA_RANK), w_in.dtype)]
    w_r = jnp.concatenate(cols, axis=1).astype(BF16)
    wg = jnp.zeros((GLA_HEADS, LANES, HEAD_W), F32)
    bg = jnp.zeros((GLA_HEADS, 1, HEAD_W), F32)
    for h in range(GLA_HEADS):
        hs = slice(h * GLA_DK, (h + 1) * GLA_DK)
        wg = wg.at[h, :GLA_RANK, :GLA_DK].set(gla_wg_f[:, hs])
        wg = wg.at[h, GLA_RANK:2 * GLA_RANK, GLA_DK:].set(gla_wg_b[:, hs])
        bg = bg.at[h, 0, :GLA_DK].set(gla_bg_f[hs])
        bg = bg.at[h, 0, GLA_DK:].set(gla_bg_b[hs])
    return w_r, wg.astype(BF16), bg


def _lambda(lq1, lk1, lq2, lk2, lam_init):
    pad = lambda v: jnp.pad(v.astype(F32), (0, LANES - v.shape[0])).reshape(1, LANES)
    return pl.pallas_call(
        functools.partial(_lambda_kernel, lam_init=lam_init),
        out_shape=jax.ShapeDtypeStruct((1, 1), F32),
        name="lam",
    )(pad(lq1), pad(lk1), pad(lq2), pad(lk2))


def _encoder_layer(x, layer, lw, *, tm=512, qb=256):
    (norm1_g, w_in, gla_wg_f, gla_bg_f, gla_wg_b, gla_bg_b, gla_norm_g, qk_norm_q, qk_norm_k,
     lambda_q1, lambda_k1, lambda_q2, lambda_k2, diff_norm_g, w_out, norm2_g, w_router,
     w_gate, w_up, w_down) = lw
    B, T, D = x.shape
    n = B * T
    cap = max(1, EC_FACTOR * n // N_EXPERTS)
    tm = min(tm, n)
    qb = min(qb, T)
    x2 = x.reshape(n, D)
    w_r, wg, bg = _layout_weights(w_in, gla_wg_f, gla_bg_f, gla_wg_b, gla_bg_b)
    row = lambda v: v.astype(F32).reshape(1, -1)
    two = lambda v: jnp.concatenate([v, v]).astype(F32).reshape(1, -1)

    gqk, gv, gog, dq, dk, dv, lr = _proj(x2, row(norm1_g), w_r, tm)
    a = _gla(gqk, gv, gog, lr, wg, bg, row(gla_norm_g), B, T)
    cos, sin = _rope_tables(T)
    lam_init = _lambda_init(layer)
    lam = _lambda(lambda_q1, lambda_k1, lambda_q2, lambda_k2, lam_init)
    d = _dattn(dq, dk, dv, cos, sin, two(qk_norm_q), two(qk_norm_k), row(diff_norm_g), lam, B, T, qb, lam_init)
    x1, h2, aff_t = _mixout(a, d, x2, w_out.astype(BF16), row(norm2_g), w_router.T.astype(F32), tm)
    idx, gate = _route(aff_t, n, cap)
    y = _ffn(idx, gate, h2, x1, w_gate, w_up, w_down, cap)
    return y.reshape(B, T, D)


def kernel(x_prompt, x_sample, norm1_g, w_in, gla_wg_f, gla_bg_f, gla_wg_b, gla_bg_b, gla_norm_g, qk_norm_q, qk_norm_k, lambda_q1, lambda_k1, lambda_q2, lambda_k2, diff_norm_g, w_out, norm2_g, w_router, w_gate, w_up, w_down):
    y_prompt, y_sample = x_prompt, x_sample
    for l in range(norm1_g.shape[0]):
        lw = (norm1_g[l], w_in[l], gla_wg_f[l], gla_bg_f[l], gla_wg_b[l], gla_bg_b[l],
              gla_norm_g[l], qk_norm_q[l], qk_norm_k[l], lambda_q1[l], lambda_k1[l],
              lambda_q2[l], lambda_k2[l], diff_norm_g[l], w_out[l], norm2_g[l],
              w_router[l], w_gate[l], w_up[l], w_down[l])
        y_prompt = _encoder_layer(y_prompt, l, lw)
        y_sample = _encoder_layer(y_sample, l, lw)
    return (y_prompt, y_sample)
```

```python
import functools
import math

import jax
import jax.numpy as jnp
from jax import lax
from jax.experimental import pallas as pl
from jax.experimental.pallas import tpu as pltpu

F32 = jnp.float32
BF16 = jnp.bfloat16
I32 = jnp.int32

LANES = 128
VMEM_LIMIT = 56 * 1024 * 1024

D_MODEL = 1024
GLA_HEADS = 4
GLA_DK = 64
GLA_DV = 128
GLA_RANK = 16
GLA_TAU = 16.0
GLA_CHUNK = 64
DIFF_HEADS = 4
DIFF_DH = 64
DIFF_DV = 128
ROPE_THETA = 10000.0
N_EXPERTS = 16
EC_FACTOR = 2
D_EXPERT = 2 * D_MODEL
RMS_EPS = 1e-6
LOG2_E = 1.4426950408889634


def _lambda_init(layer):
    return 0.8 - 0.6 * math.exp(-0.3 * layer)


HEAD_W = 128
PROJ_GROUPS = 6
PROJ_W = PROJ_GROUPS * 4 * HEAD_W + LANES


def _dot(a, b):
    return jnp.dot(a, b, preferred_element_type=F32)


def _dot_nt(a, b):
    return lax.dot_general(a, b, (((1,), (1,)), ((), ())), preferred_element_type=F32)


def _dot_tn(a, b):
    return lax.dot_general(a, b, (((0,), (0,)), ((), ())), preferred_element_type=F32)


def _split3(x):
    a = x.astype(BF16)
    r = x - a.astype(F32)
    b = r.astype(BF16)
    c = (r - b.astype(F32)).astype(BF16)
    return a, b, c


def _params(sem):
    return pltpu.CompilerParams(dimension_semantics=sem, vmem_limit_bytes=VMEM_LIMIT)


def _proj_kernel(x_ref, g_ref, w_ref, gqk_ref, gv_ref, gog_ref, dq_ref, dk_ref, dv_ref, lr_ref):
    x = x_ref[...]
    ms = jnp.mean(x * x, axis=-1, keepdims=True)
    h = (x * lax.rsqrt(ms + RMS_EPS) * g_ref[...]).astype(BF16)
    gw = 4 * HEAD_W
    for i, o in enumerate((gqk_ref, gv_ref, gog_ref, dq_ref, dk_ref, dv_ref)):
        o[...] = _dot(h, w_ref[:, i * gw:(i + 1) * gw]).astype(o.dtype)
    lr_ref[...] = _dot(h, w_ref[:, PROJ_GROUPS * gw:])


def _proj(x2, g1, w_r, tm):
    n = x2.shape[0]
    gw = 4 * HEAD_W
    row = lambda i: (i, 0)
    outs = [jax.ShapeDtypeStruct((n, gw), BF16)] * PROJ_GROUPS + [jax.ShapeDtypeStruct((n, LANES), F32)]
    return pl.pallas_call(
        _proj_kernel,
        grid=(n // tm,),
        in_specs=[pl.BlockSpec((tm, D_MODEL), row),
                  pl.BlockSpec((1, D_MODEL), lambda i: (0, 0)),
                  pl.BlockSpec((D_MODEL, PROJ_W), lambda i: (0, 0))],
        out_specs=[pl.BlockSpec((tm, gw), row)] * PROJ_GROUPS + [pl.BlockSpec((tm, LANES), row)],
        out_shape=outs,
        compiler_params=_params(("arbitrary",)),
        name="proj",
    )(x2, g1, w_r)


GLA_BLK = 256


def _gla_kernel(gqk_ref, gv_ref, gog_ref, lr_ref, wg_ref, bg_ref, ng_ref, o_ref,
                qh_ref, u_ref, dec_ref, acc_ref, *, T):
    C = GLA_CHUNK
    nblk = T // GLA_BLK
    nch = T // C
    lane = lax.broadcasted_iota(I32, (1, HEAD_W), 1)
    fwd = lane < GLA_DK
    ri = lax.broadcasted_iota(I32, (GLA_BLK, GLA_BLK), 0)
    ci = lax.broadcasted_iota(I32, (GLA_BLK, GLA_BLK), 1)
    same = (ri // C) == (ci // C)
    tri_incl = jnp.where(same & (ci <= ri), 1.0, 0.0).astype(BF16)
    blk_ones = jnp.where(same, 1.0, 0.0).astype(BF16)
    m_f = same & (ci <= ri)
    m_b = same & (ci > ri)
    wg = wg_ref[...]
    bg = bg_ref[...]

    def block(i, carry):
        r0 = pl.multiple_of(i * GLA_BLK, GLA_BLK)
        rows = pl.ds(r0, GLA_BLK)
        z = _dot(lr_ref[rows, :].astype(BF16), wg) + bg
        la = (jnp.minimum(z, 0.0) - jnp.log1p(jnp.exp(-jnp.abs(z)))) * (1.0 / GLA_TAU)
        hi = la.astype(BF16)
        lo = (la - hi.astype(F32)).astype(BF16)
        pre = _dot(tri_incl, hi) + _dot(tri_incl, lo)
        tot = _dot(blk_ones, hi) + _dot(blk_ones, lo)
        b = jnp.where(fwd, pre, tot - pre + la)
        mid = 0.5 * tot
        blk = gqk_ref[rows, :].astype(F32)
        rot = pltpu.roll(blk, GLA_DK, 1)
        qq = jnp.where(fwd, blk, rot) * (GLA_DK ** -0.5)
        kk = jnp.where(fwd, rot, blk)
        qt = qq * jnp.exp(b - mid)
        kt = (kk * jnp.exp(mid - b)).astype(BF16)
        kh = (kk * jnp.exp(tot - b)).astype(BF16)
        qh_ref[rows, :] = (qq * jnp.exp(b)).astype(BF16)
        s_f = _dot_nt(jnp.where(fwd, qt, 0.0).astype(BF16), kt)
        s_b = _dot_nt(jnp.where(fwd, 0.0, qt).astype(BF16), kt)
        s = jnp.where(m_f, s_f, jnp.where(m_b, s_b, 0.0)).astype(BF16)
        v = gv_ref[rows, :]
        acc_ref[rows, :] = _dot(s, v)
        dec = jnp.exp(tot)
        for c in range(GLA_BLK // C):
            cr = slice(c * C, (c + 1) * C)
            u_ref[i * (GLA_BLK // C) + c] = _dot_tn(v[cr, :], kh[cr, :])
            dec_ref[i * (GLA_BLK // C) + c] = jnp.broadcast_to(dec[c * C:c * C + 1, :], (8, HEAD_W))
        return carry

    lax.fori_loop(0, nblk, block, 0)

    def step(t, st):
        nf = t
        nb = nch - 1 - t
        stb = st.astype(BF16)
        rf = pl.ds(pl.multiple_of(nf * C, C), C)
        rb = pl.ds(pl.multiple_of(nb * C, C), C)
        qf = qh_ref[rf, :]
        qb = qh_ref[rb, :]
        zero = jnp.zeros_like(qf)
        acc_ref[rf, :] += _dot_nt(jnp.where(fwd, qf, zero), stb)
        acc_ref[rb, :] += _dot_nt(jnp.where(fwd, zero, qb), stb)
        d = jnp.where(fwd, dec_ref[nf][0:1, :], dec_ref[nb][0:1, :])
        u = jnp.where(fwd, u_ref[nf], u_ref[nb])
        return st * d + u

    lax.fori_loop(0, nch, step, jnp.zeros((GLA_DV, HEAD_W), F32))

    ng = ng_ref[...]
    eb = min(T, 512)

    def epi(i, carry):
        rows = pl.ds(pl.multiple_of(i * eb, eb), eb)
        o = acc_ref[rows, :]
        y = o * lax.rsqrt(jnp.mean(o * o, axis=-1, keepdims=True) + RMS_EPS) * ng
        og = gog_ref[rows, :].astype(F32)
        o_ref[rows, :] = (y * (og * jax.nn.sigmoid(og))).astype(o_ref.dtype)
        return carry

    lax.fori_loop(0, T // eb, epi, 0)


def _gla(gqk, gv, gog, lr, wg, bg, ng, B, T):
    n = B * T
    head = pl.BlockSpec((T, HEAD_W), lambda b, h: (b, h))
    return pl.pallas_call(
        functools.partial(_gla_kernel, T=T),
        grid=(B, GLA_HEADS),
        in_specs=[head, head, head,
                  pl.BlockSpec((T, LANES), lambda b, h: (b, 0)),
                  pl.BlockSpec((None, LANES, HEAD_W), lambda b, h: (h, 0, 0)),
                  pl.BlockSpec((None, 1, HEAD_W), lambda b, h: (h, 0, 0)),
                  pl.BlockSpec((1, HEAD_W), lambda b, h: (0, 0))],
        out_specs=head,
        out_shape=jax.ShapeDtypeStruct((n, GLA_HEADS * HEAD_W), BF16),
        scratch_shapes=[pltpu.VMEM((T, HEAD_W), BF16),
                        pltpu.VMEM((T // GLA_CHUNK, GLA_DV, HEAD_W), F32),
                        pltpu.VMEM((T // GLA_CHUNK, 8, HEAD_W), F32),
                        pltpu.VMEM((T, GLA_DV), F32)],
        compiler_params=_params(("arbitrary", "arbitrary")),
        name="gla",
    )(gqk, gv, gog, lr, wg, bg, ng)


DATTN_KSPLIT = 2


def _dot_hilo(x, m):
    hi = x.astype(BF16)
    lo = (x - hi.astype(F32)).astype(BF16)
    return _dot(hi, m) + _dot(lo, m)


def _qk_prep(x, gain, cos, sin):
    ri = lax.broadcasted_iota(I32, (HEAD_W, HEAD_W), 0)
    ci = lax.broadcasted_iota(I32, (HEAD_W, HEAD_W), 1)
    same_map = jnp.where((ri // DIFF_DH) == (ci // DIFF_DH), 1.0, 0.0).astype(BF16)
    swap = jnp.where(ri == (ci ^ (DIFF_DH // 2)), 1.0, 0.0).astype(BF16)
    ms = _dot_hilo(x * x, same_map) * (1.0 / DIFF_DH)
    y = x * lax.rsqrt(ms + RMS_EPS) * gain
    return y * cos + _dot_hilo(y, swap) * sin


def _dattn_kernel(dq_ref, dk_ref, dv_ref, cos_ref, sin_ref, qg_ref, kg_ref, ng_ref, lam_ref, o_ref,
                  kp_ref, s_ref, mp_ref, lp_ref, acc_ref, *, T, QB, TK, lam_init):
    qi = pl.program_id(2)
    lane = lax.broadcasted_iota(I32, (1, HEAD_W), 1)
    map0 = lane < DIFF_DH
    nk = T // TK

    @pl.when(qi == 0)
    def _():
        kb = min(T, 512)

        def kprep(i, carry):
            rows = pl.ds(pl.multiple_of(i * kb, kb), kb)
            k = _qk_prep(dk_ref[rows, :].astype(F32), kg_ref[...], cos_ref[rows, :], sin_ref[rows, :])
            kp_ref[rows, :] = k.astype(BF16)
            return carry
        lax.fori_loop(0, T // kb, kprep, 0)

    qrows = pl.ds(pl.multiple_of(qi * QB, QB), QB)
    q = _qk_prep(dq_ref[...].astype(F32), qg_ref[...], cos_ref[qrows, :], sin_ref[qrows, :])
    q = q * (DIFF_DH ** -0.5 * LOG2_E)
    qs = jnp.concatenate([jnp.where(map0, q, 0.0), jnp.where(map0, 0.0, q)], axis=0).astype(BF16)
    ntile = TK // LANES

    def lane_tiles(x):
        return [x[:, j * LANES:(j + 1) * LANES] for j in range(ntile)]

    mp_ref[...] = jnp.full(mp_ref.shape, -jnp.inf, F32)
    lp_ref[...] = jnp.zeros(lp_ref.shape, F32)
    acc_ref[...] = jnp.zeros(acc_ref.shape, F32)

    def scores(i, carry):
        cols = pl.ds(pl.multiple_of(i * TK, TK), TK)
        s = _dot_nt(qs, kp_ref[cols, :])
        s_ref[:, cols] = s
        mp_ref[...] = jnp.maximum(mp_ref[...], functools.reduce(jnp.maximum, lane_tiles(s)))
        return carry

    lax.fori_loop(0, nk, scores, 0)
    m = jnp.max(mp_ref[...], axis=-1, keepdims=True)

    def accum(i, carry):
        cols = pl.ds(pl.multiple_of(i * TK, TK), TK)
        p = jnp.exp2(s_ref[:, cols] - m)
        lp_ref[...] += functools.reduce(jnp.add, lane_tiles(p))
        acc_ref[...] += _dot(p.astype(BF16), dv_ref[cols, :])
        return carry

    lax.fori_loop(0, nk, accum, 0)
    on = acc_ref[...] / jnp.sum(lp_ref[...], axis=-1, keepdims=True)
    lam = lam_ref[...]
    o = on[:QB, :] - lam * on[QB:, :]
    y = o * lax.rsqrt(jnp.mean(o * o, axis=-1, keepdims=True) + RMS_EPS) * ng_ref[...]
    o_ref[...] = (y * (1.0 - lam_init)).astype(o_ref.dtype)


def _lambda_kernel(q1_ref, k1_ref, q2_ref, k2_ref, o_ref, *, lam_init):
    a = jnp.exp(jnp.sum(q1_ref[...] * k1_ref[...], axis=-1, keepdims=True))
    b = jnp.exp(jnp.sum(q2_ref[...] * k2_ref[...], axis=-1, keepdims=True))
    o_ref[...] = a - b + lam_init


def _dattn(dq, dk, dv, cos, sin, qg, kg, ng, lam, B, T, QB, lam_init):
    n = B * T
    TK = T // DATTN_KSPLIT
    nq = T // QB
    full = lambda shape: pl.BlockSpec(shape, lambda b, h, q: (0, 0))
    kv = pl.BlockSpec((T, HEAD_W), lambda b, h, q: (b, h))
    qo = pl.BlockSpec((QB, HEAD_W), lambda b, h, q: (b * nq + q, h))
    return pl.pallas_call(
        functools.partial(_dattn_kernel, T=T, QB=QB, TK=TK, lam_init=lam_init),
        grid=(B, DIFF_HEADS, nq),
        in_specs=[qo, kv, kv, full((T, HEAD_W)), full((T, HEAD_W)),
                  full((1, HEAD_W)), full((1, HEAD_W)), full((1, HEAD_W)), full((1, 1))],
        out_specs=qo,
        out_shape=jax.ShapeDtypeStruct((n, DIFF_HEADS * HEAD_W), BF16),
        scratch_shapes=[pltpu.VMEM((T, HEAD_W), BF16),
                        pltpu.VMEM((2 * QB, T), F32),
                        pltpu.VMEM((2 * QB, LANES), F32),
                        pltpu.VMEM((2 * QB, LANES), F32),
                        pltpu.VMEM((2 * QB, DIFF_DV), F32)],
        compiler_params=_params(("arbitrary", "arbitrary", "arbitrary")),
        name="dattn",
    )(dq, dk, dv, cos, sin, qg, kg, ng, lam)


def _mixout_kernel(a_ref, d_ref, x_ref, w_ref, g_ref, wr_ref, x1_ref, h2_ref, aff_ref):
    half = GLA_HEADS * HEAD_W
    y = _dot(a_ref[...], w_ref[:half, :]) + _dot(d_ref[...], w_ref[half:, :])
    x1 = x_ref[...] + y
    x1_ref[...] = x1
    h2 = x1 * lax.rsqrt(jnp.mean(x1 * x1, axis=-1, keepdims=True) + RMS_EPS) * g_ref[...]
    h2_ref[...] = h2
    logits = jnp.zeros((N_EXPERTS, h2.shape[0]), F32)
    hp = _split3(h2)
    wp = _split3(wr_ref[...])
    for i in range(3):
        for j in range(3 - i):
            logits = logits + _dot_nt(wp[i], hp[j])
    m = jnp.max(logits, axis=0, keepdims=True)
    e = jnp.exp(logits - m)
    aff_ref[...] = e / jnp.sum(e, axis=0, keepdims=True)


def _mixout(a, d, x2, w_out, g2, wr_t, tm):
    n = x2.shape[0]
    row = lambda i: (i, 0)
    half = GLA_HEADS * HEAD_W
    return pl.pallas_call(
        _mixout_kernel,
        grid=(n // tm,),
        in_specs=[pl.BlockSpec((tm, half), row), pl.BlockSpec((tm, half), row),
                  pl.BlockSpec((tm, D_MODEL), row),
                  pl.BlockSpec((D_MODEL, D_MODEL), lambda i: (0, 0)),
                  pl.BlockSpec((1, D_MODEL), lambda i: (0, 0)),
                  pl.BlockSpec((N_EXPERTS, D_MODEL), lambda i: (0, 0))],
        out_specs=[pl.BlockSpec((tm, D_MODEL), row), pl.BlockSpec((tm, D_MODEL), row),
                   pl.BlockSpec((N_EXPERTS, tm), lambda i: (0, i))],
        out_shape=[jax.ShapeDtypeStruct((n, D_MODEL), F32), jax.ShapeDtypeStruct((n, D_MODEL), F32),
                   jax.ShapeDtypeStruct((N_EXPERTS, n), F32)],
        compiler_params=_params(("arbitrary",)),
        name="mixout",
    )(a, d, x2, w_out, g2, wr_t)


def _route_kernel(aff_ref, idx_ref, gate_ref, *, n, cap):
    R = n // LANES
    E = N_EXPERTS
    aff = aff_ref[...]
    bits = pltpu.bitcast(aff, I32)

    def count(mask):
        c = jnp.sum(jnp.where(mask, 1.0, 0.0), axis=1, keepdims=True)
        return jnp.sum(c, axis=2, keepdims=True)

    capf = float(cap)

    def thr_step(i, thr):
        cand = thr | jnp.left_shift(jnp.int32(1), 30 - i)
        return jnp.where(count(bits >= cand) >= capf, cand, thr)

    thr = lax.fori_loop(0, 31, thr_step, jnp.zeros((E, 1, 1), I32))
    gt = bits > thr
    eq = bits == thr
    need = capf - count(gt)
    tok = (lax.broadcasted_iota(I32, (1, R, LANES), 1) * LANES
           + lax.broadcasted_iota(I32, (1, R, LANES), 2))
    nbits = max(1, (n - 1).bit_length())

    def tie_step(i, m0):
        cand = m0 | jnp.left_shift(jnp.int32(1), nbits - 1 - i)
        return jnp.where(count(eq & (tok < cand)) < need, cand, m0)

    m0 = lax.fori_loop(0, nbits, tie_step, jnp.zeros((E, 1, 1), I32))
    sel = gt | (eq & (tok <= m0))

    ci = lax.broadcasted_iota(I32, (LANES, LANES), 0)
    cj = lax.broadcasted_iota(I32, (LANES, LANES), 1)
    upper = jnp.where(ci <= cj, 1.0, 0.0).astype(BF16)
    ri = lax.broadcasted_iota(I32, (R, R), 0)
    rj = lax.broadcasted_iota(I32, (R, R), 1)
    lower = jnp.where(rj < ri, 1.0, 0.0).astype(BF16)
    jrow = lax.broadcasted_iota(I32, (1, cap), 1).astype(F32)
    rcol = lax.broadcasted_iota(I32, (R, 1), 0).astype(F32)
    ccol = lax.broadcasted_iota(I32, (LANES, 1), 0).astype(F32)

    for e in range(E):
        sel_e = sel[e]
        w = _dot(jnp.where(sel_e, 1.0, 0.0).astype(BF16), upper)
        rt = w[:, LANES - 1:LANES]
        rp = _dot(lower, jnp.broadcast_to(rt, (R, LANES)).astype(BF16))[:, 0:1]
        rows_t = jnp.where((rp <= jrow) & (jrow < rp + rt), 1.0, 0.0)
        k1 = jrow - jnp.sum(rows_t * rp, axis=0, keepdims=True) + 1.0
        rows_b = rows_t.astype(BF16)
        wrow = _dot(jnp.where(sel_e, w, 0.0).T.astype(BF16), rows_b)
        cols_t = jnp.where(wrow == k1, 1.0, 0.0)
        r_of = jnp.sum(rows_t * rcol, axis=0, keepdims=True)
        c_of = jnp.sum(cols_t * ccol, axis=0, keepdims=True)
        idx_ref[e] = (r_of * float(LANES) + c_of).astype(I32)
        a3 = _split3(aff[e].T)
        arow = _dot(a3[0], rows_b) + _dot(a3[1], rows_b) + _dot(a3[2], rows_b)
        gate_ref[e] = jnp.sum(cols_t * arow, axis=0, keepdims=True)


def _route(aff_t, n, cap):
    R = n // LANES
    return pl.pallas_call(
        functools.partial(_route_kernel, n=n, cap=cap),
        out_shape=[jax.ShapeDtypeStruct((N_EXPERTS, 1, cap), I32),
                   jax.ShapeDtypeStruct((N_EXPERTS, 1, cap), F32)],
        compiler_params=pltpu.CompilerParams(vmem_limit_bytes=VMEM_LIMIT),
        name="route",
    )(aff_t.reshape(N_EXPERTS, R, LANES))


FFN_FC = 512
FFN_TT = 512


def _ffn_kernel(idx_ref, gate_ref, h2_hbm, wg_ref, wu_ref, wd_ref, x1_hbm, out_hbm,
                stage_ref, xb_ref, acc_ref, orow_ref, wgb_ref, wub_ref, wdb_ref, sems, *, cap):
    del x1_hbm
    e = pl.program_id(0)
    fc = pl.program_id(1)
    ne = pl.num_programs(0)
    nfc = pl.num_programs(1)
    g_sem, o_sem, s_sem = sems.at[0], sems.at[1], sems.at[2]
    tt = min(FFN_TT, cap)

    def row_copy(src, s_row, dst, d_row, sem):
        return pltpu.make_async_copy(src.at[pl.ds(s_row, 1)], dst.at[pl.ds(d_row, 1)], sem)

    def issue_gather(ee, src, dst, sem):
        def body(j, c):
            row_copy(src, idx_ref[ee * cap + j], dst, j, sem).start()
            return c
        lax.fori_loop(0, cap, body, 0, unroll=8)

    def wait_rows(buf, sem):
        pltpu.make_async_copy(buf, buf, sem).wait()

    @pl.when(fc == 0)
    def _():
        @pl.when(e == 0)
        def _():
            issue_gather(0, h2_hbm, stage_ref, g_sem)
        wait_rows(stage_ref, g_sem)
        xb_ref[...] = stage_ref[...].astype(BF16)

        @pl.when(e + 1 < ne)
        def _():
            issue_gather(e + 1, h2_hbm, stage_ref, g_sem)

    @pl.when(fc == 1)
    def _():
        @pl.when(e > 0)
        def _():
            wait_rows(orow_ref, s_sem)
        issue_gather(e, out_hbm, orow_ref, o_sem)

    wgb_ref[...] = wg_ref[...].astype(BF16)
    wub_ref[...] = wu_ref[...].astype(BF16)
    wdb_ref[...] = wd_ref[...].astype(BF16)

    def tile(t, c):
        rows = pl.ds(pl.multiple_of(t * tt, tt), tt)
        x = xb_ref[rows, :]
        a = _dot(x, wgb_ref[...])
        u = _dot(x, wub_ref[...])
        hm = (a * jax.nn.sigmoid(a) * u).astype(BF16)
        y = _dot(hm, wdb_ref[...])

        @pl.when(fc == 0)
        def _():
            acc_ref[rows, :] = y

        @pl.when(fc > 0)
        def _():
            acc_ref[rows, :] += y
        return c

    lax.fori_loop(0, cap // tt, tile, 0)

    @pl.when(fc == nfc - 1)
    def _():
        wait_rows(orow_ref, o_sem)
        for b in range(cap // LANES):
            rows = slice(b * LANES, (b + 1) * LANES)
            g = gate_ref[:, rows]
            gcol = jnp.broadcast_to(g, (LANES, LANES)).T
            gfull = jnp.concatenate([gcol] * (D_MODEL // LANES), axis=1)
            orow_ref[rows, :] += acc_ref[rows, :] * gfull

        def body(j, c):
            row_copy(orow_ref, j, out_hbm, idx_ref[e * cap + j], s_sem).start()
            return c
        lax.fori_loop(0, cap, body, 0, unroll=8)

        @pl.when(e == ne - 1)
        def _():
            wait_rows(orow_ref, s_sem)


def _ffn(idx, gate, h2, x1, w_gate, w_up, w_down, cap):
    n = h2.shape[0]
    nfc = D_EXPERT // FFN_FC
    grid_spec = pltpu.PrefetchScalarGridSpec(
        num_scalar_prefetch=1,
        grid=(N_EXPERTS, nfc),
        in_specs=[pl.BlockSpec((None, 1, cap), lambda e, f, idx: (e, 0, 0)),
                  pl.BlockSpec(memory_space=pl.ANY),
                  pl.BlockSpec((None, D_MODEL, FFN_FC), lambda e, f, idx: (e, 0, f)),
                  pl.BlockSpec((None, D_MODEL, FFN_FC), lambda e, f, idx: (e, 0, f)),
                  pl.BlockSpec((None, FFN_FC, D_MODEL), lambda e, f, idx: (e, f, 0)),
                  pl.BlockSpec(memory_space=pl.ANY)],
        out_specs=pl.BlockSpec(memory_space=pl.ANY),
        scratch_shapes=[pltpu.VMEM((cap, D_MODEL), F32),
                        pltpu.VMEM((cap, D_MODEL), BF16),
                        pltpu.VMEM((cap, D_MODEL), F32),
                        pltpu.VMEM((cap, D_MODEL), F32),
                        pltpu.VMEM((D_MODEL, FFN_FC), BF16),
                        pltpu.VMEM((D_MODEL, FFN_FC), BF16),
                        pltpu.VMEM((FFN_FC, D_MODEL), BF16),
                        pltpu.SemaphoreType.DMA((3,))],
    )
    return pl.pallas_call(
        functools.partial(_ffn_kernel, cap=cap),
        grid_spec=grid_spec,
        out_shape=jax.ShapeDtypeStruct((n, D_MODEL), F32),
        input_output_aliases={6: 0},
        compiler_params=_params(("arbitrary", "arbitrary")),
        name="ffn",
    )(idx.reshape(-1), gate, h2, w_gate, w_up, w_down, x1)


def _rope_tables(T):
    half = DIFF_DH // 2
    inv_freq = jnp.power(ROPE_THETA, -jnp.arange(0, DIFF_DH, 2, dtype=F32) / DIFF_DH)
    ang = jnp.arange(T, dtype=F32)[:, None] * inv_freq[None, :]
    cos, sin = jnp.cos(ang), jnp.sin(ang)
    reps = HEAD_W // DIFF_DH
    return (jnp.concatenate([cos, cos] * reps, axis=-1),
            jnp.concatenate([-sin, sin] * reps, axis=-1))


def _layout_weights(w_in, gla_wg_f, gla_bg_f, gla_wg_b, gla_bg_b):
    q0, k0, v0, g0 = 0, 256, 512, 1024
    lr0, dq0 = 1536, 1568
    cols = []
    for h in range(GLA_HEADS):
        cols += [w_in[:, q0 + h * GLA_DK:q0 + (h + 1) * GLA_DK], w_in[:, k0 + h * GLA_DK:k0 + (h + 1) * GLA_DK]]
    cols += [w_in[:, v0:g0], w_in[:, g0:lr0], w_in[:, dq0:], w_in[:, lr0:dq0],
             jnp.zeros((D_MODEL, LANES - 2 * GLA_RANK), w_in.dtype)]
    w_r = jnp.concatenate(cols, axis=1).astype(BF16)
    wg = jnp.zeros((GLA_HEADS, LANES, HEAD_W), F32)
    bg = jnp.zeros((GLA_HEADS, 1, HEAD_W), F32)
    for h in range(GLA_HEADS):
        hs = slice(h * GLA_DK, (h + 1) * GLA_DK)
        wg = wg.at[h, :GLA_RANK, :GLA_DK].set(gla_wg_f[:, hs])
        wg = wg.at[h, GLA_RANK:2 * GLA_RANK, GLA_DK:].set(gla_wg_b[:, hs])
        bg = bg.at[h, 0, :GLA_DK].set(gla_bg_f[hs])
        bg = bg.at[h, 0, GLA_DK:].set(gla_bg_b[hs])
    return w_r, wg.astype(BF16), bg


def _lambda(lq1, lk1, lq2, lk2, lam_init):
    pad = lambda v: jnp.pad(v.astype(F32), (0, LANES - v.shape[0])).reshape(1, LANES)
    return pl.pallas_call(
        functools.partial(_lambda_kernel, lam_init=lam_init),
        out_shape=jax.ShapeDtypeStruct((1, 1), F32),
        name="lam",
    )(pad(lq1), pad(lk1), pad(lq2), pad(lk2))


def _encoder_layer(x, layer, lw, *, tm=512, qb=512):
    (norm1_g, w_in, gla_wg_f, gla_bg_f, gla_wg_b, gla_bg_b, gla_norm_g, qk_norm_q, qk_norm_k,
     lambda_q1, lambda_k1, lambda_q2, lambda_k2, diff_norm_g, w_out, norm2_g, w_router,
     w_gate, w_up, w_down) = lw
    B, T, D = x.shape
    n = B * T
    cap = max(1, EC_FACTOR * n // N_EXPERTS)
    tm = min(tm, n)
    qb = min(qb, T)
    x2 = x.reshape(n, D)
    w_r, wg, bg = _layout_weights(w_in, gla_wg_f, gla_bg_f, gla_wg_b, gla_bg_b)
    row = lambda v: v.astype(F32).reshape(1, -1)
    two = lambda v: jnp.concatenate([v, v]).astype(F32).reshape(1, -1)

    gqk, gv, gog, dq, dk, dv, lr = _proj(x2, row(norm1_g), w_r, tm)
    a = _gla(gqk, gv, gog, lr, wg, bg, row(gla_norm_g), B, T)
    cos, sin = _rope_tables(T)
    lam_init = _lambda_init(layer)
    lam = _lambda(lambda_q1, lambda_k1, lambda_q2, lambda_k2, lam_init)
    d = _dattn(dq, dk, dv, cos, sin, two(qk_norm_q), two(qk_norm_k), row(diff_norm_g), lam, B, T, qb, lam_init)
    x1, h2, aff_t = _mixout(a, d, x2, w_out.astype(BF16), row(norm2_g), w_router.T.astype(F32), tm)
    idx, gate = _route(aff_t, n, cap)
    y = _ffn(idx, gate, h2, x1, w_gate, w_up, w_down, cap)
    return y.reshape(B, T, D)


def kernel(x_prompt, x_sample, norm1_g, w_in, gla_wg_f, gla_bg_f, gla_wg_b, gla_bg_b, gla_norm_g, qk_norm_q, qk_norm_k, lambda_q1, lambda_k1, lambda_q2, lambda_k2, diff_norm_g, w_out, norm2_g, w_router, w_gate, w_up, w_down):
    y_prompt, y_sample = x_prompt, x_sample
    for l in range(norm1_g.shape[0]):
        lw = (norm1_g[l], w_in[l], gla_wg_f[l], gla_bg_f[l], gla_wg_b[l], gla_bg_b[l],
              gla_norm_g[l], qk_norm_q[l], qk_norm_k[l], lambda_q1[l], lambda_k1[l],
              lambda_q2[l], lambda_k2[l], diff_norm_g[l], w_out[l], norm2_g[l],
              w_router[l], w_gate[l], w_up[l], w_down[l])
        y_prompt = _encoder_layer(y_prompt, l, lw)
        y_sample = _encoder_layer(y_sample, l, lw)
    return (y_prompt, y_sample)
```

```python
import functools
import math

import jax
import jax.numpy as jnp
from jax import lax
from jax.experimental import pallas as pl
from jax.experimental.pallas import tpu as pltpu

F32 = jnp.float32
BF16 = jnp.bfloat16
I32 = jnp.int32

LANES = 128
VMEM_LIMIT = 56 * 1024 * 1024

D_MODEL = 1024
GLA_HEADS = 4
GLA_DK = 64
GLA_DV = 128
GLA_RANK = 16
GLA_TAU = 16.0
GLA_CHUNK = 64
DIFF_HEADS = 4
DIFF_DH = 64
DIFF_DV = 128
ROPE_THETA = 10000.0
N_EXPERTS = 16
EC_FACTOR = 2
D_EXPERT = 2 * D_MODEL
RMS_EPS = 1e-6
LOG2_E = 1.4426950408889634


def _lambda_init(layer):
    return 0.8 - 0.6 * math.exp(-0.3 * layer)


HEAD_W = 128
TOK_SUB = D_MODEL // LANES


def _tok_rows(s, ntok, first_tok=0):
    return pl.ds(first_tok * TOK_SUB + s, ntok, stride=TOK_SUB)
PROJ_GROUPS = 6
PROJ_W = PROJ_GROUPS * 4 * HEAD_W + LANES


def _dot(a, b):
    return jnp.dot(a, b, preferred_element_type=F32)


def _dot_nt(a, b):
    return lax.dot_general(a, b, (((1,), (1,)), ((), ())), preferred_element_type=F32)


def _dot_tn(a, b):
    return lax.dot_general(a, b, (((0,), (0,)), ((), ())), preferred_element_type=F32)


def _split3(x):
    a = x.astype(BF16)
    r = x - a.astype(F32)
    b = r.astype(BF16)
    c = (r - b.astype(F32)).astype(BF16)
    return a, b, c


def _params(sem):
    return pltpu.CompilerParams(dimension_semantics=sem, vmem_limit_bytes=VMEM_LIMIT)


def _proj_kernel(x_ref, g_ref, w_ref, gqk_ref, gv_ref, gog_ref, dq_ref, dk_ref, dv_ref, lr_ref):
    x = x_ref[...]
    ms = jnp.mean(x * x, axis=-1, keepdims=True)
    h = (x * lax.rsqrt(ms + RMS_EPS) * g_ref[...]).astype(BF16)
    gw = 4 * HEAD_W
    for i, o in enumerate((gqk_ref, gv_ref, gog_ref, dq_ref, dk_ref, dv_ref)):
        o[...] = _dot(h, w_ref[:, i * gw:(i + 1) * gw]).astype(o.dtype)
    lr_ref[...] = _dot(h, w_ref[:, PROJ_GROUPS * gw:])


def _proj(x2, g1, w_r, tm):
    n = x2.shape[0]
    gw = 4 * HEAD_W
    row = lambda i: (i, 0)
    outs = [jax.ShapeDtypeStruct((n, gw), BF16)] * PROJ_GROUPS + [jax.ShapeDtypeStruct((n, LANES), F32)]
    return pl.pallas_call(
        _proj_kernel,
        grid=(n // tm,),
        in_specs=[pl.BlockSpec((tm, D_MODEL), row),
                  pl.BlockSpec((1, D_MODEL), lambda i: (0, 0)),
                  pl.BlockSpec((D_MODEL, PROJ_W), lambda i: (0, 0))],
        out_specs=[pl.BlockSpec((tm, gw), row)] * PROJ_GROUPS + [pl.BlockSpec((tm, LANES), row)],
        out_shape=outs,
        compiler_params=_params(("arbitrary",)),
        name="proj",
    )(x2, g1, w_r)


GLA_BLK = 256


def _gla_kernel(gqk_ref, gv_ref, gog_ref, lr_ref, wg_ref, bg_ref, ng_ref, o_ref,
                qh_ref, u_ref, dec_ref, acc_ref, b_ref, tot_ref, *, T):
    C = GLA_CHUNK
    nblk = T // GLA_BLK
    nch = T // C
    lane = lax.broadcasted_iota(I32, (1, HEAD_W), 1)
    fwd = lane < GLA_DK
    ri = lax.broadcasted_iota(I32, (GLA_BLK, GLA_BLK), 0)
    ci = lax.broadcasted_iota(I32, (GLA_BLK, GLA_BLK), 1)
    same = (ri // C) == (ci // C)
    prefix_total = jnp.concatenate([jnp.where(same & (ci <= ri), 1.0, 0.0),
                                    jnp.where(same, 1.0, 0.0)], axis=0).astype(BF16)
    m_f = same & (ci <= ri)
    m_b = same & (ci > ri)
    wg = wg_ref[...]
    bg = bg_ref[...]
    cpb = GLA_BLK // C
    chunk_of_row = lax.broadcasted_iota(I32, (GLA_BLK, 1), 0) // C

    def gates(i, carry):
        rows = pl.ds(pl.multiple_of(i * GLA_BLK, GLA_BLK), GLA_BLK)
        z = _dot(lr_ref[rows, :].astype(BF16), wg) + bg
        la = (jnp.minimum(z, 0.0) - jnp.log1p(jnp.exp(-jnp.abs(z)))) * (1.0 / GLA_TAU)
        hi = la.astype(BF16)
        lo = (la - hi.astype(F32)).astype(BF16)
        pt = _dot(prefix_total, jnp.concatenate([hi, lo], axis=1))
        pt = pt[:, :HEAD_W] + pt[:, HEAD_W:]
        pre, tot = pt[:GLA_BLK, :], pt[GLA_BLK:, :]
        b_ref[rows, :] = jnp.where(fwd, pre, tot - pre + la)
        tot_ref[rows, :] = tot
        return carry

    lax.fori_loop(0, nblk, gates, 0, unroll=4)

    def block(i, carry):
        rows = pl.ds(pl.multiple_of(i * GLA_BLK, GLA_BLK), GLA_BLK)
        b = b_ref[rows, :]
        tot = tot_ref[rows, :]
        mid = 0.5 * tot
        blk = gqk_ref[rows, :].astype(F32)
        rot = pltpu.roll(blk, GLA_DK, 1)
        qq = jnp.where(fwd, blk, rot) * (GLA_DK ** -0.5)
        kk = jnp.where(fwd, rot, blk)
        qt = qq * jnp.exp(b - mid)
        kt = (kk * jnp.exp(mid - b)).astype(BF16)
        kh = (kk * jnp.exp(tot - b)).astype(BF16)
        qh_ref[rows, :] = (qq * jnp.exp(b)).astype(BF16)
        qfb = jnp.concatenate([jnp.where(fwd, qt, 0.0), jnp.where(fwd, 0.0, qt)], axis=0).astype(BF16)
        sfb = _dot_nt(qfb, kt)
        s = jnp.where(m_f, sfb[:GLA_BLK, :], jnp.where(m_b, sfb[GLA_BLK:, :], 0.0)).astype(BF16)
        v = gv_ref[rows, :]
        acc_ref[rows, :] = _dot(s, v)
        zero = jnp.zeros_like(kh)
        kx = jnp.concatenate([jnp.where(chunk_of_row == c, kh, zero) for c in range(cpb)], axis=1)
        ux = _dot_tn(v, kx)
        dec = jnp.exp(tot)
        for c in range(cpb):
            u_ref[i * cpb + c] = ux[:, c * HEAD_W:(c + 1) * HEAD_W]
            dec_ref[i * cpb + c] = jnp.broadcast_to(dec[c * C:c * C + 1, :], (8, HEAD_W))
        return carry

    lax.fori_loop(0, nblk, block, 0, unroll=4)

    def step(t, st):
        nf = t
        nb = nch - 1 - t
        stb = st.astype(BF16)
        rf = pl.ds(pl.multiple_of(nf * C, C), C)
        rb = pl.ds(pl.multiple_of(nb * C, C), C)
        qf = qh_ref[rf, :]
        qb = qh_ref[rb, :]
        zero = jnp.zeros_like(qf)
        acc_ref[rf, :] += _dot_nt(jnp.where(fwd, qf, zero), stb)
        acc_ref[rb, :] += _dot_nt(jnp.where(fwd, zero, qb), stb)
        d = jnp.where(fwd, dec_ref[nf][0:1, :], dec_ref[nb][0:1, :])
        u = jnp.where(fwd, u_ref[nf], u_ref[nb])
        return st * d + u

    lax.fori_loop(0, nch, step, jnp.zeros((GLA_DV, HEAD_W), F32), unroll=8)

    ng = ng_ref[...]
    eb = min(T, 512)

    def epi(i, carry):
        rows = pl.ds(pl.multiple_of(i * eb, eb), eb)
        o = acc_ref[rows, :]
        y = o * lax.rsqrt(jnp.mean(o * o, axis=-1, keepdims=True) + RMS_EPS) * ng
        og = gog_ref[rows, :].astype(F32)
        o_ref[rows, :] = (y * (og * jax.nn.sigmoid(og))).astype(o_ref.dtype)
        return carry

    lax.fori_loop(0, T // eb, epi, 0)


def _gla(gqk, gv, gog, lr, wg, bg, ng, B, T):
    n = B * T
    head = pl.BlockSpec((T, HEAD_W), lambda b, h: (b, h))
    return pl.pallas_call(
        functools.partial(_gla_kernel, T=T),
        grid=(B, GLA_HEADS),
        in_specs=[head, head, head,
                  pl.BlockSpec((T, LANES), lambda b, h: (b, 0)),
                  pl.BlockSpec((None, LANES, HEAD_W), lambda b, h: (h, 0, 0)),
                  pl.BlockSpec((None, 1, HEAD_W), lambda b, h: (h, 0, 0)),
                  pl.BlockSpec((1, HEAD_W), lambda b, h: (0, 0))],
        out_specs=head,
        out_shape=jax.ShapeDtypeStruct((n, GLA_HEADS * HEAD_W), BF16),
        scratch_shapes=[pltpu.VMEM((T, HEAD_W), BF16),
                        pltpu.VMEM((T // GLA_CHUNK, GLA_DV, HEAD_W), F32),
                        pltpu.VMEM((T // GLA_CHUNK, 8, HEAD_W), F32),
                        pltpu.VMEM((T, GLA_DV), F32),
                        pltpu.VMEM((T, HEAD_W), F32),
                        pltpu.VMEM((T, HEAD_W), F32)],
        compiler_params=_params(("arbitrary", "arbitrary")),
        name="gla",
    )(gqk, gv, gog, lr, wg, bg, ng)


DATTN_KSPLIT = 2


def _dot_hilo(x, m):
    hi = x.astype(BF16)
    lo = (x - hi.astype(F32)).astype(BF16)
    return _dot(hi, m) + _dot(lo, m)


def _qk_prep(x, gain, cos, sin):
    ri = lax.broadcasted_iota(I32, (HEAD_W, HEAD_W), 0)
    ci = lax.broadcasted_iota(I32, (HEAD_W, HEAD_W), 1)
    same_map = jnp.where((ri // DIFF_DH) == (ci // DIFF_DH), 1.0, 0.0).astype(BF16)
    swap = jnp.where(ri == (ci ^ (DIFF_DH // 2)), 1.0, 0.0).astype(BF16)
    ms = _dot_hilo(x * x, same_map) * (1.0 / DIFF_DH)
    y = x * lax.rsqrt(ms + RMS_EPS) * gain
    return y * cos + _dot_hilo(y, swap) * sin


def _dattn_kernel(dq_ref, dk_ref, dv_ref, cos_ref, sin_ref, qg_ref, kg_ref, ng_ref, lam_ref, o_ref,
                  kp_ref, s_ref, mp_ref, lp_ref, acc_ref, *, T, QB, TK, lam_init):
    qi = pl.program_id(2)
    lane = lax.broadcasted_iota(I32, (1, HEAD_W), 1)
    map0 = lane < DIFF_DH
    nk = T // TK

    @pl.when(qi == 0)
    def _():
        kb = min(T, 512)

        def kprep(i, carry):
            rows = pl.ds(pl.multiple_of(i * kb, kb), kb)
            k = _qk_prep(dk_ref[rows, :].astype(F32), kg_ref[...], cos_ref[rows, :], sin_ref[rows, :])
            kp_ref[rows, :] = k.astype(BF16)
            return carry
        lax.fori_loop(0, T // kb, kprep, 0)

    qrows = pl.ds(pl.multiple_of(qi * QB, QB), QB)
    q = _qk_prep(dq_ref[...].astype(F32), qg_ref[...], cos_ref[qrows, :], sin_ref[qrows, :])
    q = q * (DIFF_DH ** -0.5 * LOG2_E)
    qs = jnp.concatenate([jnp.where(map0, q, 0.0), jnp.where(map0, 0.0, q)], axis=0).astype(BF16)
    ntile = TK // LANES

    def lane_tiles(x):
        return [x[:, j * LANES:(j + 1) * LANES] for j in range(ntile)]

    mp_ref[...] = jnp.full(mp_ref.shape, -jnp.inf, F32)
    lp_ref[...] = jnp.zeros(lp_ref.shape, F32)
    acc_ref[...] = jnp.zeros(acc_ref.shape, F32)

    def scores(i, carry):
        cols = pl.ds(pl.multiple_of(i * TK, TK), TK)
        s = _dot_nt(qs, kp_ref[cols, :])
        s_ref[:, cols] = s
        mp_ref[...] = jnp.maximum(mp_ref[...], functools.reduce(jnp.maximum, lane_tiles(s)))
        return carry

    lax.fori_loop(0, nk, scores, 0)
    m = jnp.max(mp_ref[...], axis=-1, keepdims=True)

    def accum(i, carry):
        cols = pl.ds(pl.multiple_of(i * TK, TK), TK)
        p = jnp.exp2(s_ref[:, cols] - m)
        lp_ref[...] += functools.reduce(jnp.add, lane_tiles(p))
        acc_ref[...] += _dot(p.astype(BF16), dv_ref[cols, :])
        return carry

    lax.fori_loop(0, nk, accum, 0)
    on = acc_ref[...] / jnp.sum(lp_ref[...], axis=-1, keepdims=True)
    lam = lam_ref[...]
    o = on[:QB, :] - lam * on[QB:, :]
    y = o * lax.rsqrt(jnp.mean(o * o, axis=-1, keepdims=True) + RMS_EPS) * ng_ref[...]
    o_ref[...] = (y * (1.0 - lam_init)).astype(o_ref.dtype)


def _lambda_kernel(q1_ref, k1_ref, q2_ref, k2_ref, o_ref, *, lam_init):
    a = jnp.exp(jnp.sum(q1_ref[...] * k1_ref[...], axis=-1, keepdims=True))
    b = jnp.exp(jnp.sum(q2_ref[...] * k2_ref[...], axis=-1, keepdims=True))
    o_ref[...] = a - b + lam_init


def _dattn(dq, dk, dv, cos, sin, qg, kg, ng, lam, B, T, QB, lam_init):
    n = B * T
    TK = T // DATTN_KSPLIT
    nq = T // QB
    full = lambda shape: pl.BlockSpec(shape, lambda b, h, q: (0, 0))
    kv = pl.BlockSpec((T, HEAD_W), lambda b, h, q: (b, h))
    qo = pl.BlockSpec((QB, HEAD_W), lambda b, h, q: (b * nq + q, h))
    return pl.pallas_call(
        functools.partial(_dattn_kernel, T=T, QB=QB, TK=TK, lam_init=lam_init),
        grid=(B, DIFF_HEADS, nq),
        in_specs=[qo, kv, kv, full((T, HEAD_W)), full((T, HEAD_W)),
                  full((1, HEAD_W)), full((1, HEAD_W)), full((1, HEAD_W)), full((1, 1))],
        out_specs=qo,
        out_shape=jax.ShapeDtypeStruct((n, DIFF_HEADS * HEAD_W), BF16),
        scratch_shapes=[pltpu.VMEM((T, HEAD_W), BF16),
                        pltpu.VMEM((2 * QB, T), F32),
                        pltpu.VMEM((2 * QB, LANES), F32),
                        pltpu.VMEM((2 * QB, LANES), F32),
                        pltpu.VMEM((2 * QB, DIFF_DV), F32)],
        compiler_params=_params(("arbitrary", "arbitrary", "arbitrary")),
        name="dattn",
    )(dq, dk, dv, cos, sin, qg, kg, ng, lam)


def _mixout_kernel(a_ref, d_ref, x_ref, w_ref, g_ref, wr_ref, x1_ref, h2_ref, aff_ref):
    half = GLA_HEADS * HEAD_W
    y = _dot(a_ref[...], w_ref[:half, :]) + _dot(d_ref[...], w_ref[half:, :])
    x1 = x_ref[...] + y
    h2 = x1 * lax.rsqrt(jnp.mean(x1 * x1, axis=-1, keepdims=True) + RMS_EPS) * g_ref[...]
    for s in range(TOK_SUB):
        cols = slice(s * LANES, (s + 1) * LANES)
        x1_ref[_tok_rows(s, x1.shape[0]), :] = x1[:, cols]
        h2_ref[_tok_rows(s, x1.shape[0]), :] = h2[:, cols]
    logits = jnp.zeros((N_EXPERTS, h2.shape[0]), F32)
    hp = _split3(h2)
    wp = _split3(wr_ref[...])
    for i in range(3):
        for j in range(3 - i):
            logits = logits + _dot_nt(wp[i], hp[j])
    m = jnp.max(logits, axis=0, keepdims=True)
    e = jnp.exp(logits - m)
    aff_ref[...] = e / jnp.sum(e, axis=0, keepdims=True)


def _mixout(a, d, x2, w_out, g2, wr_t, tm):
    n = x2.shape[0]
    row = lambda i: (i, 0)
    half = GLA_HEADS * HEAD_W
    return pl.pallas_call(
        _mixout_kernel,
        grid=(n // tm,),
        in_specs=[pl.BlockSpec((tm, half), row), pl.BlockSpec((tm, half), row),
                  pl.BlockSpec((tm, D_MODEL), row),
                  pl.BlockSpec((D_MODEL, D_MODEL), lambda i: (0, 0)),
                  pl.BlockSpec((1, D_MODEL), lambda i: (0, 0)),
                  pl.BlockSpec((N_EXPERTS, D_MODEL), lambda i: (0, 0))],
        out_specs=[pl.BlockSpec((tm * TOK_SUB, LANES), row), pl.BlockSpec((tm * TOK_SUB, LANES), row),
                   pl.BlockSpec((N_EXPERTS, tm), lambda i: (0, i))],
        out_shape=[jax.ShapeDtypeStruct((n * TOK_SUB, LANES), F32), jax.ShapeDtypeStruct((n * TOK_SUB, LANES), F32),
                   jax.ShapeDtypeStruct((N_EXPERTS, n), F32)],
        compiler_params=_params(("arbitrary",)),
        name="mixout",
    )(a, d, x2, w_out, g2, wr_t)


ROUTE_BISECTIONS = 40


def _route_kernel(aff_ref, idx_ref, gate_ref, *, n, cap):
    R = n // LANES
    E = N_EXPERTS
    aff = aff_ref[...]

    def count(mask):
        c = jnp.sum(jnp.where(mask, 1.0, 0.0), axis=1, keepdims=True)
        return jnp.sum(c, axis=2, keepdims=True)

    capf = float(cap)
    k = jnp.zeros((E, 1, 1), F32)
    for bit in (128.0, 64.0, 32.0, 16.0, 8.0, 4.0, 2.0, 1.0):
        cand = k + bit
        k = jnp.where(count(aff >= jnp.exp2(1.0 - cand)) < capf, cand, k)
    hi = jnp.exp2(1.0 - k)
    lo = jnp.exp2(-k)

    def bisect(i, c):
        lo, hi = c
        mid = 0.5 * (lo + hi)
        ok = count(aff >= mid) >= capf
        return jnp.where(ok, mid, lo), jnp.where(ok, hi, mid)

    lo, hi = lax.fori_loop(0, ROUTE_BISECTIONS, bisect, (lo, hi))
    gt = aff >= hi
    eq = (aff >= lo) & (aff < hi)
    need = capf - count(gt)
    tok = (lax.broadcasted_iota(I32, (1, R, LANES), 1) * LANES
           + lax.broadcasted_iota(I32, (1, R, LANES), 2))
    nbits = max(1, (n - 1).bit_length())

    def tie_step(i, m0):
        cand = m0 | jnp.left_shift(jnp.int32(1), nbits - 1 - i)
        return jnp.where(count(eq & (tok < cand)) < need, cand, m0)

    m0 = lax.fori_loop(0, nbits, tie_step, jnp.zeros((E, 1, 1), I32))
    sel = gt | (eq & (tok <= m0))

    ci = lax.broadcasted_iota(I32, (LANES, LANES), 0)
    cj = lax.broadcasted_iota(I32, (LANES, LANES), 1)
    upper = jnp.where(ci <= cj, 1.0, 0.0).astype(BF16)
    ri = lax.broadcasted_iota(I32, (R, R), 0)
    rj = lax.broadcasted_iota(I32, (R, R), 1)
    lower = jnp.where(rj < ri, 1.0, 0.0).astype(BF16)
    jrow = lax.broadcasted_iota(I32, (1, cap), 1).astype(F32)
    rcol = lax.broadcasted_iota(I32, (R, 1), 0).astype(F32)
    ccol = lax.broadcasted_iota(I32, (LANES, 1), 0).astype(F32)

    for e in range(E):
        sel_e = sel[e]
        w = _dot(jnp.where(sel_e, 1.0, 0.0).astype(BF16), upper)
        rt = w[:, LANES - 1:LANES]
        rp = _dot(lower, jnp.broadcast_to(rt, (R, LANES)).astype(BF16))[:, 0:1]
        rows_t = jnp.where((rp <= jrow) & (jrow < rp + rt), 1.0, 0.0)
        k1 = jrow - jnp.sum(rows_t * rp, axis=0, keepdims=True) + 1.0
        rows_b = rows_t.astype(BF16)
        wrow = _dot(jnp.where(sel_e, w, 0.0).T.astype(BF16), rows_b)
        cols_t = jnp.where(wrow == k1, 1.0, 0.0)
        r_of = jnp.sum(rows_t * rcol, axis=0, keepdims=True)
        c_of = jnp.sum(cols_t * ccol, axis=0, keepdims=True)
        idx_ref[e] = (r_of * float(LANES) + c_of).astype(I32)
        a3 = _split3(aff[e].T)
        arow = _dot(a3[0], rows_b) + _dot(a3[1], rows_b) + _dot(a3[2], rows_b)
        gate_ref[e] = jnp.sum(cols_t * arow, axis=0, keepdims=True)


def _route(aff_t, n, cap):
    R = n // LANES
    return pl.pallas_call(
        functools.partial(_route_kernel, n=n, cap=cap),
        out_shape=[jax.ShapeDtypeStruct((N_EXPERTS, 1, cap), I32),
                   jax.ShapeDtypeStruct((N_EXPERTS, 1, cap), F32)],
        compiler_params=pltpu.CompilerParams(vmem_limit_bytes=VMEM_LIMIT),
        name="route",
    )(aff_t.reshape(N_EXPERTS, R, LANES))


FFN_FC = 512
FFN_TT = 512
FFN_DMA_UNROLL = 16


def _ffn_kernel(idx_ref, gate_ref, h2_hbm, wg_ref, wu_ref, wd_ref, x1_hbm, out_hbm,
                stage_ref, xb_ref, acc_ref, orow_ref, wgb_ref, wub_ref, wdb_ref, sems, *, cap, nfc):
    del x1_hbm
    e = pl.program_id(0)
    fc = pl.program_id(1)
    ne = pl.num_programs(0)
    g_sem, o_sem, s_sem = sems.at[0], sems.at[1], sems.at[2]
    tt = min(FFN_TT, cap)

    def row_copy(src, s_tok, dst, d_tok, sem):
        s_rows = pl.ds(pl.multiple_of(s_tok * TOK_SUB, TOK_SUB), TOK_SUB)
        d_rows = pl.ds(pl.multiple_of(d_tok * TOK_SUB, TOK_SUB), TOK_SUB)
        return pltpu.make_async_copy(src.at[s_rows], dst.at[d_rows], sem)

    def issue_gather(ee, src, dst, sem):
        def body(j, c):
            row_copy(src, idx_ref[ee * cap + j], dst, j, sem).start()
            return c
        lax.fori_loop(0, cap, body, 0, unroll=FFN_DMA_UNROLL)

    def wait_rows(buf, sem):
        pltpu.make_async_copy(buf, buf, sem).wait()

    gather_step = nfc - 2
    x_steps = [k for k in range(nfc) if k not in (0, gather_step)]
    assert 0 < gather_step and x_steps, "the copy schedule needs at least three hidden-dim steps"
    ntile = cap // tt
    per_tile = cap // ntile
    x_per_tile = cap // (len(x_steps) * ntile)
    e_next = jnp.minimum(e + 1, ne - 1)

    @pl.when(fc == 0)
    def _():
        @pl.when(e == 0)
        def _():
            issue_gather(0, h2_hbm, stage_ref, g_sem)
        wait_rows(stage_ref, g_sem)
        for s in range(TOK_SUB):
            xb_ref[:, s * LANES:(s + 1) * LANES] = stage_ref[_tok_rows(s, cap), :].astype(BF16)

    @pl.when((fc == gather_step) & (e > 0))
    def _():
        wait_rows(orow_ref, s_sem)

    wgb_ref[...] = wg_ref[...].astype(BF16)
    wub_ref[...] = wu_ref[...].astype(BF16)
    wdb_ref[...] = wd_ref[...].astype(BF16)

    def scatter_prev(t):
        for jj in range(per_tile):
            j = t * per_tile + jj
            row_copy(orow_ref, j, out_hbm, idx_ref[(e - 1) * cap + j], s_sem).start()

    def gather_out(t):
        for jj in range(per_tile):
            j = t * per_tile + jj
            row_copy(out_hbm, idx_ref[e * cap + j], orow_ref, j, o_sem).start()

    def gather_next(part):
        def issue(t):
            for jj in range(x_per_tile):
                j = (part * ntile + t) * x_per_tile + jj
                row_copy(h2_hbm, idx_ref[e_next * cap + j], stage_ref, j, g_sem).start()
        return issue

    def tile_loop(copies, first):
        def tile(t, c):
            if copies is not None:
                copies(t)
            rows = pl.ds(pl.multiple_of(t * tt, tt), tt)
            x = xb_ref[rows, :]
            a = _dot(x, wgb_ref[...])
            u = _dot(x, wub_ref[...])
            hm = (a * jax.nn.sigmoid(a) * u).astype(BF16)
            y = _dot(hm, wdb_ref[...])
            if first:
                acc_ref[rows, :] = y
            else:
                acc_ref[rows, :] += y
            return c
        lax.fori_loop(0, ntile, tile, 0)

    pl.when((fc == 0) & (e == 0))(lambda: tile_loop(None, True))
    pl.when((fc == 0) & (e > 0))(lambda: tile_loop(scatter_prev, True))
    pl.when(fc == gather_step)(lambda: tile_loop(gather_out, False))
    for part, k in enumerate(x_steps):
        pl.when(fc == k)(functools.partial(tile_loop, gather_next(part), False))

    @pl.when(fc == nfc - 1)
    def _():
        wait_rows(orow_ref, o_sem)
        for b in range(cap // LANES):
            rows = slice(b * LANES, (b + 1) * LANES)
            g = gate_ref[:, rows]
            gcol = jnp.broadcast_to(g, (LANES, LANES)).T
            for s in range(TOK_SUB):
                trows = _tok_rows(s, LANES, first_tok=b * LANES)
                orow_ref[trows, :] += acc_ref[rows, s * LANES:(s + 1) * LANES] * gcol

        @pl.when(e == ne - 1)
        def _():
            def body(j, c):
                row_copy(orow_ref, j, out_hbm, idx_ref[e * cap + j], s_sem).start()
                return c
            lax.fori_loop(0, cap, body, 0, unroll=FFN_DMA_UNROLL)
            wait_rows(orow_ref, s_sem)
            wait_rows(stage_ref, g_sem)


def _ffn(idx, gate, h2, x1, w_gate, w_up, w_down, cap):
    n = h2.shape[0] // TOK_SUB
    nfc = D_EXPERT // FFN_FC
    grid_spec = pltpu.PrefetchScalarGridSpec(
        num_scalar_prefetch=1,
        grid=(N_EXPERTS, nfc),
        in_specs=[pl.BlockSpec((None, 1, cap), lambda e, f, idx: (e, 0, 0)),
                  pl.BlockSpec(memory_space=pl.ANY),
                  pl.BlockSpec((None, D_MODEL, FFN_FC), lambda e, f, idx: (e, 0, f)),
                  pl.BlockSpec((None, D_MODEL, FFN_FC), lambda e, f, idx: (e, 0, f)),
                  pl.BlockSpec((None, FFN_FC, D_MODEL), lambda e, f, idx: (e, f, 0)),
                  pl.BlockSpec(memory_space=pl.ANY)],
        out_specs=pl.BlockSpec(memory_space=pl.ANY),
        scratch_shapes=[pltpu.VMEM((cap * TOK_SUB, LANES), F32),
                        pltpu.VMEM((cap, D_MODEL), BF16),
                        pltpu.VMEM((cap, D_MODEL), F32),
                        pltpu.VMEM((cap * TOK_SUB, LANES), F32),
                        pltpu.VMEM((D_MODEL, FFN_FC), BF16),
                        pltpu.VMEM((D_MODEL, FFN_FC), BF16),
                        pltpu.VMEM((FFN_FC, D_MODEL), BF16),
                        pltpu.SemaphoreType.DMA((3,))],
    )
    return pl.pallas_call(
        functools.partial(_ffn_kernel, cap=cap, nfc=nfc),
        grid_spec=grid_spec,
        out_shape=jax.ShapeDtypeStruct((n * TOK_SUB, LANES), F32),
        input_output_aliases={6: 0},
        compiler_params=_params(("arbitrary", "arbitrary")),
        name="ffn",
    )(idx.reshape(-1), gate, h2, w_gate, w_up, w_down, x1)


def _untile_kernel(x_ref, o_ref):
    for s in range(TOK_SUB):
        o_ref[:, s * LANES:(s + 1) * LANES] = x_ref[_tok_rows(s, o_ref.shape[0]), :]


def _untile(xt, tm):
    n = xt.shape[0] // TOK_SUB
    return pl.pallas_call(
        _untile_kernel,
        grid=(n // tm,),
        in_specs=[pl.BlockSpec((tm * TOK_SUB, LANES), lambda i: (i, 0))],
        out_specs=pl.BlockSpec((tm, D_MODEL), lambda i: (i, 0)),
        out_shape=jax.ShapeDtypeStruct((n, D_MODEL), F32),
        compiler_params=_params(("arbitrary",)),
        name="untile",
    )(xt)


def _rope_tables(T):
    half = DIFF_DH // 2
    inv_freq = jnp.power(ROPE_THETA, -jnp.arange(0, DIFF_DH, 2, dtype=F32) / DIFF_DH)
    ang = jnp.arange(T, dtype=F32)[:, None] * inv_freq[None, :]
    cos, sin = jnp.cos(ang), jnp.sin(ang)
    reps = HEAD_W // DIFF_DH
    return (jnp.concatenate([cos, cos] * reps, axis=-1),
            jnp.concatenate([-sin, sin] * reps, axis=-1))


def _layout_weights(w_in, gla_wg_f, gla_bg_f, gla_wg_b, gla_bg_b):
    q0, k0, v0, g0 = 0, 256, 512, 1024
    lr0, dq0 = 1536, 1568
    cols = []
    for h in range(GLA_HEADS):
        cols += [w_in[:, q0 + h * GLA_DK:q0 + (h + 1) * GLA_DK], w_in[:, k0 + h * GLA_DK:k0 + (h + 1) * GLA_DK]]
    cols += [w_in[:, v0:g0], w_in[:, g0:lr0], w_in[:, dq0:], w_in[:, lr0:dq0],
             jnp.zeros((D_MODEL, LANES - 2 * GLA_RANK), w_in.dtype)]
    w_r = jnp.concatenate(cols, axis=1).astype(BF16)
    wg = jnp.zeros((GLA_HEADS, LANES, HEAD_W), F32)
    bg = jnp.zeros((GLA_HEADS, 1, HEAD_W), F32)
    for h in range(GLA_HEADS):
        hs = slice(h * GLA_DK, (h + 1) * GLA_DK)
        wg = wg.at[h, :GLA_RANK, :GLA_DK].set(gla_wg_f[:, hs])
        wg = wg.at[h, GLA_RANK:2 * GLA_RANK, GLA_DK:].set(gla_wg_b[:, hs])
        bg = bg.at[h, 0, :GLA_DK].set(gla_bg_f[hs])
        bg = bg.at[h, 0, GLA_DK:].set(gla_bg_b[hs])
    return w_r, wg.astype(BF16), bg


def _lambda(lq1, lk1, lq2, lk2, lam_init):
    pad = lambda v: jnp.pad(v.astype(F32), (0, LANES - v.shape[0])).reshape(1, LANES)
    return pl.pallas_call(
        functools.partial(_lambda_kernel, lam_init=lam_init),
        out_shape=jax.ShapeDtypeStruct((1, 1), F32),
        name="lam",
    )(pad(lq1), pad(lk1), pad(lq2), pad(lk2))


def _encoder_layer(x, layer, lw, *, tm=512, qb=512):
    (norm1_g, w_in, gla_wg_f, gla_bg_f, gla_wg_b, gla_bg_b, gla_norm_g, qk_norm_q, qk_norm_k,
     lambda_q1, lambda_k1, lambda_q2, lambda_k2, diff_norm_g, w_out, norm2_g, w_router,
     w_gate, w_up, w_down) = lw
    B, T, D = x.shape
    n = B * T
    cap = max(1, EC_FACTOR * n // N_EXPERTS)
    tm = min(tm, n)
    qb = min(qb, T)
    x2 = x.reshape(n, D)
    w_r, wg, bg = _layout_weights(w_in, gla_wg_f, gla_bg_f, gla_wg_b, gla_bg_b)
    row = lambda v: v.astype(F32).reshape(1, -1)
    two = lambda v: jnp.concatenate([v, v]).astype(F32).reshape(1, -1)

    gqk, gv, gog, dq, dk, dv, lr = _proj(x2, row(norm1_g), w_r, tm)
    a = _gla(gqk, gv, gog, lr, wg, bg, row(gla_norm_g), B, T)
    cos, sin = _rope_tables(T)
    lam_init = _lambda_init(layer)
    lam = _lambda(lambda_q1, lambda_k1, lambda_q2, lambda_k2, lam_init)
    d = _dattn(dq, dk, dv, cos, sin, two(qk_norm_q), two(qk_norm_k), row(diff_norm_g), lam, B, T, qb, lam_init)
    x1, h2, aff_t = _mixout(a, d, x2, w_out.astype(BF16), row(norm2_g), w_router.T.astype(F32), tm)
    idx, gate = _route(aff_t, n, cap)
    y = _untile(_ffn(idx, gate, h2, x1, w_gate, w_up, w_down, cap), tm)
    return y.reshape(B, T, D)


def kernel(x_prompt, x_sample, norm1_g, w_in, gla_wg_f, gla_bg_f, gla_wg_b, gla_bg_b, gla_norm_g, qk_norm_q, qk_norm_k, lambda_q1, lambda_k1, lambda_q2, lambda_k2, diff_norm_g, w_out, norm2_g, w_router, w_gate, w_up, w_down):
    y_prompt, y_sample = x_prompt, x_sample
    for l in range(norm1_g.shape[0]):
        lw = (norm1_g[l], w_in[l], gla_wg_f[l], gla_bg_f[l], gla_wg_b[l], gla_bg_b[l],
              gla_norm_g[l], qk_norm_q[l], qk_norm_k[l], lambda_q1[l], lambda_k1[l],
              lambda_q2[l], lambda_k2[l], diff_norm_g[l], w_out[l], norm2_g[l],
              w_router[l], w_gate[l], w_up[l], w_down[l])
        y_prompt = _encoder_layer(y_prompt, l, lw)
        y_sample = _encoder_layer(y_sample, l, lw)
    return (y_prompt, y_sample)
```

```python
import functools
import math

import jax
import jax.numpy as jnp
from jax import lax
from jax.experimental import pallas as pl
from jax.experimental.pallas import tpu as pltpu

F32 = jnp.float32
BF16 = jnp.bfloat16
I32 = jnp.int32

LANES = 128
VMEM_LIMIT = 56 * 1024 * 1024

D_MODEL = 1024
GLA_HEADS = 4
GLA_DK = 64
GLA_DV = 128
GLA_RANK = 16
GLA_TAU = 16.0
GLA_CHUNK = 64
DIFF_HEADS = 4
DIFF_DH = 64
DIFF_DV = 128
ROPE_THETA = 10000.0
N_EXPERTS = 16
EC_FACTOR = 2
D_EXPERT = 2 * D_MODEL
RMS_EPS = 1e-6
LOG2_E = 1.4426950408889634


def _lambda_init(layer):
    return 0.8 - 0.6 * math.exp(-0.3 * layer)


HEAD_W = 128
TOK_SUB = D_MODEL // LANES


def _tok_rows(s, ntok, first_tok=0):
    return pl.ds(first_tok * TOK_SUB + s, ntok, stride=TOK_SUB)
PROJ_GROUPS = 6
PROJ_W = PROJ_GROUPS * 4 * HEAD_W + LANES


def _dot(a, b):
    return jnp.dot(a, b, preferred_element_type=F32)


def _dot_nt(a, b):
    return lax.dot_general(a, b, (((1,), (1,)), ((), ())), preferred_element_type=F32)


def _dot_tn(a, b):
    return lax.dot_general(a, b, (((0,), (0,)), ((), ())), preferred_element_type=F32)


def _split3(x):
    a = x.astype(BF16)
    r = x - a.astype(F32)
    b = r.astype(BF16)
    c = (r - b.astype(F32)).astype(BF16)
    return a, b, c


def _params(sem):
    return pltpu.CompilerParams(dimension_semantics=sem, vmem_limit_bytes=VMEM_LIMIT)


def _proj_kernel(x_ref, g_ref, w_ref, gqk_ref, gv_ref, gog_ref, dq_ref, dk_ref, dv_ref, lr_ref):
    x = x_ref[...]
    ms = jnp.mean(x * x, axis=-1, keepdims=True)
    h = (x * lax.rsqrt(ms + RMS_EPS) * g_ref[...]).astype(BF16)
    gw = 4 * HEAD_W
    for i, o in enumerate((gqk_ref, gv_ref, gog_ref, dq_ref, dk_ref, dv_ref)):
        o[...] = _dot(h, w_ref[:, i * gw:(i + 1) * gw]).astype(o.dtype)
    lr_ref[...] = _dot(h, w_ref[:, PROJ_GROUPS * gw:])


def _proj(x2, g1, w_r, tm):
    n = x2.shape[0]
    gw = 4 * HEAD_W
    row = lambda i: (i, 0)
    outs = [jax.ShapeDtypeStruct((n, gw), BF16)] * PROJ_GROUPS + [jax.ShapeDtypeStruct((n, LANES), F32)]
    return pl.pallas_call(
        _proj_kernel,
        grid=(n // tm,),
        in_specs=[pl.BlockSpec((tm, D_MODEL), row),
                  pl.BlockSpec((1, D_MODEL), lambda i: (0, 0)),
                  pl.BlockSpec((D_MODEL, PROJ_W), lambda i: (0, 0))],
        out_specs=[pl.BlockSpec((tm, gw), row)] * PROJ_GROUPS + [pl.BlockSpec((tm, LANES), row)],
        out_shape=outs,
        compiler_params=_params(("arbitrary",)),
        name="proj",
    )(x2, g1, w_r)


GLA_BLK = 256


def _gla_kernel(gqk_ref, gv_ref, gog_ref, lr_ref, wg_ref, bg_ref, ng_ref, o_ref,
                qh_ref, u_ref, dec_ref, acc_ref, b_ref, tot_ref, *, T):
    C = GLA_CHUNK
    nblk = T // GLA_BLK
    nch = T // C
    lane = lax.broadcasted_iota(I32, (1, HEAD_W), 1)
    fwd = lane < GLA_DK
    ri = lax.broadcasted_iota(I32, (GLA_BLK, GLA_BLK), 0)
    ci = lax.broadcasted_iota(I32, (GLA_BLK, GLA_BLK), 1)
    same = (ri // C) == (ci // C)
    prefix_total = jnp.concatenate([jnp.where(same & (ci <= ri), 1.0, 0.0),
                                    jnp.where(same, 1.0, 0.0)], axis=0).astype(BF16)
    m_f = same & (ci <= ri)
    m_b = same & (ci > ri)
    wg = wg_ref[...]
    bg = bg_ref[...]
    cpb = GLA_BLK // C
    chunk_of_row = lax.broadcasted_iota(I32, (GLA_BLK, 1), 0) // C

    def gates(i, carry):
        rows = pl.ds(pl.multiple_of(i * GLA_BLK, GLA_BLK), GLA_BLK)
        z = _dot(lr_ref[rows, :].astype(BF16), wg) + bg
        la = (jnp.minimum(z, 0.0) - jnp.log(1.0 + jnp.exp(-jnp.abs(z)))) * (1.0 / GLA_TAU)
        hi = la.astype(BF16)
        lo = (la - hi.astype(F32)).astype(BF16)
        pt = _dot(prefix_total, jnp.concatenate([hi, lo], axis=1))
        pt = pt[:, :HEAD_W] + pt[:, HEAD_W:]
        pre, tot = pt[:GLA_BLK, :], pt[GLA_BLK:, :]
        b_ref[rows, :] = jnp.where(fwd, pre, tot - pre + la)
        tot_ref[rows, :] = tot
        return carry

    lax.fori_loop(0, nblk, gates, 0, unroll=4)

    def block(i, carry):
        rows = pl.ds(pl.multiple_of(i * GLA_BLK, GLA_BLK), GLA_BLK)
        b = b_ref[rows, :]
        tot = tot_ref[rows, :]
        mid = 0.5 * tot
        blk = gqk_ref[rows, :].astype(F32)
        rot = pltpu.roll(blk, GLA_DK, 1)
        qq = jnp.where(fwd, blk, rot) * (GLA_DK ** -0.5)
        kk = jnp.where(fwd, rot, blk)
        qt = qq * jnp.exp(b - mid)
        kt = (kk * jnp.exp(mid - b)).astype(BF16)
        kh = (kk * jnp.exp(tot - b)).astype(BF16)
        qh_ref[rows, :] = (qq * jnp.exp(b)).astype(BF16)
        qfb = jnp.concatenate([jnp.where(fwd, qt, 0.0), jnp.where(fwd, 0.0, qt)], axis=0).astype(BF16)
        sfb = _dot_nt(qfb, kt)
        s = jnp.where(m_f, sfb[:GLA_BLK, :], jnp.where(m_b, sfb[GLA_BLK:, :], 0.0)).astype(BF16)
        v = gv_ref[rows, :]
        acc_ref[rows, :] = _dot(s, v)
        zero = jnp.zeros_like(kh)
        kx = jnp.concatenate([jnp.where(chunk_of_row == c, kh, zero) for c in range(cpb)], axis=1)
        ux = _dot_tn(v, kx)
        dec = jnp.exp(tot)
        for c in range(cpb):
            u_ref[i * cpb + c] = ux[:, c * HEAD_W:(c + 1) * HEAD_W]
            dec_ref[i * cpb + c] = jnp.broadcast_to(dec[c * C:c * C + 1, :], (8, HEAD_W))
        return carry

    lax.fori_loop(0, nblk, block, 0, unroll=4)

    def step(t, st):
        nf = t
        nb = nch - 1 - t
        stb = st.astype(BF16)
        rf = pl.ds(pl.multiple_of(nf * C, C), C)
        rb = pl.ds(pl.multiple_of(nb * C, C), C)
        qf = qh_ref[rf, :]
        qb = qh_ref[rb, :]
        zero = jnp.zeros_like(qf)
        acc_ref[rf, :] += _dot_nt(jnp.where(fwd, qf, zero), stb)
        acc_ref[rb, :] += _dot_nt(jnp.where(fwd, zero, qb), stb)
        d = jnp.where(fwd, dec_ref[nf][0:1, :], dec_ref[nb][0:1, :])
        u = jnp.where(fwd, u_ref[nf], u_ref[nb])
        return st * d + u

    lax.fori_loop(0, nch, step, jnp.zeros((GLA_DV, HEAD_W), F32), unroll=8)

    ng = ng_ref[...]
    eb = min(T, 512)

    def epi(i, carry):
        rows = pl.ds(pl.multiple_of(i * eb, eb), eb)
        o = acc_ref[rows, :]
        y = o * lax.rsqrt(jnp.mean(o * o, axis=-1, keepdims=True) + RMS_EPS) * ng
        og = gog_ref[rows, :].astype(F32)
        o_ref[rows, :] = (y * (og * jax.nn.sigmoid(og))).astype(o_ref.dtype)
        return carry

    lax.fori_loop(0, T // eb, epi, 0)


def _gla(gqk, gv, gog, lr, wg, bg, ng, B, T):
    n = B * T
    head = pl.BlockSpec((T, HEAD_W), lambda b, h: (b, h))
    return pl.pallas_call(
        functools.partial(_gla_kernel, T=T),
        grid=(B, GLA_HEADS),
        in_specs=[head, head, head,
                  pl.BlockSpec((T, LANES), lambda b, h: (b, 0)),
                  pl.BlockSpec((None, LANES, HEAD_W), lambda b, h: (h, 0, 0)),
                  pl.BlockSpec((None, 1, HEAD_W), lambda b, h: (h, 0, 0)),
                  pl.BlockSpec((1, HEAD_W), lambda b, h: (0, 0))],
        out_specs=head,
        out_shape=jax.ShapeDtypeStruct((n, GLA_HEADS * HEAD_W), BF16),
        scratch_shapes=[pltpu.VMEM((T, HEAD_W), BF16),
                        pltpu.VMEM((T // GLA_CHUNK, GLA_DV, HEAD_W), F32),
                        pltpu.VMEM((T // GLA_CHUNK, 8, HEAD_W), F32),
                        pltpu.VMEM((T, GLA_DV), F32),
                        pltpu.VMEM((T, HEAD_W), F32),
                        pltpu.VMEM((T, HEAD_W), F32)],
        compiler_params=_params(("arbitrary", "arbitrary")),
        name="gla",
    )(gqk, gv, gog, lr, wg, bg, ng)


DATTN_KSPLIT = 2


def _dot_hilo(x, m):
    hi = x.astype(BF16)
    lo = (x - hi.astype(F32)).astype(BF16)
    return _dot(hi, m) + _dot(lo, m)


def _qk_prep(x, gain, cos, sin):
    ri = lax.broadcasted_iota(I32, (HEAD_W, HEAD_W), 0)
    ci = lax.broadcasted_iota(I32, (HEAD_W, HEAD_W), 1)
    same_map = jnp.where((ri // DIFF_DH) == (ci // DIFF_DH), 1.0, 0.0).astype(BF16)
    swap = jnp.where(ri == (ci ^ (DIFF_DH // 2)), 1.0, 0.0).astype(BF16)
    ms = _dot_hilo(x * x, same_map) * (1.0 / DIFF_DH)
    y = x * lax.rsqrt(ms + RMS_EPS) * gain
    return y * cos + _dot_hilo(y, swap) * sin


def _dattn_kernel(dq_ref, dk_ref, dv_ref, cos_ref, sin_ref, qg_ref, kg_ref, ng_ref, lam_ref, o_ref,
                  kp_ref, s_ref, mp_ref, lp_ref, acc_ref, *, T, QB, TK, lam_init):
    qi = pl.program_id(2)
    lane = lax.broadcasted_iota(I32, (1, HEAD_W), 1)
    map0 = lane < DIFF_DH
    nk = T // TK

    @pl.when(qi == 0)
    def _():
        kb = min(T, 512)

        def kprep(i, carry):
            rows = pl.ds(pl.multiple_of(i * kb, kb), kb)
            k = _qk_prep(dk_ref[rows, :].astype(F32), kg_ref[...], cos_ref[rows, :], sin_ref[rows, :])
            kp_ref[rows, :] = k.astype(BF16)
            return carry
        lax.fori_loop(0, T // kb, kprep, 0)

    qrows = pl.ds(pl.multiple_of(qi * QB, QB), QB)
    q = _qk_prep(dq_ref[...].astype(F32), qg_ref[...], cos_ref[qrows, :], sin_ref[qrows, :])
    q = q * (DIFF_DH ** -0.5 * LOG2_E)
    qs = jnp.concatenate([jnp.where(map0, q, 0.0), jnp.where(map0, 0.0, q)], axis=0).astype(BF16)
    ntile = TK // LANES

    def lane_tiles(x):
        return [x[:, j * LANES:(j + 1) * LANES] for j in range(ntile)]

    mp_ref[...] = jnp.full(mp_ref.shape, -jnp.inf, F32)
    lp_ref[...] = jnp.zeros(lp_ref.shape, F32)
    acc_ref[...] = jnp.zeros(acc_ref.shape, F32)

    def scores(i, carry):
        cols = pl.ds(pl.multiple_of(i * TK, TK), TK)
        s = _dot_nt(qs, kp_ref[cols, :])
        s_ref[:, cols] = s
        mp_ref[...] = jnp.maximum(mp_ref[...], functools.reduce(jnp.maximum, lane_tiles(s)))
        return carry

    lax.fori_loop(0, nk, scores, 0)
    m = jnp.max(mp_ref[...], axis=-1, keepdims=True)

    def accum(i, carry):
        cols = pl.ds(pl.multiple_of(i * TK, TK), TK)
        p = jnp.exp2(s_ref[:, cols] - m)
        lp_ref[...] += functools.reduce(jnp.add, lane_tiles(p))
        acc_ref[...] += _dot(p.astype(BF16), dv_ref[cols, :])
        return carry

    lax.fori_loop(0, nk, accum, 0)
    on = acc_ref[...] / jnp.sum(lp_ref[...], axis=-1, keepdims=True)
    lam = lam_ref[...]
    o = on[:QB, :] - lam * on[QB:, :]
    y = o * lax.rsqrt(jnp.mean(o * o, axis=-1, keepdims=True) + RMS_EPS) * ng_ref[...]
    o_ref[...] = (y * (1.0 - lam_init)).astype(o_ref.dtype)


def _lambda_kernel(q1_ref, k1_ref, q2_ref, k2_ref, o_ref, *, lam_init):
    a = jnp.exp(jnp.sum(q1_ref[...] * k1_ref[...], axis=-1, keepdims=True))
    b = jnp.exp(jnp.sum(q2_ref[...] * k2_ref[...], axis=-1, keepdims=True))
    o_ref[...] = a - b + lam_init


def _dattn(dq, dk, dv, cos, sin, qg, kg, ng, lam, B, T, QB, lam_init):
    n = B * T
    TK = T // DATTN_KSPLIT
    nq = T // QB
    full = lambda shape: pl.BlockSpec(shape, lambda b, h, q: (0, 0))
    kv = pl.BlockSpec((T, HEAD_W), lambda b, h, q: (b, h))
    qo = pl.BlockSpec((QB, HEAD_W), lambda b, h, q: (b * nq + q, h))
    return pl.pallas_call(
        functools.partial(_dattn_kernel, T=T, QB=QB, TK=TK, lam_init=lam_init),
        grid=(B, DIFF_HEADS, nq),
        in_specs=[qo, kv, kv, full((T, HEAD_W)), full((T, HEAD_W)),
                  full((1, HEAD_W)), full((1, HEAD_W)), full((1, HEAD_W)), full((1, 1))],
        out_specs=qo,
        out_shape=jax.ShapeDtypeStruct((n, DIFF_HEADS * HEAD_W), BF16),
        scratch_shapes=[pltpu.VMEM((T, HEAD_W), BF16),
                        pltpu.VMEM((2 * QB, T), F32),
                        pltpu.VMEM((2 * QB, LANES), F32),
                        pltpu.VMEM((2 * QB, LANES), F32),
                        pltpu.VMEM((2 * QB, DIFF_DV), F32)],
        compiler_params=_params(("arbitrary", "arbitrary", "arbitrary")),
        name="dattn",
    )(dq, dk, dv, cos, sin, qg, kg, ng, lam)


def _mixout_kernel(a_ref, d_ref, x_ref, w_ref, g_ref, wr_ref, x1_ref, h2_ref, aff_ref):
    half = GLA_HEADS * HEAD_W
    y = _dot(a_ref[...], w_ref[:half, :]) + _dot(d_ref[...], w_ref[half:, :])
    x1 = x_ref[...] + y
    h2 = x1 * lax.rsqrt(jnp.mean(x1 * x1, axis=-1, keepdims=True) + RMS_EPS) * g_ref[...]
    for s in range(TOK_SUB):
        cols = slice(s * LANES, (s + 1) * LANES)
        x1_ref[_tok_rows(s, x1.shape[0]), :] = x1[:, cols]
        h2_ref[_tok_rows(s, x1.shape[0]), :] = h2[:, cols]
    logits = jnp.zeros((N_EXPERTS, h2.shape[0]), F32)
    hp = _split3(h2)
    wp = _split3(wr_ref[...])
    for i, j in ((0, 0), (0, 1), (1, 0)):
        logits = logits + _dot_nt(wp[i], hp[j])
    m = jnp.max(logits, axis=0, keepdims=True)
    e = jnp.exp(logits - m)
    aff_ref[...] = e / jnp.sum(e, axis=0, keepdims=True)


def _mixout(a, d, x2, w_out, g2, wr_t, tm):
    n = x2.shape[0]
    row = lambda i: (i, 0)
    half = GLA_HEADS * HEAD_W
    return pl.pallas_call(
        _mixout_kernel,
        grid=(n // tm,),
        in_specs=[pl.BlockSpec((tm, half), row), pl.BlockSpec((tm, half), row),
                  pl.BlockSpec((tm, D_MODEL), row),
                  pl.BlockSpec((D_MODEL, D_MODEL), lambda i: (0, 0)),
                  pl.BlockSpec((1, D_MODEL), lambda i: (0, 0)),
                  pl.BlockSpec((N_EXPERTS, D_MODEL), lambda i: (0, 0))],
        out_specs=[pl.BlockSpec((tm * TOK_SUB, LANES), row), pl.BlockSpec((tm * TOK_SUB, LANES), row),
                   pl.BlockSpec((N_EXPERTS, tm), lambda i: (0, i))],
        out_shape=[jax.ShapeDtypeStruct((n * TOK_SUB, LANES), F32), jax.ShapeDtypeStruct((n * TOK_SUB, LANES), F32),
                   jax.ShapeDtypeStruct((N_EXPERTS, n), F32)],
        compiler_params=_params(("arbitrary",)),
        name="mixout",
    )(a, d, x2, w_out, g2, wr_t)


ROUTE_BISECTIONS = 40


def _route_kernel(aff_ref, idx_ref, gate_ref, *, n, cap):
    R = n // LANES
    E = N_EXPERTS
    aff = aff_ref[...]

    def count(mask):
        c = jnp.sum(jnp.where(mask, 1.0, 0.0), axis=1, keepdims=True)
        return jnp.sum(c, axis=2, keepdims=True)

    capf = float(cap)
    k = jnp.zeros((E, 1, 1), F32)
    for bit in (64.0, 32.0, 16.0, 8.0, 4.0, 2.0, 1.0):
        cand = k + bit
        k = jnp.where(count(aff >= jnp.exp2(1.0 - cand)) < capf, cand, k)
    hi = jnp.exp2(1.0 - k)
    lo = jnp.where(k >= 127.0, 0.0, jnp.exp2(-jnp.minimum(k, 126.0)))

    def bisect(i, c):
        lo, hi = c
        mid = 0.5 * (lo + hi)
        ok = count(aff >= mid) >= capf
        return jnp.where(ok, mid, lo), jnp.where(ok, hi, mid)

    lo, hi = lax.fori_loop(0, ROUTE_BISECTIONS, bisect, (lo, hi))
    gt = aff >= hi
    eq = (aff >= lo) & (aff < hi)
    need = capf - count(gt)
    tok = (lax.broadcasted_iota(I32, (1, R, LANES), 1) * LANES
           + lax.broadcasted_iota(I32, (1, R, LANES), 2))
    nbits = max(1, (n - 1).bit_length())

    def tie_step(i, m0):
        cand = m0 | jnp.left_shift(jnp.int32(1), nbits - 1 - i)
        return jnp.where(count(eq & (tok < cand)) < need, cand, m0)

    m0 = lax.fori_loop(0, nbits, tie_step, jnp.zeros((E, 1, 1), I32))
    sel = gt | (eq & (tok <= m0))

    ci = lax.broadcasted_iota(I32, (LANES, LANES), 0)
    cj = lax.broadcasted_iota(I32, (LANES, LANES), 1)
    upper = jnp.where(ci <= cj, 1.0, 0.0).astype(BF16)
    ri = lax.broadcasted_iota(I32, (R, R), 0)
    rj = lax.broadcasted_iota(I32, (R, R), 1)
    lower = jnp.where(rj < ri, 1.0, 0.0).astype(BF16)
    jrow = lax.broadcasted_iota(I32, (1, cap), 1).astype(F32)
    rcol = lax.broadcasted_iota(I32, (R, 1), 0).astype(F32)
    ccol = lax.broadcasted_iota(I32, (LANES, 1), 0).astype(F32)

    for e in range(E):
        sel_e = sel[e]
        w = _dot(jnp.where(sel_e, 1.0, 0.0).astype(BF16), upper)
        rt = w[:, LANES - 1:LANES]
        rp = _dot(lower, jnp.broadcast_to(rt, (R, LANES)).astype(BF16))[:, 0:1]
        rows_t = jnp.where((rp <= jrow) & (jrow < rp + rt), 1.0, 0.0)
        k1 = jrow - jnp.sum(rows_t * rp, axis=0, keepdims=True) + 1.0
        rows_b = rows_t.astype(BF16)
        wrow = _dot(jnp.where(sel_e, w, 0.0).T.astype(BF16), rows_b)
        cols_t = jnp.where(wrow == k1, 1.0, 0.0)
        r_of = jnp.sum(rows_t * rcol, axis=0, keepdims=True)
        c_of = jnp.sum(cols_t * ccol, axis=0, keepdims=True)
        idx_ref[e] = (r_of * float(LANES) + c_of).astype(I32)
        a3 = _split3(aff[e].T)
        arow = _dot(a3[0], rows_b) + _dot(a3[1], rows_b) + _dot(a3[2], rows_b)
        gate_ref[e] = jnp.sum(cols_t * arow, axis=0, keepdims=True)


def _route(aff_t, n, cap):
    R = n // LANES
    return pl.pallas_call(
        functools.partial(_route_kernel, n=n, cap=cap),
        out_shape=[jax.ShapeDtypeStruct((N_EXPERTS, 1, cap), I32),
                   jax.ShapeDtypeStruct((N_EXPERTS, 1, cap), F32)],
        compiler_params=pltpu.CompilerParams(vmem_limit_bytes=VMEM_LIMIT),
        name="route",
    )(aff_t.reshape(N_EXPERTS, R, LANES))


FFN_FC = 512
FFN_TT = 512
FFN_DMA_UNROLL = 16


def _ffn_kernel(idx_ref, gate_ref, h2_hbm, wg_ref, wu_ref, wd_ref, x1_hbm, out_hbm,
                stage_ref, xb_ref, acc_ref, orow_ref, sems, *, cap, nfc):
    del x1_hbm
    e = pl.program_id(0)
    fc = pl.program_id(1)
    ne = pl.num_programs(0)
    g_sem, o_sem, s_sem = sems.at[0], sems.at[1], sems.at[2]
    tt = min(FFN_TT, cap)

    def row_copy(src, s_tok, dst, d_tok, sem):
        s_rows = pl.ds(pl.multiple_of(s_tok * TOK_SUB, TOK_SUB), TOK_SUB)
        d_rows = pl.ds(pl.multiple_of(d_tok * TOK_SUB, TOK_SUB), TOK_SUB)
        return pltpu.make_async_copy(src.at[s_rows], dst.at[d_rows], sem)

    def issue_gather(ee, src, dst, sem):
        def body(j, c):
            row_copy(src, idx_ref[ee * cap + j], dst, j, sem).start()
            return c
        lax.fori_loop(0, cap, body, 0, unroll=FFN_DMA_UNROLL)

    def wait_rows(buf, sem):
        pltpu.make_async_copy(buf, buf, sem).wait()

    gather_step = nfc - 2
    x_steps = [k for k in range(nfc) if k not in (0, gather_step)]
    assert 0 < gather_step and x_steps, "the copy schedule needs at least three hidden-dim steps"
    ntile = cap // tt
    per_tile = cap // ntile
    x_per_tile = cap // (len(x_steps) * ntile)
    e_next = jnp.minimum(e + 1, ne - 1)

    @pl.when(fc == 0)
    def _():
        @pl.when(e == 0)
        def _():
            issue_gather(0, h2_hbm, stage_ref, g_sem)
        wait_rows(stage_ref, g_sem)
        for s in range(TOK_SUB):
            xb_ref[:, s * LANES:(s + 1) * LANES] = stage_ref[_tok_rows(s, cap), :].astype(BF16)

    @pl.when((fc == gather_step) & (e > 0))
    def _():
        wait_rows(orow_ref, s_sem)

    def finish_rows(ee, tok0, ntok):
        for b in range(ntok // LANES):
            first = tok0 + b * LANES
            rows = pl.ds(first if isinstance(first, int) else pl.multiple_of(first, LANES), LANES)
            g = gate_ref.at[ee][:, rows]
            gcol = jnp.broadcast_to(g, (LANES, LANES)).T
            for s in range(TOK_SUB):
                trows = pl.ds(first * TOK_SUB + s, LANES, stride=TOK_SUB)
                orow_ref[trows, :] += acc_ref[rows, s * LANES:(s + 1) * LANES] * gcol

    def scatter_prev(t):
        for jj in range(per_tile):
            j = t * per_tile + jj
            row_copy(orow_ref, j, out_hbm, idx_ref[(e - 1) * cap + j], s_sem).start()

    def gather_out(t):
        for jj in range(per_tile):
            j = t * per_tile + jj
            row_copy(out_hbm, idx_ref[e * cap + j], orow_ref, j, o_sem).start()

    def gather_next(part):
        def issue(t):
            for jj in range(x_per_tile):
                j = (part * ntile + t) * x_per_tile + jj
                row_copy(h2_hbm, idx_ref[e_next * cap + j], stage_ref, j, g_sem).start()
        return issue

    def tile_loop(copies, first):
        def tile(t, c):
            if copies is not None:
                copies(t)
            rows = pl.ds(pl.multiple_of(t * tt, tt), tt)
            x = xb_ref[rows, :]
            a = _dot(x, wg_ref[...].astype(BF16))
            u = _dot(x, wu_ref[...].astype(BF16))
            hm = (a * jax.nn.sigmoid(a) * u).astype(BF16)
            y = _dot(hm, wd_ref[...].astype(BF16))
            if first:
                acc_ref[rows, :] = y
            else:
                acc_ref[rows, :] += y
            return c
        lax.fori_loop(0, ntile, tile, 0)

    pl.when((fc == 0) & (e == 0))(lambda: tile_loop(None, True))
    pl.when((fc == 0) & (e > 0))(lambda: tile_loop(scatter_prev, True))
    pl.when(fc == gather_step)(lambda: tile_loop(gather_out, False))
    for part, k in enumerate(x_steps):
        pl.when(fc == k)(functools.partial(tile_loop, gather_next(part), False))

    @pl.when(fc == nfc - 1)
    def _():
        wait_rows(orow_ref, o_sem)
        finish_rows(e, 0, cap)

        @pl.when(e == ne - 1)
        def _():
            def body(j, c):
                row_copy(orow_ref, j, out_hbm, idx_ref[e * cap + j], s_sem).start()
                return c
            lax.fori_loop(0, cap, body, 0, unroll=FFN_DMA_UNROLL)
            wait_rows(orow_ref, s_sem)
            wait_rows(stage_ref, g_sem)


def _ffn(idx, gate, h2, x1, w_gate, w_up, w_down, cap):
    n = h2.shape[0] // TOK_SUB
    nfc = D_EXPERT // FFN_FC
    grid_spec = pltpu.PrefetchScalarGridSpec(
        num_scalar_prefetch=1,
        grid=(N_EXPERTS, nfc),
        in_specs=[pl.BlockSpec((N_EXPERTS, 1, cap), lambda e, f, idx: (0, 0, 0)),
                  pl.BlockSpec(memory_space=pl.ANY),
                  pl.BlockSpec((None, D_MODEL, FFN_FC), lambda e, f, idx: (e, 0, f)),
                  pl.BlockSpec((None, D_MODEL, FFN_FC), lambda e, f, idx: (e, 0, f)),
                  pl.BlockSpec((None, FFN_FC, D_MODEL), lambda e, f, idx: (e, f, 0)),
                  pl.BlockSpec(memory_space=pl.ANY)],
        out_specs=pl.BlockSpec(memory_space=pl.ANY),
        scratch_shapes=[pltpu.VMEM((cap * TOK_SUB, LANES), F32),
                        pltpu.VMEM((cap, D_MODEL), BF16),
                        pltpu.VMEM((cap, D_MODEL), F32),
                        pltpu.VMEM((cap * TOK_SUB, LANES), F32),
                        pltpu.SemaphoreType.DMA((3,))],
    )
    return pl.pallas_call(
        functools.partial(_ffn_kernel, cap=cap, nfc=nfc),
        grid_spec=grid_spec,
        out_shape=jax.ShapeDtypeStruct((n * TOK_SUB, LANES), F32),
        input_output_aliases={6: 0},
        compiler_params=_params(("arbitrary", "arbitrary")),
        name="ffn",
    )(idx.reshape(-1), gate, h2, w_gate, w_up, w_down, x1)


def _untile_kernel(x_ref, o_ref):
    for s in range(TOK_SUB):
        o_ref[:, s * LANES:(s + 1) * LANES] = x_ref[_tok_rows(s, o_ref.shape[0]), :]


def _untile(xt, tm):
    n = xt.shape[0] // TOK_SUB
    return pl.pallas_call(
        _untile_kernel,
        grid=(n // tm,),
        in_specs=[pl.BlockSpec((tm * TOK_SUB, LANES), lambda i: (i, 0))],
        out_specs=pl.BlockSpec((tm, D_MODEL), lambda i: (i, 0)),
        out_shape=jax.ShapeDtypeStruct((n, D_MODEL), F32),
        compiler_params=_params(("arbitrary",)),
        name="untile",
    )(xt)


def _rope_tables(T):
    half = DIFF_DH // 2
    inv_freq = jnp.power(ROPE_THETA, -jnp.arange(0, DIFF_DH, 2, dtype=F32) / DIFF_DH)
    ang = jnp.arange(T, dtype=F32)[:, None] * inv_freq[None, :]
    cos, sin = jnp.cos(ang), jnp.sin(ang)
    reps = HEAD_W // DIFF_DH
    return (jnp.concatenate([cos, cos] * reps, axis=-1),
            jnp.concatenate([-sin, sin] * reps, axis=-1))


def _layout_weights(w_in, gla_wg_f, gla_bg_f, gla_wg_b, gla_bg_b):
    q0, k0, v0, g0 = 0, 256, 512, 1024
    lr0, dq0 = 1536, 1568
    cols = []
    for h in range(GLA_HEADS):
        cols += [w_in[:, q0 + h * GLA_DK:q0 + (h + 1) * GLA_DK], w_in[:, k0 + h * GLA_DK:k0 + (h + 1) * GLA_DK]]
    cols += [w_in[:, v0:g0], w_in[:, g0:lr0], w_in[:, dq0:], w_in[:, lr0:dq0],
             jnp.zeros((D_MODEL, LANES - 2 * GLA_RANK), w_in.dtype)]
    w_r = jnp.concatenate(cols, axis=1).astype(BF16)
    wg = jnp.zeros((GLA_HEADS, LANES, HEAD_W), F32)
    bg = jnp.zeros((GLA_HEADS, 1, HEAD_W), F32)
    for h in range(GLA_HEADS):
        hs = slice(h * GLA_DK, (h + 1) * GLA_DK)
        wg = wg.at[h, :GLA_RANK, :GLA_DK].set(gla_wg_f[:, hs])
        wg = wg.at[h, GLA_RANK:2 * GLA_RANK, GLA_DK:].set(gla_wg_b[:, hs])
        bg = bg.at[h, 0, :GLA_DK].set(gla_bg_f[hs])
        bg = bg.at[h, 0, GLA_DK:].set(gla_bg_b[hs])
    return w_r, wg.astype(BF16), bg


def _lambda(lq1, lk1, lq2, lk2, lam_init):
    pad = lambda v: jnp.pad(v.astype(F32), (0, LANES - v.shape[0])).reshape(1, LANES)
    return pl.pallas_call(
        functools.partial(_lambda_kernel, lam_init=lam_init),
        out_shape=jax.ShapeDtypeStruct((1, 1), F32),
        name="lam",
    )(pad(lq1), pad(lk1), pad(lq2), pad(lk2))


def _encoder_layer(x, layer, lw, *, tm=512, qb=1024):
    (norm1_g, w_in, gla_wg_f, gla_bg_f, gla_wg_b, gla_bg_b, gla_norm_g, qk_norm_q, qk_norm_k,
     lambda_q1, lambda_k1, lambda_q2, lambda_k2, diff_norm_g, w_out, norm2_g, w_router,
     w_gate, w_up, w_down) = lw
    B, T, D = x.shape
    n = B * T
    cap = max(1, EC_FACTOR * n // N_EXPERTS)
    tm = min(tm, n)
    qb = min(qb, T)
    x2 = x.reshape(n, D)
    w_r, wg, bg = _layout_weights(w_in, gla_wg_f, gla_bg_f, gla_wg_b, gla_bg_b)
    row = lambda v: v.astype(F32).reshape(1, -1)
    two = lambda v: jnp.concatenate([v, v]).astype(F32).reshape(1, -1)

    gqk, gv, gog, dq, dk, dv, lr = _proj(x2, row(norm1_g), w_r, tm)
    a = _gla(gqk, gv, gog, lr, wg, bg, row(gla_norm_g), B, T)
    cos, sin = _rope_tables(T)
    lam_init = _lambda_init(layer)
    lam = _lambda(lambda_q1, lambda_k1, lambda_q2, lambda_k2, lam_init)
    d = _dattn(dq, dk, dv, cos, sin, two(qk_norm_q), two(qk_norm_k), row(diff_norm_g), lam, B, T, qb, lam_init)
    x1, h2, aff_t = _mixout(a, d, x2, w_out.astype(BF16), row(norm2_g), w_router.T.astype(F32), tm)
    idx, gate = _route(aff_t, n, cap)
    y = _untile(_ffn(idx, gate, h2, x1, w_gate, w_up, w_down, cap), tm)
    return y.reshape(B, T, D)


def kernel(x_prompt, x_sample, norm1_g, w_in, gla_wg_f, gla_bg_f, gla_wg_b, gla_bg_b, gla_norm_g, qk_norm_q, qk_norm_k, lambda_q1, lambda_k1, lambda_q2, lambda_k2, diff_norm_g, w_out, norm2_g, w_router, w_gate, w_up, w_down):
    y_prompt, y_sample = x_prompt, x_sample
    for l in range(norm1_g.shape[0]):
        lw = (norm1_g[l], w_in[l], gla_wg_f[l], gla_bg_f[l], gla_wg_b[l], gla_bg_b[l],
              gla_norm_g[l], qk_norm_q[l], qk_norm_k[l], lambda_q1[l], lambda_k1[l],
              lambda_q2[l], lambda_k2[l], diff_norm_g[l], w_out[l], norm2_g[l],
              w_router[l], w_gate[l], w_up[l], w_down[l])
        y_prompt = _encoder_layer(y_prompt, l, lw)
        y_sample = _encoder_layer(y_sample, l, lw)
    return (y_prompt, y_sample)
```

```python
import functools
import math

import jax
import jax.numpy as jnp
from jax import lax
from jax.experimental import pallas as pl
from jax.experimental.pallas import tpu as pltpu

F32 = jnp.float32
BF16 = jnp.bfloat16
I32 = jnp.int32

LANES = 128
VMEM_LIMIT = 56 * 1024 * 1024

D_MODEL = 1024
GLA_HEADS = 4
GLA_DK = 64
GLA_DV = 128
GLA_RANK = 16
GLA_TAU = 16.0
GLA_CHUNK = 64
DIFF_HEADS = 4
DIFF_DH = 64
DIFF_DV = 128
ROPE_THETA = 10000.0
N_EXPERTS = 16
EC_FACTOR = 2
D_EXPERT = 2 * D_MODEL
RMS_EPS = 1e-6
LOG2_E = 1.4426950408889634


def _lambda_init(layer):
    return 0.8 - 0.6 * math.exp(-0.3 * layer)


HEAD_W = 128
TOK_SUB = D_MODEL // LANES


def _tok_rows(s, ntok, first_tok=0):
    return pl.ds(first_tok * TOK_SUB + s, ntok, stride=TOK_SUB)
PROJ_GROUPS = 6
PROJ_W = PROJ_GROUPS * 4 * HEAD_W + LANES


def _dot(a, b):
    return jnp.dot(a, b, preferred_element_type=F32)


def _dot_nt(a, b):
    return lax.dot_general(a, b, (((1,), (1,)), ((), ())), preferred_element_type=F32)


def _dot_tn(a, b):
    return lax.dot_general(a, b, (((0,), (0,)), ((), ())), preferred_element_type=F32)


def _split3(x):
    a = x.astype(BF16)
    r = x - a.astype(F32)
    b = r.astype(BF16)
    c = (r - b.astype(F32)).astype(BF16)
    return a, b, c


def _params(sem):
    return pltpu.CompilerParams(dimension_semantics=sem, vmem_limit_bytes=VMEM_LIMIT)


def _proj_kernel(x_ref, g_ref, w_ref, gqk_ref, gv_ref, gog_ref, dq_ref, dk_ref, dv_ref, lr_ref):
    x = x_ref[...]
    ms = jnp.mean(x * x, axis=-1, keepdims=True)
    h = (x * lax.rsqrt(ms + RMS_EPS) * g_ref[...]).astype(BF16)
    gw = 4 * HEAD_W
    for i, o in enumerate((gqk_ref, gv_ref, gog_ref, dq_ref, dk_ref, dv_ref)):
        o[...] = _dot(h, w_ref[:, i * gw:(i + 1) * gw]).astype(o.dtype)
    lr_ref[...] = _dot(h, w_ref[:, PROJ_GROUPS * gw:])


def _proj(x2, g1, w_r, tm):
    n = x2.shape[0]
    gw = 4 * HEAD_W
    row = lambda i: (i, 0)
    outs = [jax.ShapeDtypeStruct((n, gw), BF16)] * PROJ_GROUPS + [jax.ShapeDtypeStruct((n, LANES), F32)]
    return pl.pallas_call(
        _proj_kernel,
        grid=(n // tm,),
        in_specs=[pl.BlockSpec((tm, D_MODEL), row),
                  pl.BlockSpec((1, D_MODEL), lambda i: (0, 0)),
                  pl.BlockSpec((D_MODEL, PROJ_W), lambda i: (0, 0))],
        out_specs=[pl.BlockSpec((tm, gw), row)] * PROJ_GROUPS + [pl.BlockSpec((tm, LANES), row)],
        out_shape=outs,
        compiler_params=_params(("arbitrary",)),
        name="proj",
    )(x2, g1, w_r)


GLA_BLK = 256


def _gla_kernel(gqk_ref, gv_ref, gog_ref, lr_ref, wg_ref, bg_ref, ng_ref, o_ref,
                qh_ref, u_ref, dec_ref, acc_ref, b_ref, tot_ref, *, T):
    C = GLA_CHUNK
    nblk = T // GLA_BLK
    nch = T // C
    lane = lax.broadcasted_iota(I32, (1, HEAD_W), 1)
    fwd = lane < GLA_DK
    ri = lax.broadcasted_iota(I32, (GLA_BLK, GLA_BLK), 0)
    ci = lax.broadcasted_iota(I32, (GLA_BLK, GLA_BLK), 1)
    same = (ri // C) == (ci // C)
    prefix_total = jnp.concatenate([jnp.where(same & (ci <= ri), 1.0, 0.0),
                                    jnp.where(same, 1.0, 0.0)], axis=0).astype(BF16)
    m_f = same & (ci <= ri)
    m_b = same & (ci > ri)
    wg = wg_ref[...]
    bg = bg_ref[...]
    cpb = GLA_BLK // C
    chunk_of_row = lax.broadcasted_iota(I32, (GLA_BLK, 1), 0) // C

    def gates(i, carry):
        rows = pl.ds(pl.multiple_of(i * GLA_BLK, GLA_BLK), GLA_BLK)
        z = _dot(lr_ref[rows, :].astype(BF16), wg) + bg
        la = (jnp.minimum(z, 0.0) - jnp.log(1.0 + jnp.exp(-jnp.abs(z)))) * (1.0 / GLA_TAU)
        hi = la.astype(BF16)
        lo = (la - hi.astype(F32)).astype(BF16)
        pt = _dot(prefix_total, jnp.concatenate([hi, lo], axis=1))
        pt = pt[:, :HEAD_W] + pt[:, HEAD_W:]
        pre, tot = pt[:GLA_BLK, :], pt[GLA_BLK:, :]
        b_ref[rows, :] = jnp.where(fwd, pre, tot - pre + la)
        tot_ref[rows, :] = tot
        return carry

    lax.fori_loop(0, nblk, gates, 0, unroll=8)

    def block(i, carry):
        rows = pl.ds(pl.multiple_of(i * GLA_BLK, GLA_BLK), GLA_BLK)
        b = b_ref[rows, :]
        tot = tot_ref[rows, :]
        mid = 0.5 * tot
        blk = gqk_ref[rows, :].astype(F32)
        rot = pltpu.roll(blk, GLA_DK, 1)
        qq = jnp.where(fwd, blk, rot) * (GLA_DK ** -0.5)
        kk = jnp.where(fwd, rot, blk)
        qt = qq * jnp.exp(b - mid)
        kt = (kk * jnp.exp(mid - b)).astype(BF16)
        kh = (kk * jnp.exp(tot - b)).astype(BF16)
        qh_ref[rows, :] = (qq * jnp.exp(b)).astype(BF16)
        qfb = jnp.concatenate([jnp.where(fwd, qt, 0.0), jnp.where(fwd, 0.0, qt)], axis=0).astype(BF16)
        sfb = _dot_nt(qfb, kt)
        s = jnp.where(m_f, sfb[:GLA_BLK, :], jnp.where(m_b, sfb[GLA_BLK:, :], 0.0)).astype(BF16)
        v = gv_ref[rows, :]
        acc_ref[rows, :] = _dot(s, v)
        zero = jnp.zeros_like(kh)
        kx = jnp.concatenate([jnp.where(chunk_of_row == c, kh, zero) for c in range(cpb)], axis=1)
        ux = _dot_tn(v, kx)
        dec = jnp.exp(tot)
        for c in range(cpb):
            u_ref[i * cpb + c] = ux[:, c * HEAD_W:(c + 1) * HEAD_W]
            dec_ref[i * cpb + c] = jnp.broadcast_to(dec[c * C:c * C + 1, :], (8, HEAD_W))
        return carry

    lax.fori_loop(0, nblk, block, 0, unroll=8)

    def step(t, st):
        nf = t
        nb = nch - 1 - t
        stb = st.astype(BF16)
        rf = pl.ds(pl.multiple_of(nf * C, C), C)
        rb = pl.ds(pl.multiple_of(nb * C, C), C)
        qf = qh_ref[rf, :]
        qb = qh_ref[rb, :]
        zero = jnp.zeros_like(qf)
        o = _dot_nt(jnp.concatenate([jnp.where(fwd, qf, zero), jnp.where(fwd, zero, qb)], axis=0), stb)
        acc_ref[rf, :] += o[:C, :]
        acc_ref[rb, :] += o[C:, :]
        d = jnp.where(fwd, dec_ref[nf][0:1, :], dec_ref[nb][0:1, :])
        u = jnp.where(fwd, u_ref[nf], u_ref[nb])
        return st * d + u

    lax.fori_loop(0, nch, step, jnp.zeros((GLA_DV, HEAD_W), F32), unroll=8)

    ng = ng_ref[...]
    eb = min(T, 512)

    def epi(i, carry):
        rows = pl.ds(pl.multiple_of(i * eb, eb), eb)
        o = acc_ref[rows, :]
        y = o * lax.rsqrt(jnp.mean(o * o, axis=-1, keepdims=True) + RMS_EPS) * ng
        og = gog_ref[rows, :].astype(F32)
        o_ref[rows, :] = (y * (og * jax.nn.sigmoid(og))).astype(o_ref.dtype)
        return carry

    lax.fori_loop(0, T // eb, epi, 0)


def _gla(gqk, gv, gog, lr, wg, bg, ng, B, T):
    n = B * T
    head = pl.BlockSpec((T, HEAD_W), lambda b, h: (b, h))
    return pl.pallas_call(
        functools.partial(_gla_kernel, T=T),
        grid=(B, GLA_HEADS),
        in_specs=[head, head, head,
                  pl.BlockSpec((T, LANES), lambda b, h: (b, 0)),
                  pl.BlockSpec((None, LANES, HEAD_W), lambda b, h: (h, 0, 0)),
                  pl.BlockSpec((None, 1, HEAD_W), lambda b, h: (h, 0, 0)),
                  pl.BlockSpec((1, HEAD_W), lambda b, h: (0, 0))],
        out_specs=head,
        out_shape=jax.ShapeDtypeStruct((n, GLA_HEADS * HEAD_W), BF16),
        scratch_shapes=[pltpu.VMEM((T, HEAD_W), BF16),
                        pltpu.VMEM((T // GLA_CHUNK, GLA_DV, HEAD_W), F32),
                        pltpu.VMEM((T // GLA_CHUNK, 8, HEAD_W), F32),
                        pltpu.VMEM((T, GLA_DV), F32),
                        pltpu.VMEM((T, HEAD_W), F32),
                        pltpu.VMEM((T, HEAD_W), F32)],
        compiler_params=_params(("arbitrary", "arbitrary")),
        name="gla",
    )(gqk, gv, gog, lr, wg, bg, ng)


DATTN_KSPLIT = 2


def _qk_prep(x, gain, cos, sin):
    ri = lax.broadcasted_iota(I32, (HEAD_W, HEAD_W), 0)
    ci = lax.broadcasted_iota(I32, (HEAD_W, HEAD_W), 1)
    same_map = jnp.where((ri // DIFF_DH) == (ci // DIFF_DH), 1.0, 0.0).astype(BF16)
    ms = _dot((x * x).astype(BF16), same_map) * (1.0 / DIFF_DH)
    y = x * lax.rsqrt(ms + RMS_EPS) * gain
    half = DIFF_DH // 2
    first = (lax.broadcasted_iota(I32, (1, HEAD_W), 1) & half) == 0
    swapped = jnp.where(first, pltpu.roll(y, HEAD_W - half, 1), pltpu.roll(y, half, 1))
    return y * cos + swapped * sin


def _dattn_kernel(dq_ref, dk_ref, dv_ref, cos_ref, sin_ref, qg_ref, kg_ref, ng_ref, lam_ref, o_ref,
                  kp_ref, s_ref, mp_ref, lp_ref, acc_ref, *, T, QB, TK, lam_init):
    qi = pl.program_id(2)
    lane = lax.broadcasted_iota(I32, (1, HEAD_W), 1)
    map0 = lane < DIFF_DH
    nk = T // TK

    @pl.when(qi == 0)
    def _():
        kb = min(T, 512)

        def kprep(i, carry):
            rows = pl.ds(pl.multiple_of(i * kb, kb), kb)
            k = _qk_prep(dk_ref[rows, :].astype(F32), kg_ref[...], cos_ref[rows, :], sin_ref[rows, :])
            kp_ref[rows, :] = k.astype(BF16)
            return carry
        lax.fori_loop(0, T // kb, kprep, 0)

    qrows = pl.ds(pl.multiple_of(qi * QB, QB), QB)
    q = _qk_prep(dq_ref[...].astype(F32), qg_ref[...], cos_ref[qrows, :], sin_ref[qrows, :])
    q = q * (DIFF_DH ** -0.5 * LOG2_E)
    qs = jnp.concatenate([jnp.where(map0, q, 0.0), jnp.where(map0, 0.0, q)], axis=0).astype(BF16)
    ntile = TK // LANES

    def lane_tiles(x):
        return [x[:, j * LANES:(j + 1) * LANES] for j in range(ntile)]

    mp_ref[...] = jnp.full(mp_ref.shape, -jnp.inf, F32)
    lp_ref[...] = jnp.zeros(lp_ref.shape, F32)
    acc_ref[...] = jnp.zeros(acc_ref.shape, F32)

    def scores(i, carry):
        cols = pl.ds(pl.multiple_of(i * TK, TK), TK)
        s = _dot_nt(qs, kp_ref[cols, :])
        s_ref[:, cols] = s
        mp_ref[...] = jnp.maximum(mp_ref[...], functools.reduce(jnp.maximum, lane_tiles(s)))
        return carry

    lax.fori_loop(0, nk, scores, 0)
    m = jnp.max(mp_ref[...], axis=-1, keepdims=True)

    def accum(i, carry):
        cols = pl.ds(pl.multiple_of(i * TK, TK), TK)
        p = jnp.exp2(s_ref[:, cols] - m)
        lp_ref[...] += functools.reduce(jnp.add, lane_tiles(p))
        acc_ref[...] += _dot(p.astype(BF16), dv_ref[cols, :])
        return carry

    lax.fori_loop(0, nk, accum, 0)
    on = acc_ref[...] / jnp.sum(lp_ref[...], axis=-1, keepdims=True)
    lam = lam_ref[...]
    o = on[:QB, :] - lam * on[QB:, :]
    y = o * lax.rsqrt(jnp.mean(o * o, axis=-1, keepdims=True) + RMS_EPS) * ng_ref[...]
    o_ref[...] = (y * (1.0 - lam_init)).astype(o_ref.dtype)


def _lambda_kernel(q1_ref, k1_ref, q2_ref, k2_ref, o_ref, *, lam_init):
    a = jnp.exp(jnp.sum(q1_ref[...] * k1_ref[...], axis=-1, keepdims=True))
    b = jnp.exp(jnp.sum(q2_ref[...] * k2_ref[...], axis=-1, keepdims=True))
    o_ref[...] = a - b + lam_init


def _dattn(dq, dk, dv, cos, sin, qg, kg, ng, lam, B, T, QB, lam_init):
    n = B * T
    TK = T // DATTN_KSPLIT
    nq = T // QB
    full = lambda shape: pl.BlockSpec(shape, lambda b, h, q: (0, 0))
    kv = pl.BlockSpec((T, HEAD_W), lambda b, h, q: (b, h))
    qo = pl.BlockSpec((QB, HEAD_W), lambda b, h, q: (b * nq + q, h))
    return pl.pallas_call(
        functools.partial(_dattn_kernel, T=T, QB=QB, TK=TK, lam_init=lam_init),
        grid=(B, DIFF_HEADS, nq),
        in_specs=[qo, kv, kv, full((T, HEAD_W)), full((T, HEAD_W)),
                  full((1, HEAD_W)), full((1, HEAD_W)), full((1, HEAD_W)), full((1, 1))],
        out_specs=qo,
        out_shape=jax.ShapeDtypeStruct((n, DIFF_HEADS * HEAD_W), BF16),
        scratch_shapes=[pltpu.VMEM((T, HEAD_W), BF16),
                        pltpu.VMEM((2 * QB, T), F32),
                        pltpu.VMEM((2 * QB, LANES), F32),
                        pltpu.VMEM((2 * QB, LANES), F32),
                        pltpu.VMEM((2 * QB, DIFF_DV), F32)],
        compiler_params=_params(("arbitrary", "arbitrary", "arbitrary")),
        name="dattn",
    )(dq, dk, dv, cos, sin, qg, kg, ng, lam)


def _mixout_kernel(a_ref, d_ref, x_ref, w_ref, g_ref, wr_ref, x1_ref, h2_ref, aff_ref):
    half = GLA_HEADS * HEAD_W
    y = _dot(a_ref[...], w_ref[:half, :]) + _dot(d_ref[...], w_ref[half:, :])
    x1 = x_ref[...] + y
    h2 = x1 * lax.rsqrt(jnp.mean(x1 * x1, axis=-1, keepdims=True) + RMS_EPS) * g_ref[...]
    for s in range(TOK_SUB):
        cols = slice(s * LANES, (s + 1) * LANES)
        x1_ref[_tok_rows(s, x1.shape[0]), :] = x1[:, cols]
        h2_ref[_tok_rows(s, x1.shape[0]), :] = h2[:, cols]
    logits = jnp.zeros((N_EXPERTS, h2.shape[0]), F32)
    hp = _split3(h2)
    wp = _split3(wr_ref[...])
    for i, j in ((0, 0), (0, 1), (1, 0)):
        logits = logits + _dot_nt(wp[i], hp[j])
    m = jnp.max(logits, axis=0, keepdims=True)
    e = jnp.exp(logits - m)
    aff_ref[...] = e / jnp.sum(e, axis=0, keepdims=True)


def _mixout(a, d, x2, w_out, g2, wr_t, tm):
    n = x2.shape[0]
    row = lambda i: (i, 0)
    half = GLA_HEADS * HEAD_W
    return pl.pallas_call(
        _mixout_kernel,
        grid=(n // tm,),
        in_specs=[pl.BlockSpec((tm, half), row), pl.BlockSpec((tm, half), row),
                  pl.BlockSpec((tm, D_MODEL), row),
                  pl.BlockSpec((D_MODEL, D_MODEL), lambda i: (0, 0)),
                  pl.BlockSpec((1, D_MODEL), lambda i: (0, 0)),
                  pl.BlockSpec((N_EXPERTS, D_MODEL), lambda i: (0, 0))],
        out_specs=[pl.BlockSpec((tm * TOK_SUB, LANES), row), pl.BlockSpec((tm * TOK_SUB, LANES), row),
                   pl.BlockSpec((N_EXPERTS, tm), lambda i: (0, i))],
        out_shape=[jax.ShapeDtypeStruct((n * TOK_SUB, LANES), F32), jax.ShapeDtypeStruct((n * TOK_SUB, LANES), F32),
                   jax.ShapeDtypeStruct((N_EXPERTS, n), F32)],
        compiler_params=_params(("arbitrary",)),
        name="mixout",
    )(a, d, x2, w_out, g2, wr_t)


ROUTE_BISECTIONS = 40


def _route_kernel(aff_ref, idx_ref, gate_ref, *, n, cap):
    R = n // LANES
    E = N_EXPERTS
    aff = aff_ref[...]

    def count(mask):
        c = jnp.sum(jnp.where(mask, 1.0, 0.0), axis=1, keepdims=True)
        return jnp.sum(c, axis=2, keepdims=True)

    capf = float(cap)
    k = jnp.zeros((E, 1, 1), F32)
    for bit in (64.0, 32.0, 16.0, 8.0, 4.0, 2.0, 1.0):
        cand = k + bit
        k = jnp.where(count(aff >= jnp.exp2(1.0 - cand)) < capf, cand, k)
    hi = jnp.exp2(1.0 - k)
    lo = jnp.where(k >= 127.0, 0.0, jnp.exp2(-jnp.minimum(k, 126.0)))

    def bisect(i, c):
        lo, hi = c
        mid = 0.5 * (lo + hi)
        ok = count(aff >= mid) >= capf
        return jnp.where(ok, mid, lo), jnp.where(ok, hi, mid)

    lo, hi = lax.fori_loop(0, ROUTE_BISECTIONS, bisect, (lo, hi))
    gt = aff >= hi
    eq = (aff >= lo) & (aff < hi)
    need = capf - count(gt)
    tok = (lax.broadcasted_iota(I32, (1, R, LANES), 1) * LANES
           + lax.broadcasted_iota(I32, (1, R, LANES), 2))
    nbits = max(1, (n - 1).bit_length())

    def tie_step(i, m0):
        cand = m0 | jnp.left_shift(jnp.int32(1), nbits - 1 - i)
        return jnp.where(count(eq & (tok < cand)) < need, cand, m0)

    m0 = lax.fori_loop(0, nbits, tie_step, jnp.zeros((E, 1, 1), I32))
    sel = gt | (eq & (tok <= m0))

    ci = lax.broadcasted_iota(I32, (LANES, LANES), 0)
    cj = lax.broadcasted_iota(I32, (LANES, LANES), 1)
    upper = jnp.where(ci <= cj, 1.0, 0.0).astype(BF16)
    ri = lax.broadcasted_iota(I32, (R, R), 0)
    rj = lax.broadcasted_iota(I32, (R, R), 1)
    lower = jnp.where(rj < ri, 1.0, 0.0).astype(BF16)
    jrow = lax.broadcasted_iota(I32, (1, cap), 1).astype(F32)
    rcol = lax.broadcasted_iota(I32, (R, 1), 0).astype(F32)
    ccol = lax.broadcasted_iota(I32, (LANES, 1), 0).astype(F32)

    for e in range(E):
        sel_e = sel[e]
        w = _dot(jnp.where(sel_e, 1.0, 0.0).astype(BF16), upper)
        rt = w[:, LANES - 1:LANES]
        rp = _dot(lower, jnp.broadcast_to(rt, (R, LANES)).astype(BF16))[:, 0:1]
        rows_t = jnp.where((rp <= jrow) & (jrow < rp + rt), 1.0, 0.0)
        k1 = jrow - jnp.sum(rows_t * rp, axis=0, keepdims=True) + 1.0
        rows_b = rows_t.astype(BF16)
        wrow = _dot(jnp.where(sel_e, w, 0.0).T.astype(BF16), rows_b)
        cols_t = jnp.where(wrow == k1, 1.0, 0.0)
        r_of = jnp.sum(rows_t * rcol, axis=0, keepdims=True)
        c_of = jnp.sum(cols_t * ccol, axis=0, keepdims=True)
        idx_ref[e] = (r_of * float(LANES) + c_of).astype(I32)
        a3 = _split3(aff[e].T)
        arow = _dot(a3[0], rows_b) + _dot(a3[1], rows_b) + _dot(a3[2], rows_b)
        gate_ref[e] = jnp.sum(cols_t * arow, axis=0, keepdims=True)


def _route(aff_t, n, cap):
    R = n // LANES
    return pl.pallas_call(
        functools.partial(_route_kernel, n=n, cap=cap),
        out_shape=[jax.ShapeDtypeStruct((N_EXPERTS, 1, cap), I32),
                   jax.ShapeDtypeStruct((N_EXPERTS, 1, cap), F32)],
        compiler_params=pltpu.CompilerParams(vmem_limit_bytes=VMEM_LIMIT),
        name="route",
    )(aff_t.reshape(N_EXPERTS, R, LANES))


FFN_FC = 512
FFN_TT = 1024
FFN_DMA_UNROLL = 16


def _ffn_kernel(idx_ref, gate_ref, h2_hbm, wg_ref, wu_ref, wd_ref, x1_hbm, out_hbm,
                stage_ref, xb_ref, acc_ref, orow_ref, sems, *, cap, nfc):
    del x1_hbm
    e = pl.program_id(0)
    fc = pl.program_id(1)
    ne = pl.num_programs(0)
    g_sem, o_sem, s_sem = sems.at[0], sems.at[1], sems.at[2]
    tt = min(FFN_TT, cap)

    def row_copy(src, s_tok, dst, d_tok, sem):
        s_rows = pl.ds(pl.multiple_of(s_tok * TOK_SUB, TOK_SUB), TOK_SUB)
        d_rows = pl.ds(pl.multiple_of(d_tok * TOK_SUB, TOK_SUB), TOK_SUB)
        return pltpu.make_async_copy(src.at[s_rows], dst.at[d_rows], sem)

    def issue_gather(ee, src, dst, sem):
        def body(j, c):
            row_copy(src, idx_ref[ee * cap + j], dst, j, sem).start()
            return c
        lax.fori_loop(0, cap, body, 0, unroll=FFN_DMA_UNROLL)

    def wait_rows(buf, sem):
        pltpu.make_async_copy(buf, buf, sem).wait()

    gather_step = nfc - 2
    x_steps = [k for k in range(nfc) if k not in (0, gather_step)]
    assert 0 < gather_step and x_steps, "the copy schedule needs at least three hidden-dim steps"
    ntile = cap // tt
    per_tile = cap // ntile
    x_per_tile = cap // (len(x_steps) * ntile)
    e_next = jnp.minimum(e + 1, ne - 1)

    @pl.when(fc == 0)
    def _():
        @pl.when(e == 0)
        def _():
            issue_gather(0, h2_hbm, stage_ref, g_sem)
        wait_rows(stage_ref, g_sem)
        for s in range(TOK_SUB):
            xb_ref[:, s * LANES:(s + 1) * LANES] = stage_ref[_tok_rows(s, cap), :].astype(BF16)

    @pl.when((fc == gather_step) & (e > 0))
    def _():
        wait_rows(orow_ref, s_sem)

    def finish_rows(ee, tok0, ntok):
        for b in range(ntok // LANES):
            first = tok0 + b * LANES
            rows = pl.ds(first if isinstance(first, int) else pl.multiple_of(first, LANES), LANES)
            g = gate_ref.at[ee][:, rows]
            gcol = jnp.broadcast_to(g, (LANES, LANES)).T
            for s in range(TOK_SUB):
                trows = pl.ds(first * TOK_SUB + s, LANES, stride=TOK_SUB)
                orow_ref[trows, :] += acc_ref[rows, s * LANES:(s + 1) * LANES] * gcol

    def scatter_prev(t):
        for jj in range(per_tile):
            j = t * per_tile + jj
            row_copy(orow_ref, j, out_hbm, idx_ref[(e - 1) * cap + j], s_sem).start()

    def gather_out(t):
        for jj in range(per_tile):
            j = t * per_tile + jj
            row_copy(out_hbm, idx_ref[e * cap + j], orow_ref, j, o_sem).start()

    def gather_next(part):
        def issue(t):
            for jj in range(x_per_tile):
                j = (part * ntile + t) * x_per_tile + jj
                row_copy(h2_hbm, idx_ref[e_next * cap + j], stage_ref, j, g_sem).start()
        return issue

    def tile_loop(copies, first):
        def tile(t, c):
            if copies is not None:
                copies(t)
            rows = pl.ds(pl.multiple_of(t * tt, tt), tt)
            x = xb_ref[rows, :]
            a = _dot(x, wg_ref[...].astype(BF16))
            u = _dot(x, wu_ref[...].astype(BF16))
            hm = (a * jax.nn.sigmoid(a) * u).astype(BF16)
            y = _dot(hm, wd_ref[...].astype(BF16))
            if first:
                acc_ref[rows, :] = y
            else:
                acc_ref[rows, :] += y
            return c
        lax.fori_loop(0, ntile, tile, 0)

    pl.when((fc == 0) & (e == 0))(lambda: tile_loop(None, True))
    pl.when((fc == 0) & (e > 0))(lambda: tile_loop(scatter_prev, True))
    pl.when(fc == gather_step)(lambda: tile_loop(gather_out, False))
    for part, k in enumerate(x_steps):
        pl.when(fc == k)(functools.partial(tile_loop, gather_next(part), False))

    @pl.when(fc == nfc - 1)
    def _():
        wait_rows(orow_ref, o_sem)
        finish_rows(e, 0, cap)

        @pl.when(e == ne - 1)
        def _():
            def body(j, c):
                row_copy(orow_ref, j, out_hbm, idx_ref[e * cap + j], s_sem).start()
                return c
            lax.fori_loop(0, cap, body, 0, unroll=FFN_DMA_UNROLL)
            wait_rows(orow_ref, s_sem)
            wait_rows(stage_ref, g_sem)


def _ffn(idx, gate, h2, x1, w_gate, w_up, w_down, cap):
    n = h2.shape[0] // TOK_SUB
    nfc = D_EXPERT // FFN_FC
    grid_spec = pltpu.PrefetchScalarGridSpec(
        num_scalar_prefetch=1,
        grid=(N_EXPERTS, nfc),
        in_specs=[pl.BlockSpec((N_EXPERTS, 1, cap), lambda e, f, idx: (0, 0, 0)),
                  pl.BlockSpec(memory_space=pl.ANY),
                  pl.BlockSpec((None, D_MODEL, FFN_FC), lambda e, f, idx: (e, 0, f)),
                  pl.BlockSpec((None, D_MODEL, FFN_FC), lambda e, f, idx: (e, 0, f)),
                  pl.BlockSpec((None, FFN_FC, D_MODEL), lambda e, f, idx: (e, f, 0)),
                  pl.BlockSpec(memory_space=pl.ANY)],
        out_specs=pl.BlockSpec(memory_space=pl.ANY),
        scratch_shapes=[pltpu.VMEM((cap * TOK_SUB, LANES), F32),
                        pltpu.VMEM((cap, D_MODEL), BF16),
                        pltpu.VMEM((cap, D_MODEL), F32),
                        pltpu.VMEM((cap * TOK_SUB, LANES), F32),
                        pltpu.SemaphoreType.DMA((3,))],
    )
    return pl.pallas_call(
        functools.partial(_ffn_kernel, cap=cap, nfc=nfc),
        grid_spec=grid_spec,
        out_shape=jax.ShapeDtypeStruct((n * TOK_SUB, LANES), F32),
        input_output_aliases={6: 0},
        compiler_params=_params(("arbitrary", "arbitrary")),
        name="ffn",
    )(idx.reshape(-1), gate, h2, w_gate, w_up, w_down, x1)


def _untile_kernel(x_ref, o_ref):
    for s in range(TOK_SUB):
        o_ref[:, s * LANES:(s + 1) * LANES] = x_ref[_tok_rows(s, o_ref.shape[0]), :]


def _untile(xt, tm):
    n = xt.shape[0] // TOK_SUB
    return pl.pallas_call(
        _untile_kernel,
        grid=(n // tm,),
        in_specs=[pl.BlockSpec((tm * TOK_SUB, LANES), lambda i: (i, 0))],
        out_specs=pl.BlockSpec((tm, D_MODEL), lambda i: (i, 0)),
        out_shape=jax.ShapeDtypeStruct((n, D_MODEL), F32),
        compiler_params=_params(("arbitrary",)),
        name="untile",
    )(xt)


def _rope_tables(T):
    half = DIFF_DH // 2
    inv_freq = jnp.power(ROPE_THETA, -jnp.arange(0, DIFF_DH, 2, dtype=F32) / DIFF_DH)
    ang = jnp.arange(T, dtype=F32)[:, None] * inv_freq[None, :]
    cos, sin = jnp.cos(ang), jnp.sin(ang)
    reps = HEAD_W // DIFF_DH
    return (jnp.concatenate([cos, cos] * reps, axis=-1),
            jnp.concatenate([-sin, sin] * reps, axis=-1))


def _layout_weights(w_in, gla_wg_f, gla_bg_f, gla_wg_b, gla_bg_b):
    q0, k0, v0, g0 = 0, 256, 512, 1024
    lr0, dq0 = 1536, 1568
    cols = []
    for h in range(GLA_HEADS):
        cols += [w_in[:, q0 + h * GLA_DK:q0 + (h + 1) * GLA_DK], w_in[:, k0 + h * GLA_DK:k0 + (h + 1) * GLA_DK]]
    cols += [w_in[:, v0:g0], w_in[:, g0:lr0], w_in[:, dq0:], w_in[:, lr0:dq0],
             jnp.zeros((D_MODEL, LANES - 2 * GLA_RANK), w_in.dtype)]
    w_r = jnp.concatenate(cols, axis=1).astype(BF16)
    wg = jnp.zeros((GLA_HEADS, LANES, HEAD_W), F32)
    bg = jnp.zeros((GLA_HEADS, 1, HEAD_W), F32)
    for h in range(GLA_HEADS):
        hs = slice(h * GLA_DK, (h + 1) * GLA_DK)
        wg = wg.at[h, :GLA_RANK, :GLA_DK].set(gla_wg_f[:, hs])
        wg = wg.at[h, GLA_RANK:2 * GLA_RANK, GLA_DK:].set(gla_wg_b[:, hs])
        bg = bg.at[h, 0, :GLA_DK].set(gla_bg_f[hs])
        bg = bg.at[h, 0, GLA_DK:].set(gla_bg_b[hs])
    return w_r, wg.astype(BF16), bg


def _lambda(lq1, lk1, lq2, lk2, lam_init):
    pad = lambda v: jnp.pad(v.astype(F32), (0, LANES - v.shape[0])).reshape(1, LANES)
    return pl.pallas_call(
        functools.partial(_lambda_kernel, lam_init=lam_init),
        out_shape=jax.ShapeDtypeStruct((1, 1), F32),
        name="lam",
    )(pad(lq1), pad(lk1), pad(lq2), pad(lk2))


def _encoder_layer(x, layer, lw, *, tm=512, qb=1024):
    (norm1_g, w_in, gla_wg_f, gla_bg_f, gla_wg_b, gla_bg_b, gla_norm_g, qk_norm_q, qk_norm_k,
     lambda_q1, lambda_k1, lambda_q2, lambda_k2, diff_norm_g, w_out, norm2_g, w_router,
     w_gate, w_up, w_down) = lw
    B, T, D = x.shape
    n = B * T
    cap = max(1, EC_FACTOR * n // N_EXPERTS)
    tm = min(tm, n)
    qb = min(qb, T)
    x2 = x.reshape(n, D)
    w_r, wg, bg = _layout_weights(w_in, gla_wg_f, gla_bg_f, gla_wg_b, gla_bg_b)
    row = lambda v: v.astype(F32).reshape(1, -1)
    two = lambda v: jnp.concatenate([v, v]).astype(F32).reshape(1, -1)

    gqk, gv, gog, dq, dk, dv, lr = _proj(x2, row(norm1_g), w_r, tm)
    a = _gla(gqk, gv, gog, lr, wg, bg, row(gla_norm_g), B, T)
    cos, sin = _rope_tables(T)
    lam_init = _lambda_init(layer)
    lam = _lambda(lambda_q1, lambda_k1, lambda_q2, lambda_k2, lam_init)
    d = _dattn(dq, dk, dv, cos, sin, two(qk_norm_q), two(qk_norm_k), row(diff_norm_g), lam, B, T, qb, lam_init)
    x1, h2, aff_t = _mixout(a, d, x2, w_out.astype(BF16), row(norm2_g), w_router.T.astype(F32), tm)
    idx, gate = _route(aff_t, n, cap)
    y = _untile(_ffn(idx, gate, h2, x1, w_gate, w_up, w_down, cap), tm)
    return y.reshape(B, T, D)


def kernel(x_prompt, x_sample, norm1_g, w_in, gla_wg_f, gla_bg_f, gla_wg_b, gla_bg_b, gla_norm_g, qk_norm_q, qk_norm_k, lambda_q1, lambda_k1, lambda_q2, lambda_k2, diff_norm_g, w_out, norm2_g, w_router, w_gate, w_up, w_down):
    y_prompt, y_sample = x_prompt, x_sample
    for l in range(norm1_g.shape[0]):
        lw = (norm1_g[l], w_in[l], gla_wg_f[l], gla_bg_f[l], gla_wg_b[l], gla_bg_b[l],
              gla_norm_g[l], qk_norm_q[l], qk_norm_k[l], lambda_q1[l], lambda_k1[l],
              lambda_q2[l], lambda_k2[l], diff_norm_g[l], w_out[l], norm2_g[l],
              w_router[l], w_gate[l], w_up[l], w_down[l])
        y_prompt = _encoder_layer(y_prompt, l, lw)
        y_sample = _encoder_layer(y_sample, l, lw)
    return (y_prompt, y_sample)
```

```python
import functools
import math

import jax
import jax.numpy as jnp
from jax import lax
from jax.experimental import pallas as pl
from jax.experimental.pallas import tpu as pltpu

F32 = jnp.float32
BF16 = jnp.bfloat16
I32 = jnp.int32

LANES = 128
VMEM_LIMIT = 56 * 1024 * 1024

D_MODEL = 1024
GLA_HEADS = 4
GLA_DK = 64
GLA_DV = 128
GLA_RANK = 16
GLA_TAU = 16.0
GLA_CHUNK = 64
DIFF_HEADS = 4
DIFF_DH = 64
DIFF_DV = 128
ROPE_THETA = 10000.0
N_EXPERTS = 16
EC_FACTOR = 2
D_EXPERT = 2 * D_MODEL
RMS_EPS = 1e-6
LOG2_E = 1.4426950408889634


def _lambda_init(layer):
    return 0.8 - 0.6 * math.exp(-0.3 * layer)


HEAD_W = 128
TOK_SUB = D_MODEL // LANES


def _tok_rows(s, ntok, first_tok=0):
    return pl.ds(first_tok * TOK_SUB + s, ntok, stride=TOK_SUB)
PROJ_GROUPS = 6
PROJ_W = PROJ_GROUPS * 4 * HEAD_W + LANES


def _dot(a, b):
    return jnp.dot(a, b, preferred_element_type=F32)


def _dot_nt(a, b):
    return lax.dot_general(a, b, (((1,), (1,)), ((), ())), preferred_element_type=F32)


def _dot_tn(a, b):
    return lax.dot_general(a, b, (((0,), (0,)), ((), ())), preferred_element_type=F32)


def _split3(x):
    a = x.astype(BF16)
    r = x - a.astype(F32)
    b = r.astype(BF16)
    c = (r - b.astype(F32)).astype(BF16)
    return a, b, c


def _params(sem):
    return pltpu.CompilerParams(dimension_semantics=sem, vmem_limit_bytes=VMEM_LIMIT)


def _proj_kernel(x_ref, g_ref, w_ref, gqk_ref, gv_ref, gog_ref, dq_ref, dk_ref, dv_ref, lr_ref):
    x = x_ref[...]
    ms = jnp.mean(x * x, axis=-1, keepdims=True)
    h = (x * lax.rsqrt(ms + RMS_EPS) * g_ref[...]).astype(BF16)
    gw = 4 * HEAD_W
    for i, o in enumerate((gqk_ref, gv_ref, gog_ref, dq_ref, dk_ref, dv_ref)):
        o[...] = _dot(h, w_ref[:, i * gw:(i + 1) * gw]).astype(o.dtype)
    lr_ref[...] = _dot(h, w_ref[:, PROJ_GROUPS * gw:])


def _proj(x2, g1, w_r, tm):
    n = x2.shape[0]
    gw = 4 * HEAD_W
    row = lambda i: (i, 0)
    outs = [jax.ShapeDtypeStruct((n, gw), BF16)] * PROJ_GROUPS + [jax.ShapeDtypeStruct((n, LANES), F32)]
    return pl.pallas_call(
        _proj_kernel,
        grid=(n // tm,),
        in_specs=[pl.BlockSpec((tm, D_MODEL), row),
                  pl.BlockSpec((1, D_MODEL), lambda i: (0, 0)),
                  pl.BlockSpec((D_MODEL, PROJ_W), lambda i: (0, 0))],
        out_specs=[pl.BlockSpec((tm, gw), row)] * PROJ_GROUPS + [pl.BlockSpec((tm, LANES), row)],
        out_shape=outs,
        compiler_params=_params(("arbitrary",)),
        name="proj",
    )(x2, g1, w_r)


GLA_BLK = 256


def _gla_kernel(gqk_ref, gv_ref, gog_ref, lr_ref, wg_ref, bg_ref, ng_ref, o_ref,
                qh_ref, u_ref, dec_ref, acc_ref, b_ref, tot_ref, *, T):
    C = GLA_CHUNK
    nblk = T // GLA_BLK
    nch = T // C
    lane = lax.broadcasted_iota(I32, (1, HEAD_W), 1)
    fwd = lane < GLA_DK
    ri = lax.broadcasted_iota(I32, (GLA_BLK, GLA_BLK), 0)
    ci = lax.broadcasted_iota(I32, (GLA_BLK, GLA_BLK), 1)
    same = (ri // C) == (ci // C)
    prefix_total = jnp.concatenate([jnp.where(same & (ci <= ri), 1.0, 0.0),
                                    jnp.where(same, 1.0, 0.0)], axis=0).astype(BF16)
    m_f = same & (ci <= ri)
    m_b = same & (ci > ri)
    wg = wg_ref[...]
    bg = bg_ref[...]
    cpb = GLA_BLK // C
    chunk_of_row = lax.broadcasted_iota(I32, (GLA_BLK, 1), 0) // C

    def gates(i, carry):
        rows = pl.ds(pl.multiple_of(i * GLA_BLK, GLA_BLK), GLA_BLK)
        z = _dot(lr_ref[rows, :].astype(BF16), wg) + bg
        la = (jnp.minimum(z, 0.0) - jnp.log(1.0 + jnp.exp(-jnp.abs(z)))) * (1.0 / GLA_TAU)
        hi = la.astype(BF16)
        lo = (la - hi.astype(F32)).astype(BF16)
        pt = _dot(prefix_total, jnp.concatenate([hi, lo], axis=1))
        pt = pt[:, :HEAD_W] + pt[:, HEAD_W:]
        pre, tot = pt[:GLA_BLK, :], pt[GLA_BLK:, :]
        b_ref[rows, :] = jnp.where(fwd, pre, tot - pre + la)
        tot_ref[rows, :] = tot
        return carry

    lax.fori_loop(0, nblk, gates, 0, unroll=8)

    def block(i, carry):
        rows = pl.ds(pl.multiple_of(i * GLA_BLK, GLA_BLK), GLA_BLK)
        b = b_ref[rows, :]
        tot = tot_ref[rows, :]
        mid = 0.5 * tot
        blk = gqk_ref[rows, :].astype(F32)
        rot = pltpu.roll(blk, GLA_DK, 1)
        qq = jnp.where(fwd, blk, rot) * (GLA_DK ** -0.5)
        kk = jnp.where(fwd, rot, blk)
        qt = qq * jnp.exp(b - mid)
        kt = (kk * jnp.exp(mid - b)).astype(BF16)
        kh = (kk * jnp.exp(tot - b)).astype(BF16)
        qh_ref[rows, :] = (qq * jnp.exp(b)).astype(BF16)
        qfb = jnp.concatenate([jnp.where(fwd, qt, 0.0), jnp.where(fwd, 0.0, qt)], axis=0).astype(BF16)
        sfb = _dot_nt(qfb, kt)
        s = jnp.where(m_f, sfb[:GLA_BLK, :], jnp.where(m_b, sfb[GLA_BLK:, :], 0.0)).astype(BF16)
        v = gv_ref[rows, :]
        acc_ref[rows, :] = _dot(s, v)
        zero = jnp.zeros_like(kh)
        kx = jnp.concatenate([jnp.where(chunk_of_row == c, kh, zero) for c in range(cpb)], axis=1)
        ux = _dot_tn(v, kx)
        dec = jnp.exp(tot)
        for c in range(cpb):
            u_ref[i * cpb + c] = ux[:, c * HEAD_W:(c + 1) * HEAD_W]
            dec_ref[i * cpb + c] = jnp.broadcast_to(dec[c * C:c * C + 1, :], (8, HEAD_W))
        return carry

    lax.fori_loop(0, nblk, block, 0, unroll=8)

    def step(t, st):
        nf = t
        nb = nch - 1 - t
        stb = st.astype(BF16)
        rf = pl.ds(pl.multiple_of(nf * C, C), C)
        rb = pl.ds(pl.multiple_of(nb * C, C), C)
        qf = qh_ref[rf, :]
        qb = qh_ref[rb, :]
        zero = jnp.zeros_like(qf)
        o = _dot_nt(jnp.concatenate([jnp.where(fwd, qf, zero), jnp.where(fwd, zero, qb)], axis=0), stb)
        acc_ref[rf, :] += o[:C, :]
        acc_ref[rb, :] += o[C:, :]
        d = jnp.where(fwd, dec_ref[nf][0:1, :], dec_ref[nb][0:1, :])
        u = jnp.where(fwd, u_ref[nf], u_ref[nb])
        return st * d + u

    lax.fori_loop(0, nch, step, jnp.zeros((GLA_DV, HEAD_W), F32), unroll=8)

    ng = ng_ref[...]
    eb = min(T, 512)

    def epi(i, carry):
        rows = pl.ds(pl.multiple_of(i * eb, eb), eb)
        o = acc_ref[rows, :]
        y = o * lax.rsqrt(jnp.mean(o * o, axis=-1, keepdims=True) + RMS_EPS) * ng
        og = gog_ref[rows, :].astype(F32)
        o_ref[rows, :] = (y * (og * jax.nn.sigmoid(og))).astype(o_ref.dtype)
        return carry

    lax.fori_loop(0, T // eb, epi, 0)


def _gla(gqk, gv, gog, lr, wg, bg, ng, B, T):
    n = B * T
    head = pl.BlockSpec((T, HEAD_W), lambda b, h: (b, h))
    return pl.pallas_call(
        functools.partial(_gla_kernel, T=T),
        grid=(B, GLA_HEADS),
        in_specs=[head, head, head,
                  pl.BlockSpec((T, LANES), lambda b, h: (b, 0)),
                  pl.BlockSpec((None, LANES, HEAD_W), lambda b, h: (h, 0, 0)),
                  pl.BlockSpec((None, 1, HEAD_W), lambda b, h: (h, 0, 0)),
                  pl.BlockSpec((1, HEAD_W), lambda b, h: (0, 0))],
        out_specs=head,
        out_shape=jax.ShapeDtypeStruct((n, GLA_HEADS * HEAD_W), BF16),
        scratch_shapes=[pltpu.VMEM((T, HEAD_W), BF16),
                        pltpu.VMEM((T // GLA_CHUNK, GLA_DV, HEAD_W), F32),
                        pltpu.VMEM((T // GLA_CHUNK, 8, HEAD_W), F32),
                        pltpu.VMEM((T, GLA_DV), F32),
                        pltpu.VMEM((T, HEAD_W), F32),
                        pltpu.VMEM((T, HEAD_W), F32)],
        compiler_params=_params(("arbitrary", "arbitrary")),
        name="gla",
    )(gqk, gv, gog, lr, wg, bg, ng)


DATTN_KSPLIT = 2
VT_PAD = 16
DATTN_SCORE_BYTES = 32 * 1024 * 1024


def _qk_prep(x, gain, cos, sin):
    ri = lax.broadcasted_iota(I32, (HEAD_W, HEAD_W), 0)
    ci = lax.broadcasted_iota(I32, (HEAD_W, HEAD_W), 1)
    same_map = jnp.where((ri // DIFF_DH) == (ci // DIFF_DH), 1.0, 0.0).astype(BF16)
    ms = _dot((x * x).astype(BF16), same_map) * (1.0 / DIFF_DH)
    y = x * lax.rsqrt(ms + RMS_EPS) * gain
    half = DIFF_DH // 2
    first = (lax.broadcasted_iota(I32, (1, HEAD_W), 1) & half) == 0
    swapped = jnp.where(first, pltpu.roll(y, HEAD_W - half, 1), pltpu.roll(y, half, 1))
    return y * cos + swapped * sin


def _dattn_kernel(dq_ref, dk_ref, dv_ref, cos_ref, sin_ref, qg_ref, kg_ref, ng_ref, lam_ref, o_ref,
                  kp_ref, vt_ref, s_ref, mp_ref, acc_ref, *, T, QB, TK, lam_init):
    qi = pl.program_id(2)
    lane = lax.broadcasted_iota(I32, (1, HEAD_W), 1)
    map0 = lane < DIFF_DH
    nk = T // TK

    @pl.when(qi == 0)
    def _():
        kb = min(T, 512)

        def kprep(i, carry):
            rows = pl.ds(pl.multiple_of(i * kb, kb), kb)
            k = _qk_prep(dk_ref[rows, :].astype(F32), kg_ref[...], cos_ref[rows, :], sin_ref[rows, :])
            kp_ref[rows, :] = k.astype(BF16)
            vt_ref[:DIFF_DV, rows] = dv_ref[rows, :].astype(F32).T.astype(BF16)
            return carry
        lax.fori_loop(0, T // kb, kprep, 0)
        sub = lax.broadcasted_iota(I32, (VT_PAD, T), 0)
        vt_ref[DIFF_DV:, :] = jnp.where(sub == 0, 1.0, 0.0).astype(BF16)

    qrows = pl.ds(pl.multiple_of(qi * QB, QB), QB)
    q = _qk_prep(dq_ref[...].astype(F32), qg_ref[...], cos_ref[qrows, :], sin_ref[qrows, :])
    q = q * (DIFF_DH ** -0.5 * LOG2_E)
    qs = jnp.concatenate([jnp.where(map0, q, 0.0), jnp.where(map0, 0.0, q)], axis=0).astype(BF16)
    ntile = TK // LANES

    def lane_tiles(x):
        return [x[:, j * LANES:(j + 1) * LANES] for j in range(ntile)]

    mp_ref[...] = jnp.full(mp_ref.shape, -jnp.inf, F32)
    acc_ref[...] = jnp.zeros(acc_ref.shape, F32)

    def scores(i, carry):
        cols = pl.ds(pl.multiple_of(i * TK, TK), TK)
        s = _dot_nt(qs, kp_ref[cols, :])
        s_ref[:, cols] = s
        mp_ref[...] = jnp.maximum(mp_ref[...], functools.reduce(jnp.maximum, lane_tiles(s)))
        return carry

    lax.fori_loop(0, nk, scores, 0)
    m = jnp.max(mp_ref[...], axis=-1, keepdims=True)

    def accum(i, carry):
        cols = pl.ds(pl.multiple_of(i * TK, TK), TK)
        p = jnp.exp2(s_ref[:, cols] - m)
        acc_ref[...] += _dot_nt(vt_ref[:, cols], p.astype(BF16))
        return carry

    lax.fori_loop(0, nk, accum, 0)
    on = acc_ref[:DIFF_DV, :] / acc_ref[DIFF_DV:DIFF_DV + 1, :]
    lam = lam_ref[...]
    o = (on[:, :QB] - lam * on[:, QB:]).T
    y = o * lax.rsqrt(jnp.mean(o * o, axis=-1, keepdims=True) + RMS_EPS) * ng_ref[...]
    o_ref[...] = (y * (1.0 - lam_init)).astype(o_ref.dtype)


def _lambda_kernel(q1_ref, k1_ref, q2_ref, k2_ref, o_ref, *, lam_init):
    a = jnp.exp(jnp.sum(q1_ref[...] * k1_ref[...], axis=-1, keepdims=True))
    b = jnp.exp(jnp.sum(q2_ref[...] * k2_ref[...], axis=-1, keepdims=True))
    o_ref[...] = a - b + lam_init


def _dattn(dq, dk, dv, cos, sin, qg, kg, ng, lam, B, T, QB, lam_init):
    n = B * T
    TK = T // DATTN_KSPLIT
    nq = T // QB
    full = lambda shape: pl.BlockSpec(shape, lambda b, h, q: (0, 0))
    kv = pl.BlockSpec((T, HEAD_W), lambda b, h, q: (b, h))
    qo = pl.BlockSpec((QB, HEAD_W), lambda b, h, q: (b * nq + q, h))
    return pl.pallas_call(
        functools.partial(_dattn_kernel, T=T, QB=QB, TK=TK, lam_init=lam_init),
        grid=(B, DIFF_HEADS, nq),
        in_specs=[qo, kv, kv, full((T, HEAD_W)), full((T, HEAD_W)),
                  full((1, HEAD_W)), full((1, HEAD_W)), full((1, HEAD_W)), full((1, 1))],
        out_specs=qo,
        out_shape=jax.ShapeDtypeStruct((n, DIFF_HEADS * HEAD_W), BF16),
        scratch_shapes=[pltpu.VMEM((T, HEAD_W), BF16),
                        pltpu.VMEM((DIFF_DV + VT_PAD, T), BF16),
                        pltpu.VMEM((2 * QB, T), F32),
                        pltpu.VMEM((2 * QB, LANES), F32),
                        pltpu.VMEM((DIFF_DV + VT_PAD, 2 * QB), F32)],
        compiler_params=_params(("arbitrary", "arbitrary", "arbitrary")),
        name="dattn",
    )(dq, dk, dv, cos, sin, qg, kg, ng, lam)


def _mixout_kernel(a_ref, d_ref, x_ref, w_ref, g_ref, wr_ref, x1_ref, h2_ref, aff_ref):
    half = GLA_HEADS * HEAD_W
    y = _dot(a_ref[...], w_ref[:half, :]) + _dot(d_ref[...], w_ref[half:, :])
    x1 = x_ref[...] + y
    h2 = x1 * lax.rsqrt(jnp.mean(x1 * x1, axis=-1, keepdims=True) + RMS_EPS) * g_ref[...]
    for s in range(TOK_SUB):
        cols = slice(s * LANES, (s + 1) * LANES)
        x1_ref[_tok_rows(s, x1.shape[0]), :] = x1[:, cols]
        h2_ref[_tok_rows(s, x1.shape[0]), :] = h2[:, cols]
    logits = jnp.zeros((N_EXPERTS, h2.shape[0]), F32)
    hp = _split3(h2)
    wp = _split3(wr_ref[...])
    for i, j in ((0, 0), (0, 1), (1, 0)):
        logits = logits + _dot_nt(wp[i], hp[j])
    m = jnp.max(logits, axis=0, keepdims=True)
    e = jnp.exp(logits - m)
    aff_ref[...] = e / jnp.sum(e, axis=0, keepdims=True)


def _mixout(a, d, x2, w_out, g2, wr_t, tm):
    n = x2.shape[0]
    row = lambda i: (i, 0)
    half = GLA_HEADS * HEAD_W
    return pl.pallas_call(
        _mixout_kernel,
        grid=(n // tm,),
        in_specs=[pl.BlockSpec((tm, half), row), pl.BlockSpec((tm, half), row),
                  pl.BlockSpec((tm, D_MODEL), row),
                  pl.BlockSpec((D_MODEL, D_MODEL), lambda i: (0, 0)),
                  pl.BlockSpec((1, D_MODEL), lambda i: (0, 0)),
                  pl.BlockSpec((N_EXPERTS, D_MODEL), lambda i: (0, 0))],
        out_specs=[pl.BlockSpec((tm * TOK_SUB, LANES), row), pl.BlockSpec((tm * TOK_SUB, LANES), row),
                   pl.BlockSpec((N_EXPERTS, tm), lambda i: (0, i))],
        out_shape=[jax.ShapeDtypeStruct((n * TOK_SUB, LANES), F32), jax.ShapeDtypeStruct((n * TOK_SUB, LANES), F32),
                   jax.ShapeDtypeStruct((N_EXPERTS, n), F32)],
        compiler_params=_params(("arbitrary",)),
        name="mixout",
    )(a, d, x2, w_out, g2, wr_t)


ROUTE_BISECTIONS = 40


def _route_kernel(aff_ref, idx_ref, gate_ref, *, n, cap):
    R = n // LANES
    E = N_EXPERTS
    aff = aff_ref[...]

    def count(mask):
        c = jnp.sum(jnp.where(mask, 1.0, 0.0), axis=1, keepdims=True)
        return jnp.sum(c, axis=2, keepdims=True)

    capf = float(cap)
    k = jnp.zeros((E, 1, 1), F32)
    for bit in (64.0, 32.0, 16.0, 8.0, 4.0, 2.0, 1.0):
        cand = k + bit
        k = jnp.where(count(aff >= jnp.exp2(1.0 - cand)) < capf, cand, k)
    hi = jnp.exp2(1.0 - k)
    lo = jnp.where(k >= 127.0, 0.0, jnp.exp2(-jnp.minimum(k, 126.0)))

    def bisect(i, c):
        lo, hi = c
        mid = 0.5 * (lo + hi)
        ok = count(aff >= mid) >= capf
        return jnp.where(ok, mid, lo), jnp.where(ok, hi, mid)

    lo, hi = lax.fori_loop(0, ROUTE_BISECTIONS, bisect, (lo, hi))
    gt = aff >= hi
    eq = (aff >= lo) & (aff < hi)
    need = capf - count(gt)
    tok = (lax.broadcasted_iota(I32, (1, R, LANES), 1) * LANES
           + lax.broadcasted_iota(I32, (1, R, LANES), 2))
    nbits = max(1, (n - 1).bit_length())

    def tie_step(i, m0):
        cand = m0 | jnp.left_shift(jnp.int32(1), nbits - 1 - i)
        return jnp.where(count(eq & (tok < cand)) < need, cand, m0)

    m0 = lax.fori_loop(0, nbits, tie_step, jnp.zeros((E, 1, 1), I32))
    sel = gt | (eq & (tok <= m0))

    ci = lax.broadcasted_iota(I32, (LANES, LANES), 0)
    cj = lax.broadcasted_iota(I32, (LANES, LANES), 1)
    upper = jnp.where(ci <= cj, 1.0, 0.0).astype(BF16)
    ri = lax.broadcasted_iota(I32, (R, R), 0)
    rj = lax.broadcasted_iota(I32, (R, R), 1)
    lower = jnp.where(rj < ri, 1.0, 0.0).astype(BF16)
    jrow = lax.broadcasted_iota(I32, (1, cap), 1).astype(F32)
    rcol = lax.broadcasted_iota(I32, (R, 1), 0).astype(F32)
    ccol = lax.broadcasted_iota(I32, (LANES, 1), 0).astype(F32)

    for e in range(E):
        sel_e = sel[e]
        w = _dot(jnp.where(sel_e, 1.0, 0.0).astype(BF16), upper)
        rt = w[:, LANES - 1:LANES]
        rp = _dot(lower, jnp.broadcast_to(rt, (R, LANES)).astype(BF16))[:, 0:1]
        rows_t = jnp.where((rp <= jrow) & (jrow < rp + rt), 1.0, 0.0)
        k1 = jrow - jnp.sum(rows_t * rp, axis=0, keepdims=True) + 1.0
        rows_b = rows_t.astype(BF16)
        wrow = _dot(jnp.where(sel_e, w, 0.0).T.astype(BF16), rows_b)
        cols_t = jnp.where(wrow == k1, 1.0, 0.0)
        r_of = jnp.sum(rows_t * rcol, axis=0, keepdims=True)
        c_of = jnp.sum(cols_t * ccol, axis=0, keepdims=True)
        idx_ref[e] = (r_of * float(LANES) + c_of).astype(I32)
        a3 = _split3(aff[e].T)
        arow = _dot(a3[0], rows_b) + _dot(a3[1], rows_b) + _dot(a3[2], rows_b)
        gate_ref[e] = jnp.sum(cols_t * arow, axis=0, keepdims=True)


def _route(aff_t, n, cap):
    R = n // LANES
    return pl.pallas_call(
        functools.partial(_route_kernel, n=n, cap=cap),
        out_shape=[jax.ShapeDtypeStruct((N_EXPERTS, 1, cap), I32),
                   jax.ShapeDtypeStruct((N_EXPERTS, 1, cap), F32)],
        compiler_params=pltpu.CompilerParams(vmem_limit_bytes=VMEM_LIMIT),
        name="route",
    )(aff_t.reshape(N_EXPERTS, R, LANES))


FFN_FC = 512
FFN_TT = 1024
FFN_DMA_UNROLL = 16


def _ffn_kernel(idx_ref, gate_ref, h2_hbm, wg_ref, wu_ref, wd_ref, x1_hbm, out_hbm,
                stage_ref, xb_ref, acc_ref, orow_ref, sems, *, cap, nfc):
    del x1_hbm
    e = pl.program_id(0)
    fc = pl.program_id(1)
    ne = pl.num_programs(0)
    g_sem, o_sem, s_sem = sems.at[0], sems.at[1], sems.at[2]
    tt = min(FFN_TT, cap)

    def row_copy(src, s_tok, dst, d_tok, sem):
        s_rows = pl.ds(pl.multiple_of(s_tok * TOK_SUB, TOK_SUB), TOK_SUB)
        d_rows = pl.ds(pl.multiple_of(d_tok * TOK_SUB, TOK_SUB), TOK_SUB)
        return pltpu.make_async_copy(src.at[s_rows], dst.at[d_rows], sem)

    def issue_gather(ee, src, dst, sem):
        def body(j, c):
            row_copy(src, idx_ref[ee * cap + j], dst, j, sem).start()
            return c
        lax.fori_loop(0, cap, body, 0, unroll=FFN_DMA_UNROLL)

    def wait_rows(buf, sem):
        pltpu.make_async_copy(buf, buf, sem).wait()

    gather_step = nfc - 2
    x_steps = [k for k in range(nfc) if k not in (0, gather_step)]
    assert 0 < gather_step and x_steps, "the copy schedule needs at least three hidden-dim steps"
    ntile = cap // tt
    per_tile = cap // ntile
    x_per_tile = cap // (len(x_steps) * ntile)
    e_next = jnp.minimum(e + 1, ne - 1)

    @pl.when(fc == 0)
    def _():
        @pl.when(e == 0)
        def _():
            issue_gather(0, h2_hbm, stage_ref, g_sem)
        wait_rows(stage_ref, g_sem)
        for s in range(TOK_SUB):
            xb_ref[:, s * LANES:(s + 1) * LANES] = stage_ref[_tok_rows(s, cap), :].astype(BF16)

    @pl.when((fc == gather_step) & (e > 0))
    def _():
        wait_rows(orow_ref, s_sem)

    def finish_rows(ee, tok0, ntok):
        for b in range(ntok // LANES):
            first = tok0 + b * LANES
            rows = pl.ds(first if isinstance(first, int) else pl.multiple_of(first, LANES), LANES)
            g = gate_ref.at[ee][:, rows]
            gcol = jnp.broadcast_to(g, (LANES, LANES)).T
            for s in range(TOK_SUB):
                trows = pl.ds(first * TOK_SUB + s, LANES, stride=TOK_SUB)
                orow_ref[trows, :] += acc_ref[rows, s * LANES:(s + 1) * LANES] * gcol

    def scatter_prev(t):
        for jj in range(per_tile):
            j = t * per_tile + jj
            row_copy(orow_ref, j, out_hbm, idx_ref[(e - 1) * cap + j], s_sem).start()

    def gather_out(t):
        for jj in range(per_tile):
            j = t * per_tile + jj
            row_copy(out_hbm, idx_ref[e * cap + j], orow_ref, j, o_sem).start()

    def gather_next(part):
        def issue(t):
            for jj in range(x_per_tile):
                j = (part * ntile + t) * x_per_tile + jj
                row_copy(h2_hbm, idx_ref[e_next * cap + j], stage_ref, j, g_sem).start()
        return issue

    def tile_loop(copies, first):
        def tile(t, c):
            if copies is not None:
                copies(t)
            rows = pl.ds(pl.multiple_of(t * tt, tt), tt)
            x = xb_ref[rows, :]
            a = _dot(x, wg_ref[...].astype(BF16))
            u = _dot(x, wu_ref[...].astype(BF16))
            hm = (a * jax.nn.sigmoid(a) * u).astype(BF16)
            y = _dot(hm, wd_ref[...].astype(BF16))
            if first:
                acc_ref[rows, :] = y
            else:
                acc_ref[rows, :] += y
            return c
        lax.fori_loop(0, ntile, tile, 0)

    pl.when((fc == 0) & (e == 0))(lambda: tile_loop(None, True))
    pl.when((fc == 0) & (e > 0))(lambda: tile_loop(scatter_prev, True))
    pl.when(fc == gather_step)(lambda: tile_loop(gather_out, False))
    for part, k in enumerate(x_steps):
        pl.when(fc == k)(functools.partial(tile_loop, gather_next(part), False))

    @pl.when(fc == nfc - 1)
    def _():
        wait_rows(orow_ref, o_sem)
        finish_rows(e, 0, cap)

        @pl.when(e == ne - 1)
        def _():
            def body(j, c):
                row_copy(orow_ref, j, out_hbm, idx_ref[e * cap + j], s_sem).start()
                return c
            lax.fori_loop(0, cap, body, 0, unroll=FFN_DMA_UNROLL)
            wait_rows(orow_ref, s_sem)
            wait_rows(stage_ref, g_sem)


def _ffn(idx, gate, h2, x1, w_gate, w_up, w_down, cap):
    n = h2.shape[0] // TOK_SUB
    nfc = D_EXPERT // FFN_FC
    grid_spec = pltpu.PrefetchScalarGridSpec(
        num_scalar_prefetch=1,
        grid=(N_EXPERTS, nfc),
        in_specs=[pl.BlockSpec((N_EXPERTS, 1, cap), lambda e, f, idx: (0, 0, 0)),
                  pl.BlockSpec(memory_space=pl.ANY),
                  pl.BlockSpec((None, D_MODEL, FFN_FC), lambda e, f, idx: (e, 0, f)),
                  pl.BlockSpec((None, D_MODEL, FFN_FC), lambda e, f, idx: (e, 0, f)),
                  pl.BlockSpec((None, FFN_FC, D_MODEL), lambda e, f, idx: (e, f, 0)),
                  pl.BlockSpec(memory_space=pl.ANY)],
        out_specs=pl.BlockSpec(memory_space=pl.ANY),
        scratch_shapes=[pltpu.VMEM((cap * TOK_SUB, LANES), F32),
                        pltpu.VMEM((cap, D_MODEL), BF16),
                        pltpu.VMEM((cap, D_MODEL), F32),
                        pltpu.VMEM((cap * TOK_SUB, LANES), F32),
                        pltpu.SemaphoreType.DMA((3,))],
    )
    return pl.pallas_call(
        functools.partial(_ffn_kernel, cap=cap, nfc=nfc),
        grid_spec=grid_spec,
        out_shape=jax.ShapeDtypeStruct((n * TOK_SUB, LANES), F32),
        input_output_aliases={6: 0},
        compiler_params=_params(("arbitrary", "arbitrary")),
        name="ffn",
    )(idx.reshape(-1), gate, h2, w_gate, w_up, w_down, x1)


def _untile_kernel(x_ref, o_ref):
    for s in range(TOK_SUB):
        o_ref[:, s * LANES:(s + 1) * LANES] = x_ref[_tok_rows(s, o_ref.shape[0]), :]


def _untile(xt, tm):
    n = xt.shape[0] // TOK_SUB
    return pl.pallas_call(
        _untile_kernel,
        grid=(n // tm,),
        in_specs=[pl.BlockSpec((tm * TOK_SUB, LANES), lambda i: (i, 0))],
        out_specs=pl.BlockSpec((tm, D_MODEL), lambda i: (i, 0)),
        out_shape=jax.ShapeDtypeStruct((n, D_MODEL), F32),
        compiler_params=_params(("arbitrary",)),
        name="untile",
    )(xt)


def _rope_tables(T):
    half = DIFF_DH // 2
    inv_freq = jnp.power(ROPE_THETA, -jnp.arange(0, DIFF_DH, 2, dtype=F32) / DIFF_DH)
    ang = jnp.arange(T, dtype=F32)[:, None] * inv_freq[None, :]
    cos, sin = jnp.cos(ang), jnp.sin(ang)
    reps = HEAD_W // DIFF_DH
    return (jnp.concatenate([cos, cos] * reps, axis=-1),
            jnp.concatenate([-sin, sin] * reps, axis=-1))


def _layout_weights(w_in, gla_wg_f, gla_bg_f, gla_wg_b, gla_bg_b):
    q0, k0, v0, g0 = 0, 256, 512, 1024
    lr0, dq0 = 1536, 1568
    cols = []
    for h in range(GLA_HEADS):
        cols += [w_in[:, q0 + h * GLA_DK:q0 + (h + 1) * GLA_DK], w_in[:, k0 + h * GLA_DK:k0 + (h + 1) * GLA_DK]]
    cols += [w_in[:, v0:g0], w_in[:, g0:lr0], w_in[:, dq0:], w_in[:, lr0:dq0],
             jnp.zeros((D_MODEL, LANES - 2 * GLA_RANK), w_in.dtype)]
    w_r = jnp.concatenate(cols, axis=1).astype(BF16)
    wg = jnp.zeros((GLA_HEADS, LANES, HEAD_W), F32)
    bg = jnp.zeros((GLA_HEADS, 1, HEAD_W), F32)
    for h in range(GLA_HEADS):
        hs = slice(h * GLA_DK, (h + 1) * GLA_DK)
        wg = wg.at[h, :GLA_RANK, :GLA_DK].set(gla_wg_f[:, hs])
        wg = wg.at[h, GLA_RANK:2 * GLA_RANK, GLA_DK:].set(gla_wg_b[:, hs])
        bg = bg.at[h, 0, :GLA_DK].set(gla_bg_f[hs])
        bg = bg.at[h, 0, GLA_DK:].set(gla_bg_b[hs])
    return w_r, wg.astype(BF16), bg


def _lambda(lq1, lk1, lq2, lk2, lam_init):
    pad = lambda v: jnp.pad(v.astype(F32), (0, LANES - v.shape[0])).reshape(1, LANES)
    return pl.pallas_call(
        functools.partial(_lambda_kernel, lam_init=lam_init),
        out_shape=jax.ShapeDtypeStruct((1, 1), F32),
        name="lam",
    )(pad(lq1), pad(lk1), pad(lq2), pad(lk2))


def _encoder_layer(x, layer, lw, *, tm=512):
    (norm1_g, w_in, gla_wg_f, gla_bg_f, gla_wg_b, gla_bg_b, gla_norm_g, qk_norm_q, qk_norm_k,
     lambda_q1, lambda_k1, lambda_q2, lambda_k2, diff_norm_g, w_out, norm2_g, w_router,
     w_gate, w_up, w_down) = lw
    B, T, D = x.shape
    n = B * T
    cap = max(1, EC_FACTOR * n // N_EXPERTS)
    tm = min(tm, n)
    qb = min(T, DATTN_SCORE_BYTES // (2 * 4 * T))
    x2 = x.reshape(n, D)
    w_r, wg, bg = _layout_weights(w_in, gla_wg_f, gla_bg_f, gla_wg_b, gla_bg_b)
    row = lambda v: v.astype(F32).reshape(1, -1)
    two = lambda v: jnp.concatenate([v, v]).astype(F32).reshape(1, -1)

    gqk, gv, gog, dq, dk, dv, lr = _proj(x2, row(norm1_g), w_r, tm)
    a = _gla(gqk, gv, gog, lr, wg, bg, row(gla_norm_g), B, T)
    cos, sin = _rope_tables(T)
    lam_init = _lambda_init(layer)
    lam = _lambda(lambda_q1, lambda_k1, lambda_q2, lambda_k2, lam_init)
    d = _dattn(dq, dk, dv, cos, sin, two(qk_norm_q), two(qk_norm_k), row(diff_norm_g), lam, B, T, qb, lam_init)
    x1, h2, aff_t = _mixout(a, d, x2, w_out.astype(BF16), row(norm2_g), w_router.T.astype(F32), tm)
    idx, gate = _route(aff_t, n, cap)
    y = _untile(_ffn(idx, gate, h2, x1, w_gate, w_up, w_down, cap), tm)
    return y.reshape(B, T, D)


def kernel(x_prompt, x_sample, norm1_g, w_in, gla_wg_f, gla_bg_f, gla_wg_b, gla_bg_b, gla_norm_g, qk_norm_q, qk_norm_k, lambda_q1, lambda_k1, lambda_q2, lambda_k2, diff_norm_g, w_out, norm2_g, w_router, w_gate, w_up, w_down):
    y_prompt, y_sample = x_prompt, x_sample
    for l in range(norm1_g.shape[0]):
        lw = (norm1_g[l], w_in[l], gla_wg_f[l], gla_bg_f[l], gla_wg_b[l], gla_bg_b[l],
              gla_norm_g[l], qk_norm_q[l], qk_norm_k[l], lambda_q1[l], lambda_k1[l],
              lambda_q2[l], lambda_k2[l], diff_norm_g[l], w_out[l], norm2_g[l],
              w_router[l], w_gate[l], w_up[l], w_down[l])
        y_prompt = _encoder_layer(y_prompt, l, lw)
        y_sample = _encoder_layer(y_sample, l, lw)
    return (y_prompt, y_sample)
```

```python
import functools
import math

import jax
import jax.numpy as jnp
from jax import lax
from jax.experimental import pallas as pl
from jax.experimental.pallas import tpu as pltpu

F32 = jnp.float32
BF16 = jnp.bfloat16
I32 = jnp.int32

LANES = 128
VMEM_LIMIT = 56 * 1024 * 1024

D_MODEL = 1024
GLA_HEADS = 4
GLA_DK = 64
GLA_DV = 128
GLA_RANK = 16
GLA_TAU = 16.0
GLA_CHUNK = 64
DIFF_HEADS = 4
DIFF_DH = 64
DIFF_DV = 128
ROPE_THETA = 10000.0
N_EXPERTS = 16
EC_FACTOR = 2
D_EXPERT = 2 * D_MODEL
RMS_EPS = 1e-6
LOG2_E = 1.4426950408889634


def _lambda_init(layer):
    return 0.8 - 0.6 * math.exp(-0.3 * layer)


HEAD_W = 128
TOK_SUB = D_MODEL // LANES


def _tok_rows(s, ntok, first_tok=0):
    return pl.ds(first_tok * TOK_SUB + s, ntok, stride=TOK_SUB)
PROJ_GROUPS = 6
PROJ_W = PROJ_GROUPS * 4 * HEAD_W + LANES


def _dot(a, b):
    return jnp.dot(a, b, preferred_element_type=F32)


def _dot_nt(a, b):
    return lax.dot_general(a, b, (((1,), (1,)), ((), ())), preferred_element_type=F32)


def _dot_tn(a, b):
    return lax.dot_general(a, b, (((0,), (0,)), ((), ())), preferred_element_type=F32)


def _split3(x):
    a = x.astype(BF16)
    r = x - a.astype(F32)
    b = r.astype(BF16)
    c = (r - b.astype(F32)).astype(BF16)
    return a, b, c


def _params(sem):
    return pltpu.CompilerParams(dimension_semantics=sem, vmem_limit_bytes=VMEM_LIMIT)


def _qk_prep(x, gain, cos, sin, same_map):
    ms = _dot((x * x).astype(BF16), same_map) * (1.0 / DIFF_DH)
    y = x * lax.rsqrt(ms + RMS_EPS) * gain
    width = x.shape[1]
    half = DIFF_DH // 2
    first = (lax.broadcasted_iota(I32, (1, width), 1) & half) == 0
    swapped = jnp.where(first, pltpu.roll(y, width - half, 1), pltpu.roll(y, half, 1))
    return y * cos + swapped * sin


def _proj_kernel(x_ref, g_ref, w_ref, cos_ref, sin_ref, qg_ref, kg_ref, map_ref,
                 gqk_ref, gv_ref, gog_ref, dq_ref, dk_ref, dv_ref, lr_ref):
    x = x_ref[...]
    ms = jnp.mean(x * x, axis=-1, keepdims=True)
    h = (x * lax.rsqrt(ms + RMS_EPS) * g_ref[...]).astype(BF16)
    gw = 4 * HEAD_W
    group = lambda i: _dot(h, w_ref[:, i * gw:(i + 1) * gw])
    for i, o in ((0, gqk_ref), (1, gv_ref), (2, gog_ref), (5, dv_ref)):
        o[...] = group(i).astype(o.dtype)
    cos = jnp.concatenate([cos_ref[...]] * DIFF_HEADS, axis=1)
    sin = jnp.concatenate([sin_ref[...]] * DIFF_HEADS, axis=1)
    q = _qk_prep(group(3), qg_ref[...], cos, sin, map_ref[...])
    dq_ref[...] = (q * (DIFF_DH ** -0.5 * LOG2_E)).astype(dq_ref.dtype)
    dk_ref[...] = _qk_prep(group(4), kg_ref[...], cos, sin, map_ref[...]).astype(dk_ref.dtype)
    lr_ref[...] = _dot(h, w_ref[:, PROJ_GROUPS * gw:])


def _proj(x2, g1, w_r, cos, sin, qg, kg, tm):
    n = x2.shape[0]
    T = cos.shape[0]
    gw = 4 * HEAD_W
    row = lambda i: (i, 0)
    const = lambda i: (0, 0)
    pos = lambda i: (i % (T // tm), 0)
    ri = lax.broadcasted_iota(I32, (gw, gw), 0)
    ci = lax.broadcasted_iota(I32, (gw, gw), 1)
    same_map = ((ri // DIFF_DH) == (ci // DIFF_DH)).astype(BF16)
    outs = [jax.ShapeDtypeStruct((n, gw), BF16)] * PROJ_GROUPS + [jax.ShapeDtypeStruct((n, LANES), F32)]
    return pl.pallas_call(
        _proj_kernel,
        grid=(n // tm,),
        in_specs=[pl.BlockSpec((tm, D_MODEL), row),
                  pl.BlockSpec((1, D_MODEL), const),
                  pl.BlockSpec((D_MODEL, PROJ_W), const),
                  pl.BlockSpec((tm, HEAD_W), pos), pl.BlockSpec((tm, HEAD_W), pos),
                  pl.BlockSpec((1, gw), const), pl.BlockSpec((1, gw), const),
                  pl.BlockSpec((gw, gw), const)],
        out_specs=[pl.BlockSpec((tm, gw), row)] * PROJ_GROUPS + [pl.BlockSpec((tm, LANES), row)],
        out_shape=outs,
        compiler_params=_params(("arbitrary",)),
        name="proj",
    )(x2, g1, w_r, cos, sin, qg, kg, same_map)


GLA_BLK = 256


def _gla_kernel(gqk_ref, gv_ref, gog_ref, lr_ref, wg_ref, bg_ref, ng_ref, o_ref,
                qh_ref, u_ref, dec_ref, acc_ref, b_ref, tot_ref, *, T):
    C = GLA_CHUNK
    nblk = T // GLA_BLK
    nch = T // C
    lane = lax.broadcasted_iota(I32, (1, HEAD_W), 1)
    fwd = lane < GLA_DK
    ri = lax.broadcasted_iota(I32, (GLA_BLK, GLA_BLK), 0)
    ci = lax.broadcasted_iota(I32, (GLA_BLK, GLA_BLK), 1)
    same = (ri // C) == (ci // C)
    prefix_total = jnp.concatenate([jnp.where(same & (ci <= ri), 1.0, 0.0),
                                    jnp.where(same, 1.0, 0.0)], axis=0).astype(BF16)
    m_f = same & (ci <= ri)
    m_b = same & (ci > ri)
    wg = wg_ref[...]
    bg = bg_ref[...]
    cpb = GLA_BLK // C
    chunk_of_row = lax.broadcasted_iota(I32, (GLA_BLK, 1), 0) // C

    def gates(i, carry):
        rows = pl.ds(pl.multiple_of(i * GLA_BLK, GLA_BLK), GLA_BLK)
        z = _dot(lr_ref[rows, :].astype(BF16), wg) + bg
        la = (jnp.minimum(z, 0.0) - jnp.log(1.0 + jnp.exp(-jnp.abs(z)))) * (1.0 / GLA_TAU)
        hi = la.astype(BF16)
        lo = (la - hi.astype(F32)).astype(BF16)
        pt = _dot(prefix_total, jnp.concatenate([hi, lo], axis=1))
        pt = pt[:, :HEAD_W] + pt[:, HEAD_W:]
        pre, tot = pt[:GLA_BLK, :], pt[GLA_BLK:, :]
        b_ref[rows, :] = jnp.where(fwd, pre, tot - pre + la)
        tot_ref[rows, :] = tot
        return carry

    lax.fori_loop(0, nblk, gates, 0, unroll=8)

    def block(i, carry):
        rows = pl.ds(pl.multiple_of(i * GLA_BLK, GLA_BLK), GLA_BLK)
        b = b_ref[rows, :]
        tot = tot_ref[rows, :]
        mid = 0.5 * tot
        blk = gqk_ref[rows, :].astype(F32)
        rot = pltpu.roll(blk, GLA_DK, 1)
        qq = jnp.where(fwd, blk, rot) * (GLA_DK ** -0.5)
        kk = jnp.where(fwd, rot, blk)
        qt = qq * jnp.exp(b - mid)
        kt = (kk * jnp.exp(mid - b)).astype(BF16)
        kh = (kk * jnp.exp(tot - b)).astype(BF16)
        qh_ref[rows, :] = (qq * jnp.exp(b)).astype(BF16)
        qfb = jnp.concatenate([jnp.where(fwd, qt, 0.0), jnp.where(fwd, 0.0, qt)], axis=0).astype(BF16)
        sfb = _dot_nt(qfb, kt)
        s = jnp.where(m_f, sfb[:GLA_BLK, :], jnp.where(m_b, sfb[GLA_BLK:, :], 0.0)).astype(BF16)
        v = gv_ref[rows, :]
        acc_ref[rows, :] = _dot(s, v)
        zero = jnp.zeros_like(kh)
        kx = jnp.concatenate([jnp.where(chunk_of_row == c, kh, zero) for c in range(cpb)], axis=1)
        ux = _dot_tn(v, kx)
        dec = jnp.exp(tot)
        for c in range(cpb):
            u_ref[i * cpb + c] = ux[:, c * HEAD_W:(c + 1) * HEAD_W]
            dec_ref[i * cpb + c] = jnp.broadcast_to(dec[c * C:c * C + 1, :], (8, HEAD_W))
        return carry

    lax.fori_loop(0, nblk, block, 0, unroll=8)

    def step(t, st):
        nf = t
        nb = nch - 1 - t
        stb = st.astype(BF16)
        rf = pl.ds(pl.multiple_of(nf * C, C), C)
        rb = pl.ds(pl.multiple_of(nb * C, C), C)
        qf = qh_ref[rf, :]
        qb = qh_ref[rb, :]
        zero = jnp.zeros_like(qf)
        o = _dot_nt(jnp.concatenate([jnp.where(fwd, qf, zero), jnp.where(fwd, zero, qb)], axis=0), stb)
        acc_ref[rf, :] += o[:C, :]
        acc_ref[rb, :] += o[C:, :]
        d = jnp.where(fwd, dec_ref[nf][0:1, :], dec_ref[nb][0:1, :])
        u = jnp.where(fwd, u_ref[nf], u_ref[nb])
        return st * d + u

    lax.fori_loop(0, nch, step, jnp.zeros((GLA_DV, HEAD_W), F32), unroll=8)

    ng = ng_ref[...]
    eb = min(T, 512)

    def epi(i, carry):
        rows = pl.ds(pl.multiple_of(i * eb, eb), eb)
        o = acc_ref[rows, :]
        y = o * lax.rsqrt(jnp.mean(o * o, axis=-1, keepdims=True) + RMS_EPS) * ng
        og = gog_ref[rows, :].astype(F32)
        o_ref[rows, :] = (y * (og * jax.nn.sigmoid(og))).astype(o_ref.dtype)
        return carry

    lax.fori_loop(0, T // eb, epi, 0)


def _gla(gqk, gv, gog, lr, wg, bg, ng, B, T):
    n = B * T
    head = pl.BlockSpec((T, HEAD_W), lambda b, h: (b, h))
    return pl.pallas_call(
        functools.partial(_gla_kernel, T=T),
        grid=(B, GLA_HEADS),
        in_specs=[head, head, head,
                  pl.BlockSpec((T, LANES), lambda b, h: (b, 0)),
                  pl.BlockSpec((None, LANES, HEAD_W), lambda b, h: (h, 0, 0)),
                  pl.BlockSpec((None, 1, HEAD_W), lambda b, h: (h, 0, 0)),
                  pl.BlockSpec((1, HEAD_W), lambda b, h: (0, 0))],
        out_specs=head,
        out_shape=jax.ShapeDtypeStruct((n, GLA_HEADS * HEAD_W), BF16),
        scratch_shapes=[pltpu.VMEM((T, HEAD_W), BF16),
                        pltpu.VMEM((T // GLA_CHUNK, GLA_DV, HEAD_W), F32),
                        pltpu.VMEM((T // GLA_CHUNK, 8, HEAD_W), F32),
                        pltpu.VMEM((T, GLA_DV), F32),
                        pltpu.VMEM((T, HEAD_W), F32),
                        pltpu.VMEM((T, HEAD_W), F32)],
        compiler_params=_params(("arbitrary", "arbitrary")),
        name="gla",
    )(gqk, gv, gog, lr, wg, bg, ng)


DATTN_KSPLIT = 2
DATTN_SCORE_BYTES = 32 * 1024 * 1024


def _dattn_kernel(dq_ref, dk_ref, dv_ref, ng_ref, lam_ref, o_ref,
                  s_ref, mp_ref, lp_ref, acc_ref, *, T, QB, TK, lam_init):
    lane = lax.broadcasted_iota(I32, (1, HEAD_W), 1)
    map0 = lane < DIFF_DH
    nk = T // TK
    q = dq_ref[...]
    zero = jnp.zeros_like(q)
    qs = jnp.concatenate([jnp.where(map0, q, zero), jnp.where(map0, zero, q)], axis=0)
    ntile = TK // LANES

    def lane_tiles(x):
        return [x[:, j * LANES:(j + 1) * LANES] for j in range(ntile)]

    mp_ref[...] = jnp.full(mp_ref.shape, -jnp.inf, F32)
    lp_ref[...] = jnp.zeros(lp_ref.shape, F32)
    acc_ref[...] = jnp.zeros(acc_ref.shape, F32)

    def scores(i, carry):
        cols = pl.ds(pl.multiple_of(i * TK, TK), TK)
        s = _dot_nt(qs, dk_ref[cols, :])
        s_ref[:, cols] = s
        mp_ref[...] = jnp.maximum(mp_ref[...], functools.reduce(jnp.maximum, lane_tiles(s)))
        return carry

    lax.fori_loop(0, nk, scores, 0)
    m = jnp.max(mp_ref[...], axis=-1, keepdims=True)

    def accum(i, carry):
        cols = pl.ds(pl.multiple_of(i * TK, TK), TK)
        p = jnp.exp2(s_ref[:, cols] - m)
        lp_ref[...] += functools.reduce(jnp.add, lane_tiles(p))
        acc_ref[...] += _dot(p.astype(BF16), dv_ref[cols, :])
        return carry

    lax.fori_loop(0, nk, accum, 0)
    on = acc_ref[...] / jnp.sum(lp_ref[...], axis=-1, keepdims=True)
    lam = lam_ref[...]
    o = on[:QB, :] - lam * on[QB:, :]
    y = o * lax.rsqrt(jnp.mean(o * o, axis=-1, keepdims=True) + RMS_EPS) * ng_ref[...]
    o_ref[...] = (y * (1.0 - lam_init)).astype(o_ref.dtype)


def _lambda_kernel(q1_ref, k1_ref, q2_ref, k2_ref, o_ref, *, lam_init):
    a = jnp.exp(jnp.sum(q1_ref[...] * k1_ref[...], axis=-1, keepdims=True))
    b = jnp.exp(jnp.sum(q2_ref[...] * k2_ref[...], axis=-1, keepdims=True))
    o_ref[...] = a - b + lam_init


def _dattn(dq, dk, dv, ng, lam, B, T, QB, lam_init):
    n = B * T
    TK = T // DATTN_KSPLIT
    nq = T // QB
    full = lambda shape: pl.BlockSpec(shape, lambda b, h, q: (0, 0))
    kv = pl.BlockSpec((T, HEAD_W), lambda b, h, q: (b, h))
    qo = pl.BlockSpec((QB, HEAD_W), lambda b, h, q: (b * nq + q, h))
    return pl.pallas_call(
        functools.partial(_dattn_kernel, T=T, QB=QB, TK=TK, lam_init=lam_init),
        grid=(B, DIFF_HEADS, nq),
        in_specs=[qo, kv, kv, full((1, HEAD_W)), full((1, 1))],
        out_specs=qo,
        out_shape=jax.ShapeDtypeStruct((n, DIFF_HEADS * HEAD_W), BF16),
        scratch_shapes=[pltpu.VMEM((2 * QB, T), F32),
                        pltpu.VMEM((2 * QB, LANES), F32),
                        pltpu.VMEM((2 * QB, LANES), F32),
                        pltpu.VMEM((2 * QB, DIFF_DV), F32)],
        compiler_params=_params(("arbitrary", "arbitrary", "arbitrary")),
        name="dattn",
    )(dq, dk, dv, ng, lam)


def _mixout_kernel(a_ref, d_ref, x_ref, w_ref, g_ref, wr_ref, x1_ref, h2_ref, aff_ref):
    half = GLA_HEADS * HEAD_W
    y = _dot(a_ref[...], w_ref[:half, :]) + _dot(d_ref[...], w_ref[half:, :])
    x1 = x_ref[...] + y
    h2 = x1 * lax.rsqrt(jnp.mean(x1 * x1, axis=-1, keepdims=True) + RMS_EPS) * g_ref[...]
    for s in range(TOK_SUB):
        cols = slice(s * LANES, (s + 1) * LANES)
        x1_ref[_tok_rows(s, x1.shape[0]), :] = x1[:, cols]
        h2_ref[_tok_rows(s, x1.shape[0]), :] = h2[:, cols]
    logits = jnp.zeros((N_EXPERTS, h2.shape[0]), F32)
    hp = _split3(h2)
    wp = _split3(wr_ref[...])
    for i, j in ((0, 0), (0, 1), (1, 0)):
        logits = logits + _dot_nt(wp[i], hp[j])
    m = jnp.max(logits, axis=0, keepdims=True)
    e = jnp.exp(logits - m)
    aff_ref[...] = e / jnp.sum(e, axis=0, keepdims=True)


def _mixout(a, d, x2, w_out, g2, wr_t, tm):
    n = x2.shape[0]
    row = lambda i: (i, 0)
    half = GLA_HEADS * HEAD_W
    return pl.pallas_call(
        _mixout_kernel,
        grid=(n // tm,),
        in_specs=[pl.BlockSpec((tm, half), row), pl.BlockSpec((tm, half), row),
                  pl.BlockSpec((tm, D_MODEL), row),
                  pl.BlockSpec((D_MODEL, D_MODEL), lambda i: (0, 0)),
                  pl.BlockSpec((1, D_MODEL), lambda i: (0, 0)),
                  pl.BlockSpec((N_EXPERTS, D_MODEL), lambda i: (0, 0))],
        out_specs=[pl.BlockSpec((tm * TOK_SUB, LANES), row), pl.BlockSpec((tm * TOK_SUB, LANES), row),
                   pl.BlockSpec((N_EXPERTS, tm), lambda i: (0, i))],
        out_shape=[jax.ShapeDtypeStruct((n * TOK_SUB, LANES), F32), jax.ShapeDtypeStruct((n * TOK_SUB, LANES), F32),
                   jax.ShapeDtypeStruct((N_EXPERTS, n), F32)],
        compiler_params=_params(("arbitrary",)),
        name="mixout",
    )(a, d, x2, w_out, g2, wr_t)


ROUTE_BISECTIONS = 40


def _route_kernel(aff_ref, idx_ref, gate_ref, *, n, cap):
    R = n // LANES
    E = N_EXPERTS
    aff = aff_ref[...]

    def count(mask):
        c = jnp.sum(jnp.where(mask, 1.0, 0.0), axis=1, keepdims=True)
        return jnp.sum(c, axis=2, keepdims=True)

    capf = float(cap)
    k = jnp.zeros((E, 1, 1), F32)
    for bit in (64.0, 32.0, 16.0, 8.0, 4.0, 2.0, 1.0):
        cand = k + bit
        k = jnp.where(count(aff >= jnp.exp2(1.0 - cand)) < capf, cand, k)
    hi = jnp.exp2(1.0 - k)
    lo = jnp.where(k >= 127.0, 0.0, jnp.exp2(-jnp.minimum(k, 126.0)))

    def bisect(i, c):
        lo, hi = c
        mid = 0.5 * (lo + hi)
        ok = count(aff >= mid) >= capf
        return jnp.where(ok, mid, lo), jnp.where(ok, hi, mid)

    lo, hi = lax.fori_loop(0, ROUTE_BISECTIONS, bisect, (lo, hi))
    gt = aff >= hi
    eq = (aff >= lo) & (aff < hi)
    need = capf - count(gt)
    tok = (lax.broadcasted_iota(I32, (1, R, LANES), 1) * LANES
           + lax.broadcasted_iota(I32, (1, R, LANES), 2))
    nbits = max(1, (n - 1).bit_length())

    def tie_step(i, m0):
        cand = m0 | jnp.left_shift(jnp.int32(1), nbits - 1 - i)
        return jnp.where(count(eq & (tok < cand)) < need, cand, m0)

    m0 = lax.fori_loop(0, nbits, tie_step, jnp.zeros((E, 1, 1), I32))
    sel = gt | (eq & (tok <= m0))

    ci = lax.broadcasted_iota(I32, (LANES, LANES), 0)
    cj = lax.broadcasted_iota(I32, (LANES, LANES), 1)
    upper = jnp.where(ci <= cj, 1.0, 0.0).astype(BF16)
    ri = lax.broadcasted_iota(I32, (R, R), 0)
    rj = lax.broadcasted_iota(I32, (R, R), 1)
    lower = jnp.where(rj < ri, 1.0, 0.0).astype(BF16)
    jrow = lax.broadcasted_iota(I32, (1, cap), 1).astype(F32)
    rcol = lax.broadcasted_iota(I32, (R, 1), 0).astype(F32)
    ccol = lax.broadcasted_iota(I32, (LANES, 1), 0).astype(F32)

    for e in range(E):
        sel_e = sel[e]
        w = _dot(jnp.where(sel_e, 1.0, 0.0).astype(BF16), upper)
        rt = w[:, LANES - 1:LANES]
        rp = _dot(lower, jnp.broadcast_to(rt, (R, LANES)).astype(BF16))[:, 0:1]
        rows_t = jnp.where((rp <= jrow) & (jrow < rp + rt), 1.0, 0.0)
        k1 = jrow - jnp.sum(rows_t * rp, axis=0, keepdims=True) + 1.0
        rows_b = rows_t.astype(BF16)
        wrow = _dot(jnp.where(sel_e, w, 0.0).T.astype(BF16), rows_b)
        cols_t = jnp.where(wrow == k1, 1.0, 0.0)
        r_of = jnp.sum(rows_t * rcol, axis=0, keepdims=True)
        c_of = jnp.sum(cols_t * ccol, axis=0, keepdims=True)
        idx_ref[e] = (r_of * float(LANES) + c_of).astype(I32)
        a3 = _split3(aff[e].T)
        arow = _dot(a3[0], rows_b) + _dot(a3[1], rows_b) + _dot(a3[2], rows_b)
        gate_ref[e] = jnp.sum(cols_t * arow, axis=0, keepdims=True)


def _route(aff_t, n, cap):
    R = n // LANES
    return pl.pallas_call(
        functools.partial(_route_kernel, n=n, cap=cap),
        out_shape=[jax.ShapeDtypeStruct((N_EXPERTS, 1, cap), I32),
                   jax.ShapeDtypeStruct((N_EXPERTS, 1, cap), F32)],
        compiler_params=pltpu.CompilerParams(vmem_limit_bytes=VMEM_LIMIT),
        name="route",
    )(aff_t.reshape(N_EXPERTS, R, LANES))


FFN_FC = 512
FFN_TT = 1024
FFN_DMA_UNROLL = 16


def _ffn_kernel(idx_ref, gate_ref, h2_hbm, wg_ref, wu_ref, wd_ref, x1_hbm, out_hbm,
                stage_ref, xb_ref, acc_ref, orow_ref, sems, *, cap, nfc):
    del x1_hbm
    e = pl.program_id(0)
    fc = pl.program_id(1)
    ne = pl.num_programs(0)
    g_sem, o_sem, s_sem = sems.at[0], sems.at[1], sems.at[2]
    tt = min(FFN_TT, cap)

    def row_copy(src, s_tok, dst, d_tok, sem):
        s_rows = pl.ds(pl.multiple_of(s_tok * TOK_SUB, TOK_SUB), TOK_SUB)
        d_rows = pl.ds(pl.multiple_of(d_tok * TOK_SUB, TOK_SUB), TOK_SUB)
        return pltpu.make_async_copy(src.at[s_rows], dst.at[d_rows], sem)

    def issue_gather(ee, src, dst, sem):
        def body(j, c):
            row_copy(src, idx_ref[ee * cap + j], dst, j, sem).start()
            return c
        lax.fori_loop(0, cap, body, 0, unroll=FFN_DMA_UNROLL)

    def wait_rows(buf, sem):
        pltpu.make_async_copy(buf, buf, sem).wait()

    gather_step = nfc - 2
    x_steps = [k for k in range(nfc) if k not in (0, gather_step)]
    assert 0 < gather_step and x_steps, "the copy schedule needs at least three hidden-dim steps"
    ntile = cap // tt
    per_tile = cap // ntile
    x_per_tile = cap // (len(x_steps) * ntile)
    e_next = jnp.minimum(e + 1, ne - 1)

    @pl.when(fc == 0)
    def _():
        @pl.when(e == 0)
        def _():
            issue_gather(0, h2_hbm, stage_ref, g_sem)
        wait_rows(stage_ref, g_sem)
        for s in range(TOK_SUB):
            xb_ref[:, s * LANES:(s + 1) * LANES] = stage_ref[_tok_rows(s, cap), :].astype(BF16)

    @pl.when((fc == gather_step) & (e > 0))
    def _():
        wait_rows(orow_ref, s_sem)

    def finish_rows(ee, tok0, ntok):
        for b in range(ntok // LANES):
            first = tok0 + b * LANES
            rows = pl.ds(first if isinstance(first, int) else pl.multiple_of(first, LANES), LANES)
            g = gate_ref.at[ee][:, rows]
            gcol = jnp.broadcast_to(g, (LANES, LANES)).T
            for s in range(TOK_SUB):
                trows = pl.ds(first * TOK_SUB + s, LANES, stride=TOK_SUB)
                orow_ref[trows, :] += acc_ref[rows, s * LANES:(s + 1) * LANES] * gcol

    def scatter_prev(t):
        for jj in range(per_tile):
            j = t * per_tile + jj
            row_copy(orow_ref, j, out_hbm, idx_ref[(e - 1) * cap + j], s_sem).start()

    def gather_out(t):
        for jj in range(per_tile):
            j = t * per_tile + jj
            row_copy(out_hbm, idx_ref[e * cap + j], orow_ref, j, o_sem).start()

    def gather_next(part):
        def issue(t):
            for jj in range(x_per_tile):
                j = (part * ntile + t) * x_per_tile + jj
                row_copy(h2_hbm, idx_ref[e_next * cap + j], stage_ref, j, g_sem).start()
        return issue

    def tile_loop(copies, first):
        def tile(t, c):
            if copies is not None:
                copies(t)
            rows = pl.ds(pl.multiple_of(t * tt, tt), tt)
            x = xb_ref[rows, :]
            a = _dot(x, wg_ref[...].astype(BF16))
            u = _dot(x, wu_ref[...].astype(BF16))
            hm = (a * jax.nn.sigmoid(a) * u).astype(BF16)
            y = _dot(hm, wd_ref[...].astype(BF16))
            if first:
                acc_ref[rows, :] = y
            else:
                acc_ref[rows, :] += y
            return c
        lax.fori_loop(0, ntile, tile, 0)

    pl.when((fc == 0) & (e == 0))(lambda: tile_loop(None, True))
    pl.when((fc == 0) & (e > 0))(lambda: tile_loop(scatter_prev, True))
    pl.when(fc == gather_step)(lambda: tile_loop(gather_out, False))
    for part, k in enumerate(x_steps):
        pl.when(fc == k)(functools.partial(tile_loop, gather_next(part), False))

    @pl.when(fc == nfc - 1)
    def _():
        wait_rows(orow_ref, o_sem)
        finish_rows(e, 0, cap)

        @pl.when(e == ne - 1)
        def _():
            def body(j, c):
                row_copy(orow_ref, j, out_hbm, idx_ref[e * cap + j], s_sem).start()
                return c
            lax.fori_loop(0, cap, body, 0, unroll=FFN_DMA_UNROLL)
            wait_rows(orow_ref, s_sem)
            wait_rows(stage_ref, g_sem)


def _ffn(idx, gate, h2, x1, w_gate, w_up, w_down, cap):
    n = h2.shape[0] // TOK_SUB
    nfc = D_EXPERT // FFN_FC
    grid_spec = pltpu.PrefetchScalarGridSpec(
        num_scalar_prefetch=1,
        grid=(N_EXPERTS, nfc),
        in_specs=[pl.BlockSpec((N_EXPERTS, 1, cap), lambda e, f, idx: (0, 0, 0)),
                  pl.BlockSpec(memory_space=pl.ANY),
                  pl.BlockSpec((None, D_MODEL, FFN_FC), lambda e, f, idx: (e, 0, f)),
                  pl.BlockSpec((None, D_MODEL, FFN_FC), lambda e, f, idx: (e, 0, f)),
                  pl.BlockSpec((None, FFN_FC, D_MODEL), lambda e, f, idx: (e, f, 0)),
                  pl.BlockSpec(memory_space=pl.ANY)],
        out_specs=pl.BlockSpec(memory_space=pl.ANY),
        scratch_shapes=[pltpu.VMEM((cap * TOK_SUB, LANES), F32),
                        pltpu.VMEM((cap, D_MODEL), BF16),
                        pltpu.VMEM((cap, D_MODEL), F32),
                        pltpu.VMEM((cap * TOK_SUB, LANES), F32),
                        pltpu.SemaphoreType.DMA((3,))],
    )
    return pl.pallas_call(
        functools.partial(_ffn_kernel, cap=cap, nfc=nfc),
        grid_spec=grid_spec,
        out_shape=jax.ShapeDtypeStruct((n * TOK_SUB, LANES), F32),
        input_output_aliases={6: 0},
        compiler_params=_params(("arbitrary", "arbitrary")),
        name="ffn",
    )(idx.reshape(-1), gate, h2, w_gate, w_up, w_down, x1)


def _untile_kernel(x_ref, o_ref):
    for s in range(TOK_SUB):
        o_ref[:, s * LANES:(s + 1) * LANES] = x_ref[_tok_rows(s, o_ref.shape[0]), :]


def _untile(xt, tm):
    n = xt.shape[0] // TOK_SUB
    return pl.pallas_call(
        _untile_kernel,
        grid=(n // tm,),
        in_specs=[pl.BlockSpec((tm * TOK_SUB, LANES), lambda i: (i, 0))],
        out_specs=pl.BlockSpec((tm, D_MODEL), lambda i: (i, 0)),
        out_shape=jax.ShapeDtypeStruct((n, D_MODEL), F32),
        compiler_params=_params(("arbitrary",)),
        name="untile",
    )(xt)


def _rope_tables(T):
    half = DIFF_DH // 2
    inv_freq = jnp.power(ROPE_THETA, -jnp.arange(0, DIFF_DH, 2, dtype=F32) / DIFF_DH)
    ang = jnp.arange(T, dtype=F32)[:, None] * inv_freq[None, :]
    cos, sin = jnp.cos(ang), jnp.sin(ang)
    reps = HEAD_W // DIFF_DH
    return (jnp.concatenate([cos, cos] * reps, axis=-1),
            jnp.concatenate([-sin, sin] * reps, axis=-1))


def _layout_weights(w_in, gla_wg_f, gla_bg_f, gla_wg_b, gla_bg_b):
    q0, k0, v0, g0 = 0, 256, 512, 1024
    lr0, dq0 = 1536, 1568
    cols = []
    for h in range(GLA_HEADS):
        cols += [w_in[:, q0 + h * GLA_DK:q0 + (h + 1) * GLA_DK], w_in[:, k0 + h * GLA_DK:k0 + (h + 1) * GLA_DK]]
    cols += [w_in[:, v0:g0], w_in[:, g0:lr0], w_in[:, dq0:], w_in[:, lr0:dq0],
             jnp.zeros((D_MODEL, LANES - 2 * GLA_RANK), w_in.dtype)]
    w_r = jnp.concatenate(cols, axis=1).astype(BF16)
    wg = jnp.zeros((GLA_HEADS, LANES, HEAD_W), F32)
    bg = jnp.zeros((GLA_HEADS, 1, HEAD_W), F32)
    for h in range(GLA_HEADS):
        hs = slice(h * GLA_DK, (h + 1) * GLA_DK)
        wg = wg.at[h, :GLA_RANK, :GLA_DK].set(gla_wg_f[:, hs])
        wg = wg.at[h, GLA_RANK:2 * GLA_RANK, GLA_DK:].set(gla_wg_b[:, hs])
        bg = bg.at[h, 0, :GLA_DK].set(gla_bg_f[hs])
        bg = bg.at[h, 0, GLA_DK:].set(gla_bg_b[hs])
    return w_r, wg.astype(BF16), bg


def _lambda(lq1, lk1, lq2, lk2, lam_init):
    pad = lambda v: jnp.pad(v.astype(F32), (0, LANES - v.shape[0])).reshape(1, LANES)
    return pl.pallas_call(
        functools.partial(_lambda_kernel, lam_init=lam_init),
        out_shape=jax.ShapeDtypeStruct((1, 1), F32),
        name="lam",
    )(pad(lq1), pad(lk1), pad(lq2), pad(lk2))


def _encoder_layer(x, layer, lw, *, tm=512):
    (norm1_g, w_in, gla_wg_f, gla_bg_f, gla_wg_b, gla_bg_b, gla_norm_g, qk_norm_q, qk_norm_k,
     lambda_q1, lambda_k1, lambda_q2, lambda_k2, diff_norm_g, w_out, norm2_g, w_router,
     w_gate, w_up, w_down) = lw
    B, T, D = x.shape
    n = B * T
    cap = max(1, EC_FACTOR * n // N_EXPERTS)
    tm = min(tm, n)
    qb = min(T, DATTN_SCORE_BYTES // (2 * 4 * T))
    x2 = x.reshape(n, D)
    w_r, wg, bg = _layout_weights(w_in, gla_wg_f, gla_bg_f, gla_wg_b, gla_bg_b)
    row = lambda v: v.astype(F32).reshape(1, -1)
    per_map = lambda v: jnp.tile(v.astype(F32), 2 * DIFF_HEADS).reshape(1, -1)

    cos, sin = _rope_tables(T)
    gqk, gv, gog, dq, dk, dv, lr = _proj(x2, row(norm1_g), w_r, cos, sin,
                                         per_map(qk_norm_q), per_map(qk_norm_k), min(tm, T))
    a = _gla(gqk, gv, gog, lr, wg, bg, row(gla_norm_g), B, T)
    lam_init = _lambda_init(layer)
    lam = _lambda(lambda_q1, lambda_k1, lambda_q2, lambda_k2, lam_init)
    d = _dattn(dq, dk, dv, row(diff_norm_g), lam, B, T, qb, lam_init)
    x1, h2, aff_t = _mixout(a, d, x2, w_out.astype(BF16), row(norm2_g), w_router.T.astype(F32), tm)
    idx, gate = _route(aff_t, n, cap)
    y = _untile(_ffn(idx, gate, h2, x1, w_gate, w_up, w_down, cap), tm)
    return y.reshape(B, T, D)


def kernel(x_prompt, x_sample, norm1_g, w_in, gla_wg_f, gla_bg_f, gla_wg_b, gla_bg_b, gla_norm_g, qk_norm_q, qk_norm_k, lambda_q1, lambda_k1, lambda_q2, lambda_k2, diff_norm_g, w_out, norm2_g, w_router, w_gate, w_up, w_down):
    y_prompt, y_sample = x_prompt, x_sample
    for l in range(norm1_g.shape[0]):
        lw = (norm1_g[l], w_in[l], gla_wg_f[l], gla_bg_f[l], gla_wg_b[l], gla_bg_b[l],
              gla_norm_g[l], qk_norm_q[l], qk_norm_k[l], lambda_q1[l], lambda_k1[l],
              lambda_q2[l], lambda_k2[l], diff_norm_g[l], w_out[l], norm2_g[l],
              w_router[l], w_gate[l], w_up[l], w_down[l])
        y_prompt = _encoder_layer(y_prompt, l, lw)
        y_sample = _encoder_layer(y_sample, l, lw)
    return (y_prompt, y_sample)
```

```python
import functools
import math

import jax
import jax.numpy as jnp
from jax import lax
from jax.experimental import pallas as pl
from jax.experimental.pallas import tpu as pltpu

F32 = jnp.float32
BF16 = jnp.bfloat16
I32 = jnp.int32

LANES = 128
MXU_W = 256
VMEM_LIMIT = 56 * 1024 * 1024

D_MODEL = 1024
GLA_HEADS = 4
GLA_DK = 64
GLA_DV = 128
GLA_RANK = 16
GLA_TAU = 16.0
GLA_CHUNK = 64
DIFF_HEADS = 4
DIFF_DH = 64
DIFF_DV = 128
ROPE_THETA = 10000.0
N_EXPERTS = 16
EC_FACTOR = 2
D_EXPERT = 2 * D_MODEL
RMS_EPS = 1e-6
LOG2_E = 1.4426950408889634


def _lambda_init(layer):
    return 0.8 - 0.6 * math.exp(-0.3 * layer)


HEAD_W = 128
TOK_SUB = D_MODEL // LANES


def _tok_rows(s, ntok, first_tok=0):
    return pl.ds(first_tok * TOK_SUB + s, ntok, stride=TOK_SUB)
PROJ_GROUPS = 6
PROJ_W = PROJ_GROUPS * 4 * HEAD_W + LANES


def _dot(a, b):
    return jnp.dot(a, b, preferred_element_type=F32)


def _dot_nt(a, b):
    return lax.dot_general(a, b, (((1,), (1,)), ((), ())), preferred_element_type=F32)


def _dot_tn(a, b):
    return lax.dot_general(a, b, (((0,), (0,)), ((), ())), preferred_element_type=F32)


def _split3(x):
    a = x.astype(BF16)
    r = x - a.astype(F32)
    b = r.astype(BF16)
    c = (r - b.astype(F32)).astype(BF16)
    return a, b, c


def _params(sem):
    return pltpu.CompilerParams(dimension_semantics=sem, vmem_limit_bytes=VMEM_LIMIT)


def _qk_prep(x, gain, cos, sin, same_map):
    x2 = (x * x).astype(BF16)
    width = x.shape[1]
    mw = same_map.shape[0]
    ms = jnp.concatenate([_dot(x2[:, c:c + mw], same_map) for c in range(0, width, mw)], axis=1)
    y = x * lax.rsqrt(ms * (1.0 / DIFF_DH) + RMS_EPS) * gain
    half = DIFF_DH // 2
    first = (lax.broadcasted_iota(I32, (1, width), 1) & half) == 0
    swapped = jnp.where(first, pltpu.roll(y, width - half, 1), pltpu.roll(y, half, 1))
    return y * cos + swapped * sin


def _proj_kernel(x_ref, g_ref, w_ref, cos_ref, sin_ref, qg_ref, kg_ref, map_ref,
                 gqk_ref, gv_ref, gog_ref, dq_ref, dk_ref, dv_ref, lr_ref):
    x = x_ref[...]
    ms = jnp.mean(x * x, axis=-1, keepdims=True)
    h = (x * lax.rsqrt(ms + RMS_EPS) * g_ref[...]).astype(BF16)
    gw = 4 * HEAD_W
    group = lambda i: _dot(h, w_ref[:, i * gw:(i + 1) * gw])
    for i, o in ((0, gqk_ref), (1, gv_ref), (2, gog_ref), (5, dv_ref)):
        o[...] = group(i).astype(o.dtype)
    cos = jnp.concatenate([cos_ref[...]] * DIFF_HEADS, axis=1)
    sin = jnp.concatenate([sin_ref[...]] * DIFF_HEADS, axis=1)
    q = _qk_prep(group(3), qg_ref[...], cos, sin, map_ref[...])
    dq_ref[...] = (q * (DIFF_DH ** -0.5 * LOG2_E)).astype(dq_ref.dtype)
    dk_ref[...] = _qk_prep(group(4), kg_ref[...], cos, sin, map_ref[...]).astype(dk_ref.dtype)
    lr_ref[...] = _dot(h, w_ref[:, PROJ_GROUPS * gw:])


def _proj(x2, g1, w_r, cos, sin, qg, kg, tm):
    n = x2.shape[0]
    T = cos.shape[0]
    gw = 4 * HEAD_W
    row = lambda i: (i, 0)
    const = lambda i: (0, 0)
    pos = lambda i: (i % (T // tm), 0)
    ri = lax.broadcasted_iota(I32, (MXU_W, MXU_W), 0)
    ci = lax.broadcasted_iota(I32, (MXU_W, MXU_W), 1)
    same_map = ((ri // DIFF_DH) == (ci // DIFF_DH)).astype(BF16)
    outs = [jax.ShapeDtypeStruct((n, gw), BF16)] * PROJ_GROUPS + [jax.ShapeDtypeStruct((n, LANES), F32)]
    return pl.pallas_call(
        _proj_kernel,
        grid=(n // tm,),
        in_specs=[pl.BlockSpec((tm, D_MODEL), row),
                  pl.BlockSpec((1, D_MODEL), const),
                  pl.BlockSpec((D_MODEL, PROJ_W), const),
                  pl.BlockSpec((tm, HEAD_W), pos), pl.BlockSpec((tm, HEAD_W), pos),
                  pl.BlockSpec((1, gw), const), pl.BlockSpec((1, gw), const),
                  pl.BlockSpec((MXU_W, MXU_W), const)],
        out_specs=[pl.BlockSpec((tm, gw), row)] * PROJ_GROUPS + [pl.BlockSpec((tm, LANES), row)],
        out_shape=outs,
        compiler_params=_params(("arbitrary",)),
        name="proj",
    )(x2, g1, w_r, cos, sin, qg, kg, same_map)


GLA_BLK = 256


def _gla_kernel(gqk_ref, gv_ref, gog_ref, lr_ref, wg_ref, bg_ref, ng_ref, o_ref,
                qh_ref, u_ref, dec_ref, acc_ref, b_ref, tot_ref, *, T):
    C = GLA_CHUNK
    nblk = T // GLA_BLK
    nch = T // C
    lane = lax.broadcasted_iota(I32, (1, HEAD_W), 1)
    fwd = lane < GLA_DK
    ri = lax.broadcasted_iota(I32, (GLA_BLK, GLA_BLK), 0)
    ci = lax.broadcasted_iota(I32, (GLA_BLK, GLA_BLK), 1)
    same = (ri // C) == (ci // C)
    prefix_total = jnp.concatenate([jnp.where(same & (ci <= ri), 1.0, 0.0),
                                    jnp.where(same, 1.0, 0.0)], axis=0).astype(BF16)
    m_f = same & (ci <= ri)
    m_b = same & (ci > ri)
    wg = wg_ref[...]
    bg = bg_ref[...]
    cpb = GLA_BLK // C
    chunk_of_row = lax.broadcasted_iota(I32, (GLA_BLK, 1), 0) // C

    def gates(i, carry):
        rows = pl.ds(pl.multiple_of(i * GLA_BLK, GLA_BLK), GLA_BLK)
        z = _dot(lr_ref[rows, :].astype(BF16), wg) + bg
        la = (jnp.minimum(z, 0.0) - jnp.log(1.0 + jnp.exp(-jnp.abs(z)))) * (1.0 / GLA_TAU)
        hi = la.astype(BF16)
        lo = (la - hi.astype(F32)).astype(BF16)
        pt = _dot(prefix_total, jnp.concatenate([hi, lo], axis=1))
        pt = pt[:, :HEAD_W] + pt[:, HEAD_W:]
        pre, tot = pt[:GLA_BLK, :], pt[GLA_BLK:, :]
        b_ref[rows, :] = jnp.where(fwd, pre, tot - pre + la)
        tot_ref[rows, :] = tot
        return carry

    lax.fori_loop(0, nblk, gates, 0, unroll=8)

    def block(i, carry):
        rows = pl.ds(pl.multiple_of(i * GLA_BLK, GLA_BLK), GLA_BLK)
        b = b_ref[rows, :]
        tot = tot_ref[rows, :]
        mid = 0.5 * tot
        blk = gqk_ref[rows, :].astype(F32)
        rot = pltpu.roll(blk, GLA_DK, 1)
        qq = jnp.where(fwd, blk, rot) * (GLA_DK ** -0.5)
        kk = jnp.where(fwd, rot, blk)
        qt = qq * jnp.exp(b - mid)
        kt = (kk * jnp.exp(mid - b)).astype(BF16)
        kh = (kk * jnp.exp(tot - b)).astype(BF16)
        qh_ref[rows, :] = (qq * jnp.exp(b)).astype(BF16)
        qfb = jnp.concatenate([jnp.where(fwd, qt, 0.0), jnp.where(fwd, 0.0, qt)], axis=0).astype(BF16)
        sfb = _dot_nt(qfb, kt)
        s = jnp.where(m_f, sfb[:GLA_BLK, :], jnp.where(m_b, sfb[GLA_BLK:, :], 0.0)).astype(BF16)
        v = gv_ref[rows, :]
        acc_ref[rows, :] = _dot(s, v)
        zero = jnp.zeros_like(kh)
        kx = jnp.concatenate([jnp.where(chunk_of_row == c, kh, zero) for c in range(cpb)], axis=1)
        ux = _dot_tn(v, kx)
        dec = jnp.exp(tot)
        for c in range(cpb):
            u_ref[i * cpb + c] = ux[:, c * HEAD_W:(c + 1) * HEAD_W]
            dec_ref[i * cpb + c] = jnp.broadcast_to(dec[c * C:c * C + 1, :], (8, HEAD_W))
        return carry

    lax.fori_loop(0, nblk, block, 0, unroll=8)

    def step(t, st):
        nf = t
        nb = nch - 1 - t
        stb = st.astype(BF16)
        rf = pl.ds(pl.multiple_of(nf * C, C), C)
        rb = pl.ds(pl.multiple_of(nb * C, C), C)
        qf = qh_ref[rf, :]
        qb = qh_ref[rb, :]
        zero = jnp.zeros_like(qf)
        o = _dot_nt(jnp.concatenate([jnp.where(fwd, qf, zero), jnp.where(fwd, zero, qb)], axis=0), stb)
        acc_ref[rf, :] += o[:C, :]
        acc_ref[rb, :] += o[C:, :]
        d = jnp.where(fwd, dec_ref[nf][0:1, :], dec_ref[nb][0:1, :])
        u = jnp.where(fwd, u_ref[nf], u_ref[nb])
        return st * d + u

    lax.fori_loop(0, nch, step, jnp.zeros((GLA_DV, HEAD_W), F32), unroll=8)

    ng = ng_ref[...]
    eb = min(T, 512)

    def epi(i, carry):
        rows = pl.ds(pl.multiple_of(i * eb, eb), eb)
        o = acc_ref[rows, :]
        y = o * lax.rsqrt(jnp.mean(o * o, axis=-1, keepdims=True) + RMS_EPS) * ng
        og = gog_ref[rows, :].astype(F32)
        o_ref[rows, :] = (y * (og * jax.nn.sigmoid(og))).astype(o_ref.dtype)
        return carry

    lax.fori_loop(0, T // eb, epi, 0)


def _gla(gqk, gv, gog, lr, wg, bg, ng, B, T):
    n = B * T
    head = pl.BlockSpec((T, HEAD_W), lambda b, h: (b, h))
    return pl.pallas_call(
        functools.partial(_gla_kernel, T=T),
        grid=(B, GLA_HEADS),
        in_specs=[head, head, head,
                  pl.BlockSpec((T, LANES), lambda b, h: (b, 0)),
                  pl.BlockSpec((None, LANES, HEAD_W), lambda b, h: (h, 0, 0)),
                  pl.BlockSpec((None, 1, HEAD_W), lambda b, h: (h, 0, 0)),
                  pl.BlockSpec((1, HEAD_W), lambda b, h: (0, 0))],
        out_specs=head,
        out_shape=jax.ShapeDtypeStruct((n, GLA_HEADS * HEAD_W), BF16),
        scratch_shapes=[pltpu.VMEM((T, HEAD_W), BF16),
                        pltpu.VMEM((T // GLA_CHUNK, GLA_DV, HEAD_W), F32),
                        pltpu.VMEM((T // GLA_CHUNK, 8, HEAD_W), F32),
                        pltpu.VMEM((T, GLA_DV), F32),
                        pltpu.VMEM((T, HEAD_W), F32),
                        pltpu.VMEM((T, HEAD_W), F32)],
        compiler_params=_params(("arbitrary", "arbitrary")),
        name="gla",
    )(gqk, gv, gog, lr, wg, bg, ng)


DATTN_KSPLIT = 2
DATTN_SCORE_BYTES = 32 * 1024 * 1024


def _dattn_kernel(dq_ref, dk_ref, dv_ref, ng_ref, lam_ref, o_ref,
                  s_ref, mp_ref, lp_ref, acc_ref, *, T, QB, TK, lam_init):
    lane = lax.broadcasted_iota(I32, (1, HEAD_W), 1)
    map0 = lane < DIFF_DH
    nk = T // TK
    q = dq_ref[...]
    zero = jnp.zeros_like(q)
    qs = jnp.concatenate([jnp.where(map0, q, zero), jnp.where(map0, zero, q)], axis=0)
    ntile = TK // LANES

    def lane_tiles(x):
        return [x[:, j * LANES:(j + 1) * LANES] for j in range(ntile)]

    mp_ref[...] = jnp.full(mp_ref.shape, -jnp.inf, F32)
    lp_ref[...] = jnp.zeros(lp_ref.shape, F32)
    acc_ref[...] = jnp.zeros(acc_ref.shape, F32)

    def scores(i, carry):
        cols = pl.ds(pl.multiple_of(i * TK, TK), TK)
        s = _dot_nt(qs, dk_ref[cols, :])
        s_ref[:, cols] = s
        mp_ref[...] = jnp.maximum(mp_ref[...], functools.reduce(jnp.maximum, lane_tiles(s)))
        return carry

    lax.fori_loop(0, nk, scores, 0)
    m = jnp.max(mp_ref[...], axis=-1, keepdims=True)

    def accum(i, carry):
        cols = pl.ds(pl.multiple_of(i * TK, TK), TK)
        p = jnp.exp2(s_ref[:, cols] - m)
        lp_ref[...] += functools.reduce(jnp.add, lane_tiles(p))
        acc_ref[...] += _dot(p.astype(BF16), dv_ref[cols, :])
        return carry

    lax.fori_loop(0, nk, accum, 0)
    on = acc_ref[...] / jnp.sum(lp_ref[...], axis=-1, keepdims=True)
    lam = lam_ref[...]
    o = on[:QB, :] - lam * on[QB:, :]
    y = o * lax.rsqrt(jnp.mean(o * o, axis=-1, keepdims=True) + RMS_EPS) * ng_ref[...]
    o_ref[...] = (y * (1.0 - lam_init)).astype(o_ref.dtype)


def _lambda_kernel(q1_ref, k1_ref, q2_ref, k2_ref, o_ref, *, lam_init):
    a = jnp.exp(jnp.sum(q1_ref[...] * k1_ref[...], axis=-1, keepdims=True))
    b = jnp.exp(jnp.sum(q2_ref[...] * k2_ref[...], axis=-1, keepdims=True))
    o_ref[...] = a - b + lam_init


def _dattn(dq, dk, dv, ng, lam, B, T, QB, lam_init):
    n = B * T
    TK = T // DATTN_KSPLIT
    nq = T // QB
    full = lambda shape: pl.BlockSpec(shape, lambda b, h, q: (0, 0))
    kv = pl.BlockSpec((T, HEAD_W), lambda b, h, q: (b, h))
    qo = pl.BlockSpec((QB, HEAD_W), lambda b, h, q: (b * nq + q, h))
    return pl.pallas_call(
        functools.partial(_dattn_kernel, T=T, QB=QB, TK=TK, lam_init=lam_init),
        grid=(B, DIFF_HEADS, nq),
        in_specs=[qo, kv, kv, full((1, HEAD_W)), full((1, 1))],
        out_specs=qo,
        out_shape=jax.ShapeDtypeStruct((n, DIFF_HEADS * HEAD_W), BF16),
        scratch_shapes=[pltpu.VMEM((2 * QB, T), F32),
                        pltpu.VMEM((2 * QB, LANES), F32),
                        pltpu.VMEM((2 * QB, LANES), F32),
                        pltpu.VMEM((2 * QB, DIFF_DV), F32)],
        compiler_params=_params(("arbitrary", "arbitrary", "arbitrary")),
        name="dattn",
    )(dq, dk, dv, ng, lam)


def _mixout_kernel(a_ref, d_ref, x_ref, w_ref, g_ref, wr_ref, x1_ref, h2_ref, aff_ref):
    half = GLA_HEADS * HEAD_W
    y = _dot(a_ref[...], w_ref[:half, :]) + _dot(d_ref[...], w_ref[half:, :])
    x1 = x_ref[...] + y
    h2 = x1 * lax.rsqrt(jnp.mean(x1 * x1, axis=-1, keepdims=True) + RMS_EPS) * g_ref[...]
    for s in range(TOK_SUB):
        cols = slice(s * LANES, (s + 1) * LANES)
        x1_ref[_tok_rows(s, x1.shape[0]), :] = x1[:, cols]
        h2_ref[_tok_rows(s, x1.shape[0]), :] = h2[:, cols]
    logits = jnp.zeros((N_EXPERTS, h2.shape[0]), F32)
    hp = _split3(h2)
    wp = _split3(wr_ref[...])
    for i, j in ((0, 0), (0, 1), (1, 0)):
        logits = logits + _dot_nt(wp[i], hp[j])
    m = jnp.max(logits, axis=0, keepdims=True)
    e = jnp.exp(logits - m)
    aff_ref[...] = e / jnp.sum(e, axis=0, keepdims=True)


def _mixout(a, d, x2, w_out, g2, wr_t, tm):
    n = x2.shape[0]
    row = lambda i: (i, 0)
    half = GLA_HEADS * HEAD_W
    return pl.pallas_call(
        _mixout_kernel,
        grid=(n // tm,),
        in_specs=[pl.BlockSpec((tm, half), row), pl.BlockSpec((tm, half), row),
                  pl.BlockSpec((tm, D_MODEL), row),
                  pl.BlockSpec((D_MODEL, D_MODEL), lambda i: (0, 0)),
                  pl.BlockSpec((1, D_MODEL), lambda i: (0, 0)),
                  pl.BlockSpec((N_EXPERTS, D_MODEL), lambda i: (0, 0))],
        out_specs=[pl.BlockSpec((tm * TOK_SUB, LANES), row), pl.BlockSpec((tm * TOK_SUB, LANES), row),
                   pl.BlockSpec((N_EXPERTS, tm), lambda i: (0, i))],
        out_shape=[jax.ShapeDtypeStruct((n * TOK_SUB, LANES), F32), jax.ShapeDtypeStruct((n * TOK_SUB, LANES), F32),
                   jax.ShapeDtypeStruct((N_EXPERTS, n), F32)],
        compiler_params=_params(("arbitrary",)),
        name="mixout",
    )(a, d, x2, w_out, g2, wr_t)


ROUTE_BISECTIONS = 40


def _route_kernel(aff_ref, idx_ref, gate_ref, *, n, cap):
    R = n // LANES
    E = N_EXPERTS
    aff = aff_ref[...]

    def count(mask):
        c = jnp.sum(jnp.where(mask, 1.0, 0.0), axis=1, keepdims=True)
        return jnp.sum(c, axis=2, keepdims=True)

    capf = float(cap)
    k = jnp.zeros((E, 1, 1), F32)
    for bit in (64.0, 32.0, 16.0, 8.0, 4.0, 2.0, 1.0):
        cand = k + bit
        k = jnp.where(count(aff >= jnp.exp2(1.0 - cand)) < capf, cand, k)
    hi = jnp.exp2(1.0 - k)
    lo = jnp.where(k >= 127.0, 0.0, jnp.exp2(-jnp.minimum(k, 126.0)))

    def bisect(i, c):
        lo, hi = c
        mid = 0.5 * (lo + hi)
        ok = count(aff >= mid) >= capf
        return jnp.where(ok, mid, lo), jnp.where(ok, hi, mid)

    lo, hi = lax.fori_loop(0, ROUTE_BISECTIONS, bisect, (lo, hi))
    gt = aff >= hi
    eq = (aff >= lo) & (aff < hi)
    need = capf - count(gt)
    tok = (lax.broadcasted_iota(I32, (1, R, LANES), 1) * LANES
           + lax.broadcasted_iota(I32, (1, R, LANES), 2))
    nbits = max(1, (n - 1).bit_length())

    def tie_step(i, m0):
        cand = m0 | jnp.left_shift(jnp.int32(1), nbits - 1 - i)
        return jnp.where(count(eq & (tok < cand)) < need, cand, m0)

    m0 = lax.fori_loop(0, nbits, tie_step, jnp.zeros((E, 1, 1), I32))
    sel = gt | (eq & (tok <= m0))

    ci = lax.broadcasted_iota(I32, (LANES, LANES), 0)
    cj = lax.broadcasted_iota(I32, (LANES, LANES), 1)
    upper = jnp.where(ci <= cj, 1.0, 0.0).astype(BF16)
    ri = lax.broadcasted_iota(I32, (R, R), 0)
    rj = lax.broadcasted_iota(I32, (R, R), 1)
    lower = jnp.where(rj < ri, 1.0, 0.0).astype(BF16)
    jrow = lax.broadcasted_iota(I32, (1, cap), 1).astype(F32)
    rcol = lax.broadcasted_iota(I32, (R, 1), 0).astype(F32)
    ccol = lax.broadcasted_iota(I32, (LANES, 1), 0).astype(F32)

    for e in range(E):
        sel_e = sel[e]
        w = _dot(jnp.where(sel_e, 1.0, 0.0).astype(BF16), upper)
        rt = w[:, LANES - 1:LANES]
        rp = _dot(lower, jnp.broadcast_to(rt, (R, LANES)).astype(BF16))[:, 0:1]
        rows_t = jnp.where((rp <= jrow) & (jrow < rp + rt), 1.0, 0.0)
        k1 = jrow - jnp.sum(rows_t * rp, axis=0, keepdims=True) + 1.0
        rows_b = rows_t.astype(BF16)
        wrow = _dot(jnp.where(sel_e, w, 0.0).T.astype(BF16), rows_b)
        cols_t = jnp.where(wrow == k1, 1.0, 0.0)
        r_of = jnp.sum(rows_t * rcol, axis=0, keepdims=True)
        c_of = jnp.sum(cols_t * ccol, axis=0, keepdims=True)
        idx_ref[e] = (r_of * float(LANES) + c_of).astype(I32)
        a3 = _split3(aff[e].T)
        arow = _dot(a3[0], rows_b) + _dot(a3[1], rows_b) + _dot(a3[2], rows_b)
        gate_ref[e] = jnp.sum(cols_t * arow, axis=0, keepdims=True)


def _route(aff_t, n, cap):
    R = n // LANES
    return pl.pallas_call(
        functools.partial(_route_kernel, n=n, cap=cap),
        out_shape=[jax.ShapeDtypeStruct((N_EXPERTS, 1, cap), I32),
                   jax.ShapeDtypeStruct((N_EXPERTS, 1, cap), F32)],
        compiler_params=pltpu.CompilerParams(vmem_limit_bytes=VMEM_LIMIT),
        name="route",
    )(aff_t.reshape(N_EXPERTS, R, LANES))


FFN_FC = 512
FFN_TT = 1024
FFN_DMA_UNROLL = 16


def _ffn_kernel(idx_ref, gate_ref, h2_hbm, wg_ref, wu_ref, wd_ref, x1_hbm, out_hbm,
                stage_ref, xb_ref, acc_ref, orow_ref, sems, *, cap, nfc):
    del x1_hbm
    e = pl.program_id(0)
    fc = pl.program_id(1)
    ne = pl.num_programs(0)
    g_sem, o_sem, s_sem = sems.at[0], sems.at[1], sems.at[2]
    tt = min(FFN_TT, cap)

    def row_copy(src, s_tok, dst, d_tok, sem):
        s_rows = pl.ds(pl.multiple_of(s_tok * TOK_SUB, TOK_SUB), TOK_SUB)
        d_rows = pl.ds(pl.multiple_of(d_tok * TOK_SUB, TOK_SUB), TOK_SUB)
        return pltpu.make_async_copy(src.at[s_rows], dst.at[d_rows], sem)

    def issue_gather(ee, src, dst, sem):
        def body(j, c):
            row_copy(src, idx_ref[ee * cap + j], dst, j, sem).start()
            return c
        lax.fori_loop(0, cap, body, 0, unroll=FFN_DMA_UNROLL)

    def wait_rows(buf, sem):
        pltpu.make_async_copy(buf, buf, sem).wait()

    gather_step = nfc - 2
    ntile = cap // tt
    ntrip = nfc * ntile
    quota = 3 * cap // ntrip
    plan = [[] for _ in range(ntrip)]

    def room(trip):
        return quota - sum(c for _, _, c in plan[trip])

    def place(stream, trip, total):
        done = 0
        while done < total:
            take = min(room(trip), total - done)
            if take:
                plan[trip].append((stream, done, take))
                done += take
            trip += done < total
        return trip

    assert place("S", 0, cap) < gather_step * ntile, "the scatter must be issued a trip before its wait"
    place("O", gather_step * ntile, cap)
    x_done = 0
    for trip in range(ntrip):
        take = min(room(trip), cap - x_done)
        if take:
            plan[trip].append(("X", x_done, take))
            x_done += take
    assert x_done == cap
    e_next = jnp.minimum(e + 1, ne - 1)

    @pl.when(fc == 0)
    def _():
        @pl.when(e == 0)
        def _():
            issue_gather(0, h2_hbm, stage_ref, g_sem)
        wait_rows(stage_ref, g_sem)
        for s in range(TOK_SUB):
            xb_ref[:, s * LANES:(s + 1) * LANES] = stage_ref[_tok_rows(s, cap), :].astype(BF16)

    @pl.when((fc == gather_step) & (e > 0))
    def _():
        wait_rows(orow_ref, s_sem)

    def finish_rows(ee, tok0, ntok):
        for b in range(ntok // LANES):
            first = tok0 + b * LANES
            rows = pl.ds(first if isinstance(first, int) else pl.multiple_of(first, LANES), LANES)
            g = gate_ref.at[ee][:, rows]
            gcol = jnp.broadcast_to(g, (LANES, LANES)).T
            for s in range(TOK_SUB):
                trows = pl.ds(first * TOK_SUB + s, LANES, stride=TOK_SUB)
                orow_ref[trows, :] += acc_ref[rows, s * LANES:(s + 1) * LANES] * gcol

    def issue(stream, j):
        if stream == "S":
            row_copy(orow_ref, j, out_hbm, idx_ref[(e - 1) * cap + j], s_sem).start()
        elif stream == "O":
            row_copy(out_hbm, idx_ref[e * cap + j], orow_ref, j, o_sem).start()
        else:
            row_copy(h2_hbm, idx_ref[e_next * cap + j], stage_ref, j, g_sem).start()

    def step_body(k, with_scatter):
        for t in range(ntile):
            for stream, start, count in plan[k * ntile + t]:
                if stream != "S" or with_scatter:
                    for j in range(start, start + count):
                        issue(stream, j)
            rows = slice(t * tt, (t + 1) * tt)
            x = xb_ref[rows, :]
            a = _dot(x, wg_ref[...].astype(BF16))
            u = _dot(x, wu_ref[...].astype(BF16))
            hm = (a * jax.nn.sigmoid(a) * u).astype(BF16)
            y = _dot(hm, wd_ref[...].astype(BF16))
            if k == 0:
                acc_ref[rows, :] = y
            else:
                acc_ref[rows, :] += y

    for k in range(nfc):
        if any(s == "S" for trip in plan[k * ntile:(k + 1) * ntile] for s, _, _ in trip):
            pl.when((fc == k) & (e == 0))(functools.partial(step_body, k, False))
            pl.when((fc == k) & (e > 0))(functools.partial(step_body, k, True))
        else:
            pl.when(fc == k)(functools.partial(step_body, k, True))

    @pl.when(fc == nfc - 1)
    def _():
        wait_rows(orow_ref, o_sem)
        finish_rows(e, 0, cap)

        @pl.when(e == ne - 1)
        def _():
            def body(j, c):
                row_copy(orow_ref, j, out_hbm, idx_ref[e * cap + j], s_sem).start()
                return c
            lax.fori_loop(0, cap, body, 0, unroll=FFN_DMA_UNROLL)
            wait_rows(orow_ref, s_sem)
            wait_rows(stage_ref, g_sem)


def _ffn(idx, gate, h2, x1, w_gate, w_up, w_down, cap):
    n = h2.shape[0] // TOK_SUB
    nfc = D_EXPERT // FFN_FC
    grid_spec = pltpu.PrefetchScalarGridSpec(
        num_scalar_prefetch=1,
        grid=(N_EXPERTS, nfc),
        in_specs=[pl.BlockSpec((N_EXPERTS, 1, cap), lambda e, f, idx: (0, 0, 0)),
                  pl.BlockSpec(memory_space=pl.ANY),
                  pl.BlockSpec((None, D_MODEL, FFN_FC), lambda e, f, idx: (e, 0, f)),
                  pl.BlockSpec((None, D_MODEL, FFN_FC), lambda e, f, idx: (e, 0, f)),
                  pl.BlockSpec((None, FFN_FC, D_MODEL), lambda e, f, idx: (e, f, 0)),
                  pl.BlockSpec(memory_space=pl.ANY)],
        out_specs=pl.BlockSpec(memory_space=pl.ANY),
        scratch_shapes=[pltpu.VMEM((cap * TOK_SUB, LANES), F32),
                        pltpu.VMEM((cap, D_MODEL), BF16),
                        pltpu.VMEM((cap, D_MODEL), F32),
                        pltpu.VMEM((cap * TOK_SUB, LANES), F32),
                        pltpu.SemaphoreType.DMA((3,))],
    )
    return pl.pallas_call(
        functools.partial(_ffn_kernel, cap=cap, nfc=nfc),
        grid_spec=grid_spec,
        out_shape=jax.ShapeDtypeStruct((n * TOK_SUB, LANES), F32),
        input_output_aliases={6: 0},
        compiler_params=_params(("arbitrary", "arbitrary")),
        name="ffn",
    )(idx.reshape(-1), gate, h2, w_gate, w_up, w_down, x1)


def _untile_kernel(x_ref, o_ref):
    for s in range(TOK_SUB):
        o_ref[:, s * LANES:(s + 1) * LANES] = x_ref[_tok_rows(s, o_ref.shape[0]), :]


def _untile(xt, tm):
    n = xt.shape[0] // TOK_SUB
    return pl.pallas_call(
        _untile_kernel,
        grid=(n // tm,),
        in_specs=[pl.BlockSpec((tm * TOK_SUB, LANES), lambda i: (i, 0))],
        out_specs=pl.BlockSpec((tm, D_MODEL), lambda i: (i, 0)),
        out_shape=jax.ShapeDtypeStruct((n, D_MODEL), F32),
        compiler_params=_params(("arbitrary",)),
        name="untile",
    )(xt)


def _rope_tables(T):
    half = DIFF_DH // 2
    inv_freq = jnp.power(ROPE_THETA, -jnp.arange(0, DIFF_DH, 2, dtype=F32) / DIFF_DH)
    ang = jnp.arange(T, dtype=F32)[:, None] * inv_freq[None, :]
    cos, sin = jnp.cos(ang), jnp.sin(ang)
    reps = HEAD_W // DIFF_DH
    return (jnp.concatenate([cos, cos] * reps, axis=-1),
            jnp.concatenate([-sin, sin] * reps, axis=-1))


def _layout_weights(w_in, gla_wg_f, gla_bg_f, gla_wg_b, gla_bg_b):
    q0, k0, v0, g0 = 0, 256, 512, 1024
    lr0, dq0 = 1536, 1568
    cols = []
    for h in range(GLA_HEADS):
        cols += [w_in[:, q0 + h * GLA_DK:q0 + (h + 1) * GLA_DK], w_in[:, k0 + h * GLA_DK:k0 + (h + 1) * GLA_DK]]
    cols += [w_in[:, v0:g0], w_in[:, g0:lr0], w_in[:, dq0:], w_in[:, lr0:dq0],
             jnp.zeros((D_MODEL, LANES - 2 * GLA_RANK), w_in.dtype)]
    w_r = jnp.concatenate(cols, axis=1).astype(BF16)
    wg = jnp.zeros((GLA_HEADS, LANES, HEAD_W), F32)
    bg = jnp.zeros((GLA_HEADS, 1, HEAD_W), F32)
    for h in range(GLA_HEADS):
        hs = slice(h * GLA_DK, (h + 1) * GLA_DK)
        wg = wg.at[h, :GLA_RANK, :GLA_DK].set(gla_wg_f[:, hs])
        wg = wg.at[h, GLA_RANK:2 * GLA_RANK, GLA_DK:].set(gla_wg_b[:, hs])
        bg = bg.at[h, 0, :GLA_DK].set(gla_bg_f[hs])
        bg = bg.at[h, 0, GLA_DK:].set(gla_bg_b[hs])
    return w_r, wg.astype(BF16), bg


def _lambda(lq1, lk1, lq2, lk2, lam_init):
    pad = lambda v: jnp.pad(v.astype(F32), (0, LANES - v.shape[0])).reshape(1, LANES)
    return pl.pallas_call(
        functools.partial(_lambda_kernel, lam_init=lam_init),
        out_shape=jax.ShapeDtypeStruct((1, 1), F32),
        name="lam",
    )(pad(lq1), pad(lk1), pad(lq2), pad(lk2))


def _encoder_layer(x, layer, lw, *, tm=512):
    (norm1_g, w_in, gla_wg_f, gla_bg_f, gla_wg_b, gla_bg_b, gla_norm_g, qk_norm_q, qk_norm_k,
     lambda_q1, lambda_k1, lambda_q2, lambda_k2, diff_norm_g, w_out, norm2_g, w_router,
     w_gate, w_up, w_down) = lw
    B, T, D = x.shape
    n = B * T
    cap = max(1, EC_FACTOR * n // N_EXPERTS)
    tm = min(tm, n)
    qb = min(T, DATTN_SCORE_BYTES // (2 * 4 * T))
    x2 = x.reshape(n, D)
    w_r, wg, bg = _layout_weights(w_in, gla_wg_f, gla_bg_f, gla_wg_b, gla_bg_b)
    row = lambda v: v.astype(F32).reshape(1, -1)
    per_map = lambda v: jnp.tile(v.astype(F32), 2 * DIFF_HEADS).reshape(1, -1)

    cos, sin = _rope_tables(T)
    gqk, gv, gog, dq, dk, dv, lr = _proj(x2, row(norm1_g), w_r, cos, sin,
                                         per_map(qk_norm_q), per_map(qk_norm_k), min(tm, T))
    a = _gla(gqk, gv, gog, lr, wg, bg, row(gla_norm_g), B, T)
    lam_init = _lambda_init(layer)
    lam = _lambda(lambda_q1, lambda_k1, lambda_q2, lambda_k2, lam_init)
    d = _dattn(dq, dk, dv, row(diff_norm_g), lam, B, T, qb, lam_init)
    x1, h2, aff_t = _mixout(a, d, x2, w_out.astype(BF16), row(norm2_g), w_router.T.astype(F32), tm)
    idx, gate = _route(aff_t, n, cap)
    y = _untile(_ffn(idx, gate, h2, x1, w_gate, w_up, w_down, cap), min(2 * tm, n))
    return y.reshape(B, T, D)


def kernel(x_prompt, x_sample, norm1_g, w_in, gla_wg_f, gla_bg_f, gla_wg_b, gla_bg_b, gla_norm_g, qk_norm_q, qk_norm_k, lambda_q1, lambda_k1, lambda_q2, lambda_k2, diff_norm_g, w_out, norm2_g, w_router, w_gate, w_up, w_down):
    y_prompt, y_sample = x_prompt, x_sample
    for l in range(norm1_g.shape[0]):
        lw = (norm1_g[l], w_in[l], gla_wg_f[l], gla_bg_f[l], gla_wg_b[l], gla_bg_b[l],
              gla_norm_g[l], qk_norm_q[l], qk_norm_k[l], lambda_q1[l], lambda_k1[l],
              lambda_q2[l], lambda_k2[l], diff_norm_g[l], w_out[l], norm2_g[l],
              w_router[l], w_gate[l], w_up[l], w_down[l])
        y_prompt = _encoder_layer(y_prompt, l, lw)
        y_sample = _encoder_layer(y_sample, l, lw)
    return (y_prompt, y_sample)
```

```python
import functools
import math

import jax
import jax.numpy as jnp
from jax import lax
from jax.experimental import pallas as pl
from jax.experimental.pallas import tpu as pltpu

F32 = jnp.float32
BF16 = jnp.bfloat16
I32 = jnp.int32

LANES = 128
MXU_W = 256
VMEM_LIMIT = 56 * 1024 * 1024

D_MODEL = 1024
GLA_HEADS = 4
GLA_DK = 64
GLA_DV = 128
GLA_RANK = 16
GLA_TAU = 16.0
GLA_CHUNK = 64
DIFF_HEADS = 4
DIFF_DH = 64
DIFF_DV = 128
ROPE_THETA = 10000.0
N_EXPERTS = 16
EC_FACTOR = 2
D_EXPERT = 2 * D_MODEL
RMS_EPS = 1e-6
LOG2_E = 1.4426950408889634


def _lambda_init(layer):
    return 0.8 - 0.6 * math.exp(-0.3 * layer)


HEAD_W = 128
TOK_SUB = D_MODEL // LANES


def _tok_rows(s, ntok, first_tok=0):
    return pl.ds(first_tok * TOK_SUB + s, ntok, stride=TOK_SUB)
PROJ_GROUPS = 6
PROJ_W = PROJ_GROUPS * 4 * HEAD_W + LANES


def _dot(a, b):
    return jnp.dot(a, b, preferred_element_type=F32)


def _dot_nt(a, b):
    return lax.dot_general(a, b, (((1,), (1,)), ((), ())), preferred_element_type=F32)


def _dot_tn(a, b):
    return lax.dot_general(a, b, (((0,), (0,)), ((), ())), preferred_element_type=F32)


def _split3(x):
    a = x.astype(BF16)
    r = x - a.astype(F32)
    b = r.astype(BF16)
    c = (r - b.astype(F32)).astype(BF16)
    return a, b, c


def _params(sem):
    return pltpu.CompilerParams(dimension_semantics=sem, vmem_limit_bytes=VMEM_LIMIT)


def _qk_prep(x, gain, cos, sin, same_map):
    x2 = (x * x).astype(BF16)
    width = x.shape[1]
    mw = same_map.shape[0]
    ms = jnp.concatenate([_dot(x2[:, c:c + mw], same_map) for c in range(0, width, mw)], axis=1)
    y = x * lax.rsqrt(ms * (1.0 / DIFF_DH) + RMS_EPS) * gain
    half = DIFF_DH // 2
    first = (lax.broadcasted_iota(I32, (1, width), 1) & half) == 0
    swapped = jnp.where(first, pltpu.roll(y, width - half, 1), pltpu.roll(y, half, 1))
    return y * cos + swapped * sin


def _proj_kernel(x_ref, g_ref, w_ref, cos_ref, sin_ref, qg_ref, kg_ref, map_ref,
                 gqk_ref, gv_ref, gog_ref, dq_ref, dk_ref, dv_ref, lr_ref):
    x = x_ref[...]
    ms = jnp.mean(x * x, axis=-1, keepdims=True)
    h = (x * lax.rsqrt(ms + RMS_EPS) * g_ref[...]).astype(BF16)
    gw = 4 * HEAD_W
    group = lambda i: _dot(h, w_ref[:, i * gw:(i + 1) * gw])
    for i, o in ((0, gqk_ref), (1, gv_ref), (2, gog_ref), (5, dv_ref)):
        o[...] = group(i).astype(o.dtype)
    cos = jnp.concatenate([cos_ref[...]] * DIFF_HEADS, axis=1)
    sin = jnp.concatenate([sin_ref[...]] * DIFF_HEADS, axis=1)
    q = _qk_prep(group(3), qg_ref[...], cos, sin, map_ref[...])
    dq_ref[...] = (q * (DIFF_DH ** -0.5 * LOG2_E)).astype(dq_ref.dtype)
    dk_ref[...] = _qk_prep(group(4), kg_ref[...], cos, sin, map_ref[...]).astype(dk_ref.dtype)
    lr_ref[...] = _dot(h, w_ref[:, PROJ_GROUPS * gw:])


def _proj(x2, g1, w_r, cos, sin, qg, kg, tm):
    n = x2.shape[0]
    T = cos.shape[0]
    gw = 4 * HEAD_W
    row = lambda i: (i, 0)
    const = lambda i: (0, 0)
    pos = lambda i: (i % (T // tm), 0)
    ri = lax.broadcasted_iota(I32, (MXU_W, MXU_W), 0)
    ci = lax.broadcasted_iota(I32, (MXU_W, MXU_W), 1)
    same_map = ((ri // DIFF_DH) == (ci // DIFF_DH)).astype(BF16)
    outs = [jax.ShapeDtypeStruct((n, gw), BF16)] * PROJ_GROUPS + [jax.ShapeDtypeStruct((n, LANES), F32)]
    return pl.pallas_call(
        _proj_kernel,
        grid=(n // tm,),
        in_specs=[pl.BlockSpec((tm, D_MODEL), row),
                  pl.BlockSpec((1, D_MODEL), const),
                  pl.BlockSpec((D_MODEL, PROJ_W), const),
                  pl.BlockSpec((tm, HEAD_W), pos), pl.BlockSpec((tm, HEAD_W), pos),
                  pl.BlockSpec((1, gw), const), pl.BlockSpec((1, gw), const),
                  pl.BlockSpec((MXU_W, MXU_W), const)],
        out_specs=[pl.BlockSpec((tm, gw), row)] * PROJ_GROUPS + [pl.BlockSpec((tm, LANES), row)],
        out_shape=outs,
        compiler_params=_params(("arbitrary",)),
        name="proj",
    )(x2, g1, w_r, cos, sin, qg, kg, same_map)


GLA_BLK = 256


def _gla_kernel(gqk_ref, gv_ref, gog_ref, lr_ref, wg_ref, bg_ref, ng_ref, o_ref,
                qh_ref, u_ref, dec_ref, acc_ref, b_ref, tot_ref, *, T):
    C = GLA_CHUNK
    nblk = T // GLA_BLK
    nch = T // C
    lane = lax.broadcasted_iota(I32, (1, HEAD_W), 1)
    fwd = lane < GLA_DK
    ri = lax.broadcasted_iota(I32, (GLA_BLK, GLA_BLK), 0)
    ci = lax.broadcasted_iota(I32, (GLA_BLK, GLA_BLK), 1)
    same = (ri // C) == (ci // C)
    prefix_total = jnp.concatenate([jnp.where(same & (ci <= ri), 1.0, 0.0),
                                    jnp.where(same, 1.0, 0.0)], axis=0).astype(BF16)
    m_f = same & (ci <= ri)
    m_b = same & (ci > ri)
    wg = wg_ref[...]
    bg = bg_ref[...]
    cpb = GLA_BLK // C
    chunk_of_row = lax.broadcasted_iota(I32, (GLA_BLK, 1), 0) // C

    def gates(i, carry):
        rows = pl.ds(pl.multiple_of(i * GLA_BLK, GLA_BLK), GLA_BLK)
        z = _dot(lr_ref[rows, :].astype(BF16), wg) + bg
        la = (jnp.minimum(z, 0.0) - jnp.log(1.0 + jnp.exp(-jnp.abs(z)))) * (1.0 / GLA_TAU)
        hi = la.astype(BF16)
        lo = (la - hi.astype(F32)).astype(BF16)
        pt = _dot(prefix_total, jnp.concatenate([hi, lo], axis=1))
        pt = pt[:, :HEAD_W] + pt[:, HEAD_W:]
        pre, tot = pt[:GLA_BLK, :], pt[GLA_BLK:, :]
        b_ref[rows, :] = jnp.where(fwd, pre, tot - pre + la)
        tot_ref[rows, :] = tot
        return carry

    lax.fori_loop(0, nblk, gates, 0, unroll=8)

    def block(i, carry):
        rows = pl.ds(pl.multiple_of(i * GLA_BLK, GLA_BLK), GLA_BLK)
        b = b_ref[rows, :]
        tot = tot_ref[rows, :]
        mid = 0.5 * tot
        blk = gqk_ref[rows, :].astype(F32)
        rot = pltpu.roll(blk, GLA_DK, 1)
        qq = jnp.where(fwd, blk, rot) * (GLA_DK ** -0.5)
        kk = jnp.where(fwd, rot, blk)
        qt = qq * jnp.exp(b - mid)
        kt = (kk * jnp.exp(mid - b)).astype(BF16)
        kh = (kk * jnp.exp(tot - b)).astype(BF16)
        qh_ref[rows, :] = (qq * jnp.exp(b)).astype(BF16)
        qfb = jnp.concatenate([jnp.where(fwd, qt, 0.0), jnp.where(fwd, 0.0, qt)], axis=0).astype(BF16)
        sfb = _dot_nt(qfb, kt)
        s = jnp.where(m_f, sfb[:GLA_BLK, :], jnp.where(m_b, sfb[GLA_BLK:, :], 0.0)).astype(BF16)
        v = gv_ref[rows, :]
        acc_ref[rows, :] = _dot(s, v)
        zero = jnp.zeros_like(kh)
        kx = jnp.concatenate([jnp.where(chunk_of_row == c, kh, zero) for c in range(cpb)], axis=1)
        ux = _dot_tn(v, kx)
        dec = jnp.exp(tot)
        for c in range(cpb):
            u_ref[i * cpb + c] = ux[:, c * HEAD_W:(c + 1) * HEAD_W]
            dec_ref[i * cpb + c] = jnp.broadcast_to(dec[c * C:c * C + 1, :], (8, HEAD_W))
        return carry

    lax.fori_loop(0, nblk, block, 0, unroll=8)

    def step(t, st):
        nf = t
        nb = nch - 1 - t
        stb = st.astype(BF16)
        rf = pl.ds(pl.multiple_of(nf * C, C), C)
        rb = pl.ds(pl.multiple_of(nb * C, C), C)
        qf = qh_ref[rf, :]
        qb = qh_ref[rb, :]
        zero = jnp.zeros_like(qf)
        o = _dot_nt(jnp.concatenate([jnp.where(fwd, qf, zero), jnp.where(fwd, zero, qb)], axis=0), stb)
        acc_ref[rf, :] += o[:C, :]
        acc_ref[rb, :] += o[C:, :]
        d = jnp.where(fwd, dec_ref[nf][0:1, :], dec_ref[nb][0:1, :])
        u = jnp.where(fwd, u_ref[nf], u_ref[nb])
        return st * d + u

    lax.fori_loop(0, nch, step, jnp.zeros((GLA_DV, HEAD_W), F32), unroll=16)

    ng = ng_ref[...]
    eb = min(T, 512)

    def epi(i, carry):
        rows = pl.ds(pl.multiple_of(i * eb, eb), eb)
        o = acc_ref[rows, :]
        y = o * lax.rsqrt(jnp.mean(o * o, axis=-1, keepdims=True) + RMS_EPS) * ng
        og = gog_ref[rows, :].astype(F32)
        o_ref[rows, :] = (y * (og * jax.nn.sigmoid(og))).astype(o_ref.dtype)
        return carry

    lax.fori_loop(0, T // eb, epi, 0)


def _gla(gqk, gv, gog, lr, wg, bg, ng, B, T):
    n = B * T
    head = pl.BlockSpec((T, HEAD_W), lambda b, h: (b, h))
    return pl.pallas_call(
        functools.partial(_gla_kernel, T=T),
        grid=(B, GLA_HEADS),
        in_specs=[head, head, head,
                  pl.BlockSpec((T, LANES), lambda b, h: (b, 0)),
                  pl.BlockSpec((None, LANES, HEAD_W), lambda b, h: (h, 0, 0)),
                  pl.BlockSpec((None, 1, HEAD_W), lambda b, h: (h, 0, 0)),
                  pl.BlockSpec((1, HEAD_W), lambda b, h: (0, 0))],
        out_specs=head,
        out_shape=jax.ShapeDtypeStruct((n, GLA_HEADS * HEAD_W), BF16),
        scratch_shapes=[pltpu.VMEM((T, HEAD_W), BF16),
                        pltpu.VMEM((T // GLA_CHUNK, GLA_DV, HEAD_W), F32),
                        pltpu.VMEM((T // GLA_CHUNK, 8, HEAD_W), F32),
                        pltpu.VMEM((T, GLA_DV), F32),
                        pltpu.VMEM((T, HEAD_W), F32),
                        pltpu.VMEM((T, HEAD_W), F32)],
        compiler_params=_params(("arbitrary", "arbitrary")),
        name="gla",
    )(gqk, gv, gog, lr, wg, bg, ng)


DATTN_KSPLIT = 2
DATTN_SCORE_BYTES = 32 * 1024 * 1024


def _dattn_kernel(dq_ref, dk_ref, dv_ref, ng_ref, lam_ref, o_ref,
                  s_ref, mp_ref, lp_ref, acc_ref, *, T, QB, TK, lam_init):
    lane = lax.broadcasted_iota(I32, (1, HEAD_W), 1)
    map0 = lane < DIFF_DH
    nk = T // TK
    q = dq_ref[...]
    zero = jnp.zeros_like(q)
    qs = jnp.concatenate([jnp.where(map0, q, zero), jnp.where(map0, zero, q)], axis=0)
    ntile = TK // LANES

    def lane_tiles(x):
        return [x[:, j * LANES:(j + 1) * LANES] for j in range(ntile)]

    mp_ref[...] = jnp.full(mp_ref.shape, -jnp.inf, F32)
    lp_ref[...] = jnp.zeros(lp_ref.shape, F32)
    acc_ref[...] = jnp.zeros(acc_ref.shape, F32)

    def scores(i, carry):
        cols = pl.ds(pl.multiple_of(i * TK, TK), TK)
        s = _dot_nt(qs, dk_ref[cols, :])
        s_ref[:, cols] = s
        mp_ref[...] = jnp.maximum(mp_ref[...], functools.reduce(jnp.maximum, lane_tiles(s)))
        return carry

    lax.fori_loop(0, nk, scores, 0)
    m = jnp.max(mp_ref[...], axis=-1, keepdims=True)

    def accum(i, carry):
        cols = pl.ds(pl.multiple_of(i * TK, TK), TK)
        p = jnp.exp2(s_ref[:, cols] - m)
        lp_ref[...] += functools.reduce(jnp.add, lane_tiles(p))
        acc_ref[...] += _dot(p.astype(BF16), dv_ref[cols, :])
        return carry

    lax.fori_loop(0, nk, accum, 0)
    on = acc_ref[...] / jnp.sum(lp_ref[...], axis=-1, keepdims=True)
    lam = lam_ref[...]
    o = on[:QB, :] - lam * on[QB:, :]
    y = o * lax.rsqrt(jnp.mean(o * o, axis=-1, keepdims=True) + RMS_EPS) * ng_ref[...]
    o_ref[...] = (y * (1.0 - lam_init)).astype(o_ref.dtype)


def _lambda_kernel(q1_ref, k1_ref, q2_ref, k2_ref, o_ref, *, lam_init):
    a = jnp.exp(jnp.sum(q1_ref[...] * k1_ref[...], axis=-1, keepdims=True))
    b = jnp.exp(jnp.sum(q2_ref[...] * k2_ref[...], axis=-1, keepdims=True))
    o_ref[...] = a - b + lam_init


def _dattn(dq, dk, dv, ng, lam, B, T, QB, lam_init):
    n = B * T
    TK = T // DATTN_KSPLIT
    nq = T // QB
    full = lambda shape: pl.BlockSpec(shape, lambda b, h, q: (0, 0))
    kv = pl.BlockSpec((T, HEAD_W), lambda b, h, q: (b, h))
    qo = pl.BlockSpec((QB, HEAD_W), lambda b, h, q: (b * nq + q, h))
    return pl.pallas_call(
        functools.partial(_dattn_kernel, T=T, QB=QB, TK=TK, lam_init=lam_init),
        grid=(B, DIFF_HEADS, nq),
        in_specs=[qo, kv, kv, full((1, HEAD_W)), full((1, 1))],
        out_specs=qo,
        out_shape=jax.ShapeDtypeStruct((n, DIFF_HEADS * HEAD_W), BF16),
        scratch_shapes=[pltpu.VMEM((2 * QB, T), F32),
                        pltpu.VMEM((2 * QB, LANES), F32),
                        pltpu.VMEM((2 * QB, LANES), F32),
                        pltpu.VMEM((2 * QB, DIFF_DV), F32)],
        compiler_params=_params(("arbitrary", "arbitrary", "arbitrary")),
        name="dattn",
    )(dq, dk, dv, ng, lam)


def _mixout_kernel(a_ref, d_ref, x_ref, w_ref, g_ref, wr_ref, x1_ref, h2_ref, aff_ref):
    half = GLA_HEADS * HEAD_W
    y = _dot(a_ref[...], w_ref[:half, :]) + _dot(d_ref[...], w_ref[half:, :])
    x1 = x_ref[...] + y
    h2 = x1 * lax.rsqrt(jnp.mean(x1 * x1, axis=-1, keepdims=True) + RMS_EPS) * g_ref[...]
    for s in range(TOK_SUB):
        cols = slice(s * LANES, (s + 1) * LANES)
        x1_ref[_tok_rows(s, x1.shape[0]), :] = x1[:, cols]
        h2_ref[_tok_rows(s, x1.shape[0]), :] = h2[:, cols]
    logits = jnp.zeros((N_EXPERTS, h2.shape[0]), F32)
    hp = _split3(h2)
    wp = _split3(wr_ref[...])
    for i, j in ((0, 0), (0, 1), (1, 0)):
        logits = logits + _dot_nt(wp[i], hp[j])
    m = jnp.max(logits, axis=0, keepdims=True)
    e = jnp.exp(logits - m)
    aff_ref[...] = e / jnp.sum(e, axis=0, keepdims=True)


def _mixout(a, d, x2, w_out, g2, wr_t, tm):
    n = x2.shape[0]
    row = lambda i: (i, 0)
    half = GLA_HEADS * HEAD_W
    return pl.pallas_call(
        _mixout_kernel,
        grid=(n // tm,),
        in_specs=[pl.BlockSpec((tm, half), row), pl.BlockSpec((tm, half), row),
                  pl.BlockSpec((tm, D_MODEL), row),
                  pl.BlockSpec((D_MODEL, D_MODEL), lambda i: (0, 0)),
                  pl.BlockSpec((1, D_MODEL), lambda i: (0, 0)),
                  pl.BlockSpec((N_EXPERTS, D_MODEL), lambda i: (0, 0))],
        out_specs=[pl.BlockSpec((tm * TOK_SUB, LANES), row), pl.BlockSpec((tm * TOK_SUB, LANES), row),
                   pl.BlockSpec((N_EXPERTS, tm), lambda i: (0, i))],
        out_shape=[jax.ShapeDtypeStruct((n * TOK_SUB, LANES), F32), jax.ShapeDtypeStruct((n * TOK_SUB, LANES), F32),
                   jax.ShapeDtypeStruct((N_EXPERTS, n), F32)],
        compiler_params=_params(("arbitrary",)),
        name="mixout",
    )(a, d, x2, w_out, g2, wr_t)


ROUTE_BISECTIONS = 40


def _route_kernel(aff_ref, idx_ref, gate_ref, *, n, cap):
    R = n // LANES
    E = N_EXPERTS
    aff = aff_ref[...]

    def count(mask):
        c = jnp.sum(jnp.where(mask, 1.0, 0.0), axis=1, keepdims=True)
        return jnp.sum(c, axis=2, keepdims=True)

    capf = float(cap)
    k = jnp.zeros((E, 1, 1), F32)
    for bit in (64.0, 32.0, 16.0, 8.0, 4.0, 2.0, 1.0):
        cand = k + bit
        k = jnp.where(count(aff >= jnp.exp2(1.0 - cand)) < capf, cand, k)
    hi = jnp.exp2(1.0 - k)
    lo = jnp.where(k >= 127.0, 0.0, jnp.exp2(-jnp.minimum(k, 126.0)))

    def bisect(i, c):
        lo, hi = c
        mid = 0.5 * (lo + hi)
        ok = count(aff >= mid) >= capf
        return jnp.where(ok, mid, lo), jnp.where(ok, hi, mid)

    lo, hi = lax.fori_loop(0, ROUTE_BISECTIONS, bisect, (lo, hi))
    gt = aff >= hi
    eq = (aff >= lo) & (aff < hi)
    need = capf - count(gt)
    tok = (lax.broadcasted_iota(I32, (1, R, LANES), 1) * LANES
           + lax.broadcasted_iota(I32, (1, R, LANES), 2))
    nbits = max(1, (n - 1).bit_length())

    def tie_step(i, m0):
        cand = m0 | jnp.left_shift(jnp.int32(1), nbits - 1 - i)
        return jnp.where(count(eq & (tok < cand)) < need, cand, m0)

    m0 = lax.fori_loop(0, nbits, tie_step, jnp.zeros((E, 1, 1), I32))
    sel = gt | (eq & (tok <= m0))

    ci = lax.broadcasted_iota(I32, (LANES, LANES), 0)
    cj = lax.broadcasted_iota(I32, (LANES, LANES), 1)
    upper = jnp.where(ci <= cj, 1.0, 0.0).astype(BF16)
    ri = lax.broadcasted_iota(I32, (R, R), 0)
    rj = lax.broadcasted_iota(I32, (R, R), 1)
    lower = jnp.where(rj < ri, 1.0, 0.0).astype(BF16)
    jrow = lax.broadcasted_iota(I32, (1, cap), 1).astype(F32)
    rcol = lax.broadcasted_iota(I32, (R, 1), 0).astype(F32)
    ccol = lax.broadcasted_iota(I32, (LANES, 1), 0).astype(F32)

    for e in range(E):
        sel_e = sel[e]
        w = _dot(jnp.where(sel_e, 1.0, 0.0).astype(BF16), upper)
        rt = w[:, LANES - 1:LANES]
        rp = _dot(lower, jnp.broadcast_to(rt, (R, LANES)).astype(BF16))[:, 0:1]
        rows_t = jnp.where((rp <= jrow) & (jrow < rp + rt), 1.0, 0.0)
        k1 = jrow - jnp.sum(rows_t * rp, axis=0, keepdims=True) + 1.0
        rows_b = rows_t.astype(BF16)
        wrow = _dot(jnp.where(sel_e, w, 0.0).T.astype(BF16), rows_b)
        cols_t = jnp.where(wrow == k1, 1.0, 0.0)
        r_of = jnp.sum(rows_t * rcol, axis=0, keepdims=True)
        c_of = jnp.sum(cols_t * ccol, axis=0, keepdims=True)
        idx_ref[e] = (r_of * float(LANES) + c_of).astype(I32)
        a3 = _split3(aff[e].T)
        arow = _dot(a3[0], rows_b) + _dot(a3[1], rows_b) + _dot(a3[2], rows_b)
        gate_ref[e] = jnp.sum(cols_t * arow, axis=0, keepdims=True)


def _route(aff_t, n, cap):
    R = n // LANES
    return pl.pallas_call(
        functools.partial(_route_kernel, n=n, cap=cap),
        out_shape=[jax.ShapeDtypeStruct((N_EXPERTS, 1, cap), I32),
                   jax.ShapeDtypeStruct((N_EXPERTS, 1, cap), F32)],
        compiler_params=pltpu.CompilerParams(vmem_limit_bytes=VMEM_LIMIT),
        name="route",
    )(aff_t.reshape(N_EXPERTS, R, LANES))


FFN_FC = 512
FFN_TT = 1024
FFN_DMA_UNROLL = 16


def _ffn_kernel(idx_ref, gate_ref, h2_hbm, wg_ref, wu_ref, wd_ref, x1_hbm, out_hbm,
                stage_ref, xb_ref, acc_ref, orow_ref, sems, *, cap, nfc):
    del x1_hbm
    e = pl.program_id(0)
    fc = pl.program_id(1)
    ne = pl.num_programs(0)
    g_sem, o_sem, s_sem = sems.at[0], sems.at[1], sems.at[2]
    tt = min(FFN_TT, cap)

    def row_copy(src, s_tok, dst, d_tok, sem):
        s_rows = pl.ds(pl.multiple_of(s_tok * TOK_SUB, TOK_SUB), TOK_SUB)
        d_rows = pl.ds(pl.multiple_of(d_tok * TOK_SUB, TOK_SUB), TOK_SUB)
        return pltpu.make_async_copy(src.at[s_rows], dst.at[d_rows], sem)

    def issue_gather(ee, src, dst, sem):
        def body(j, c):
            row_copy(src, idx_ref[ee * cap + j], dst, j, sem).start()
            return c
        lax.fori_loop(0, cap, body, 0, unroll=FFN_DMA_UNROLL)

    def wait_rows(buf, sem):
        pltpu.make_async_copy(buf, buf, sem).wait()

    gather_step = nfc - 2
    x_steps = [k for k in range(nfc) if k not in (0, gather_step)]
    assert 0 < gather_step and x_steps, "the copy schedule needs at least three hidden-dim steps"
    ntile = cap // tt
    per_tile = cap // ntile
    x_per_tile = cap // (len(x_steps) * ntile)
    e_next = jnp.minimum(e + 1, ne - 1)

    @pl.when(fc == 0)
    def _():
        @pl.when(e == 0)
        def _():
            issue_gather(0, h2_hbm, stage_ref, g_sem)
        wait_rows(stage_ref, g_sem)
        for s in range(TOK_SUB):
            xb_ref[:, s * LANES:(s + 1) * LANES] = stage_ref[_tok_rows(s, cap), :].astype(BF16)

    @pl.when((fc == gather_step) & (e > 0))
    def _():
        wait_rows(orow_ref, s_sem)

    def finish_rows(ee, tok0, ntok):
        for b in range(ntok // LANES):
            first = tok0 + b * LANES
            rows = pl.ds(first if isinstance(first, int) else pl.multiple_of(first, LANES), LANES)
            g = gate_ref.at[ee][:, rows]
            gcol = jnp.broadcast_to(g, (LANES, LANES)).T
            for s in range(TOK_SUB):
                trows = pl.ds(first * TOK_SUB + s, LANES, stride=TOK_SUB)
                orow_ref[trows, :] += acc_ref[rows, s * LANES:(s + 1) * LANES] * gcol

    def scatter_prev(t):
        for jj in range(per_tile):
            j = t * per_tile + jj
            row_copy(orow_ref, j, out_hbm, idx_ref[(e - 1) * cap + j], s_sem).start()

    def gather_out(t):
        for jj in range(per_tile):
            j = t * per_tile + jj
            row_copy(out_hbm, idx_ref[e * cap + j], orow_ref, j, o_sem).start()

    def gather_next(part):
        def issue(t):
            for jj in range(x_per_tile):
                j = (part * ntile + t) * x_per_tile + jj
                row_copy(h2_hbm, idx_ref[e_next * cap + j], stage_ref, j, g_sem).start()
        return issue

    def tile_loop(copies, first):
        def tile(t, c):
            if copies is not None:
                copies(t)
            rows = pl.ds(pl.multiple_of(t * tt, tt), tt)
            x = xb_ref[rows, :]
            a = _dot(x, wg_ref[...].astype(BF16))
            u = _dot(x, wu_ref[...].astype(BF16))
            hm = (a * jax.nn.sigmoid(a) * u).astype(BF16)
            y = _dot(hm, wd_ref[...].astype(BF16))
            if first:
                acc_ref[rows, :] = y
            else:
                acc_ref[rows, :] += y
            return c
        lax.fori_loop(0, ntile, tile, 0)

    pl.when((fc == 0) & (e == 0))(lambda: tile_loop(None, True))
    pl.when((fc == 0) & (e > 0))(lambda: tile_loop(scatter_prev, True))
    pl.when(fc == gather_step)(lambda: tile_loop(gather_out, False))
    for part, k in enumerate(x_steps):
        pl.when(fc == k)(functools.partial(tile_loop, gather_next(part), False))

    @pl.when(fc == nfc - 1)
    def _():
        wait_rows(orow_ref, o_sem)
        finish_rows(e, 0, cap)

        @pl.when(e == ne - 1)
        def _():
            def body(j, c):
                row_copy(orow_ref, j, out_hbm, idx_ref[e * cap + j], s_sem).start()
                return c
            lax.fori_loop(0, cap, body, 0, unroll=FFN_DMA_UNROLL)
            wait_rows(orow_ref, s_sem)
            wait_rows(stage_ref, g_sem)


def _ffn(idx, gate, h2, x1, w_gate, w_up, w_down, cap):
    n = h2.shape[0] // TOK_SUB
    nfc = D_EXPERT // FFN_FC
    grid_spec = pltpu.PrefetchScalarGridSpec(
        num_scalar_prefetch=1,
        grid=(N_EXPERTS, nfc),
        in_specs=[pl.BlockSpec((N_EXPERTS, 1, cap), lambda e, f, idx: (0, 0, 0)),
                  pl.BlockSpec(memory_space=pl.ANY),
                  pl.BlockSpec((None, D_MODEL, FFN_FC), lambda e, f, idx: (e, 0, f)),
                  pl.BlockSpec((None, D_MODEL, FFN_FC), lambda e, f, idx: (e, 0, f)),
                  pl.BlockSpec((None, FFN_FC, D_MODEL), lambda e, f, idx: (e, f, 0)),
                  pl.BlockSpec(memory_space=pl.ANY)],
        out_specs=pl.BlockSpec(memory_space=pl.ANY),
        scratch_shapes=[pltpu.VMEM((cap * TOK_SUB, LANES), F32),
                        pltpu.VMEM((cap, D_MODEL), BF16),
                        pltpu.VMEM((cap, D_MODEL), F32),
                        pltpu.VMEM((cap * TOK_SUB, LANES), F32),
                        pltpu.SemaphoreType.DMA((3,))],
    )
    return pl.pallas_call(
        functools.partial(_ffn_kernel, cap=cap, nfc=nfc),
        grid_spec=grid_spec,
        out_shape=jax.ShapeDtypeStruct((n * TOK_SUB, LANES), F32),
        input_output_aliases={6: 0},
        compiler_params=_params(("arbitrary", "arbitrary")),
        name="ffn",
    )(idx.reshape(-1), gate, h2, w_gate, w_up, w_down, x1)


def _untile_kernel(x_ref, o_ref):
    for s in range(TOK_SUB):
        o_ref[:, s * LANES:(s + 1) * LANES] = x_ref[_tok_rows(s, o_ref.shape[0]), :]


def _untile(xt, tm):
    n = xt.shape[0] // TOK_SUB
    return pl.pallas_call(
        _untile_kernel,
        grid=(n // tm,),
        in_specs=[pl.BlockSpec((tm * TOK_SUB, LANES), lambda i: (i, 0))],
        out_specs=pl.BlockSpec((tm, D_MODEL), lambda i: (i, 0)),
        out_shape=jax.ShapeDtypeStruct((n, D_MODEL), F32),
        compiler_params=_params(("arbitrary",)),
        name="untile",
    )(xt)


def _rope_tables(T):
    half = DIFF_DH // 2
    inv_freq = jnp.power(ROPE_THETA, -jnp.arange(0, DIFF_DH, 2, dtype=F32) / DIFF_DH)
    ang = jnp.arange(T, dtype=F32)[:, None] * inv_freq[None, :]
    cos, sin = jnp.cos(ang), jnp.sin(ang)
    reps = HEAD_W // DIFF_DH
    return (jnp.concatenate([cos, cos] * reps, axis=-1),
            jnp.concatenate([-sin, sin] * reps, axis=-1))


def _layout_weights(w_in, gla_wg_f, gla_bg_f, gla_wg_b, gla_bg_b):
    q0, k0, v0, g0 = 0, 256, 512, 1024
    lr0, dq0 = 1536, 1568
    cols = []
    for h in range(GLA_HEADS):
        cols += [w_in[:, q0 + h * GLA_DK:q0 + (h + 1) * GLA_DK], w_in[:, k0 + h * GLA_DK:k0 + (h + 1) * GLA_DK]]
    cols += [w_in[:, v0:g0], w_in[:, g0:lr0], w_in[:, dq0:], w_in[:, lr0:dq0],
             jnp.zeros((D_MODEL, LANES - 2 * GLA_RANK), w_in.dtype)]
    w_r = jnp.concatenate(cols, axis=1).astype(BF16)
    wg = jnp.zeros((GLA_HEADS, LANES, HEAD_W), F32)
    bg = jnp.zeros((GLA_HEADS, 1, HEAD_W), F32)
    for h in range(GLA_HEADS):
        hs = slice(h * GLA_DK, (h + 1) * GLA_DK)
        wg = wg.at[h, :GLA_RANK, :GLA_DK].set(gla_wg_f[:, hs])
        wg = wg.at[h, GLA_RANK:2 * GLA_RANK, GLA_DK:].set(gla_wg_b[:, hs])
        bg = bg.at[h, 0, :GLA_DK].set(gla_bg_f[hs])
        bg = bg.at[h, 0, GLA_DK:].set(gla_bg_b[hs])
    return w_r, wg.astype(BF16), bg


def _lambda(lq1, lk1, lq2, lk2, lam_init):
    pad = lambda v: jnp.pad(v.astype(F32), (0, LANES - v.shape[0])).reshape(1, LANES)
    return pl.pallas_call(
        functools.partial(_lambda_kernel, lam_init=lam_init),
        out_shape=jax.ShapeDtypeStruct((1, 1), F32),
        name="lam",
    )(pad(lq1), pad(lk1), pad(lq2), pad(lk2))


def _encoder_layer(x, layer, lw, *, tm=512):
    (norm1_g, w_in, gla_wg_f, gla_bg_f, gla_wg_b, gla_bg_b, gla_norm_g, qk_norm_q, qk_norm_k,
     lambda_q1, lambda_k1, lambda_q2, lambda_k2, diff_norm_g, w_out, norm2_g, w_router,
     w_gate, w_up, w_down) = lw
    B, T, D = x.shape
    n = B * T
    cap = max(1, EC_FACTOR * n // N_EXPERTS)
    tm = min(tm, n)
    qb = min(T, DATTN_SCORE_BYTES // (2 * 4 * T))
    x2 = x.reshape(n, D)
    w_r, wg, bg = _layout_weights(w_in, gla_wg_f, gla_bg_f, gla_wg_b, gla_bg_b)
    row = lambda v: v.astype(F32).reshape(1, -1)
    per_map = lambda v: jnp.tile(v.astype(F32), 2 * DIFF_HEADS).reshape(1, -1)

    cos, sin = _rope_tables(T)
    gqk, gv, gog, dq, dk, dv, lr = _proj(x2, row(norm1_g), w_r, cos, sin,
                                         per_map(qk_norm_q), per_map(qk_norm_k), min(tm, T))
    a = _gla(gqk, gv, gog, lr, wg, bg, row(gla_norm_g), B, T)
    lam_init = _lambda_init(layer)
    lam = _lambda(lambda_q1, lambda_k1, lambda_q2, lambda_k2, lam_init)
    d = _dattn(dq, dk, dv, row(diff_norm_g), lam, B, T, qb, lam_init)
    x1, h2, aff_t = _mixout(a, d, x2, w_out.astype(BF16), row(norm2_g), w_router.T.astype(F32), tm)
    idx, gate = _route(aff_t, n, cap)
    y = _untile(_ffn(idx, gate, h2, x1, w_gate, w_up, w_down, cap), min(2 * tm, n))
    return y.reshape(B, T, D)


def kernel(x_prompt, x_sample, norm1_g, w_in, gla_wg_f, gla_bg_f, gla_wg_b, gla_bg_b, gla_norm_g, qk_norm_q, qk_norm_k, lambda_q1, lambda_k1, lambda_q2, lambda_k2, diff_norm_g, w_out, norm2_g, w_router, w_gate, w_up, w_down):
    y_prompt, y_sample = x_prompt, x_sample
    for l in range(norm1_g.shape[0]):
        lw = (norm1_g[l], w_in[l], gla_wg_f[l], gla_bg_f[l], gla_wg_b[l], gla_bg_b[l],
              gla_norm_g[l], qk_norm_q[l], qk_norm_k[l], lambda_q1[l], lambda_k1[l],
              lambda_q2[l], lambda_k2[l], diff_norm_g[l], w_out[l], norm2_g[l],
              w_router[l], w_gate[l], w_up[l], w_down[l])
        y_prompt = _encoder_layer(y_prompt, l, lw)
        y_sample = _encoder_layer(y_sample, l, lw)
    return (y_prompt, y_sample)
```

```python
import functools
import math

import jax
import jax.numpy as jnp
from jax import lax
from jax.experimental import pallas as pl
from jax.experimental.pallas import tpu as pltpu

F32 = jnp.float32
BF16 = jnp.bfloat16
I32 = jnp.int32

LANES = 128
MXU_W = 256
VMEM_LIMIT = 56 * 1024 * 1024

D_MODEL = 1024
GLA_HEADS = 4
GLA_DK = 64
GLA_DV = 128
GLA_RANK = 16
GLA_TAU = 16.0
GLA_CHUNK = 64
DIFF_HEADS = 4
DIFF_DH = 64
DIFF_DV = 128
ROPE_THETA = 10000.0
N_EXPERTS = 16
EC_FACTOR = 2
D_EXPERT = 2 * D_MODEL
RMS_EPS = 1e-6
LOG2_E = 1.4426950408889634


def _lambda_init(layer):
    return 0.8 - 0.6 * math.exp(-0.3 * layer)


HEAD_W = 128
PROJ_GROUPS = 6
PROJ_W = PROJ_GROUPS * 4 * HEAD_W + LANES
TOK_SUB = D_MODEL // LANES


def _tok_rows(s, ntok, first_tok=0):
    return pl.ds(first_tok * TOK_SUB + s, ntok, stride=TOK_SUB)


def _dot(a, b):
    return jnp.dot(a, b, preferred_element_type=F32)


def _dot_nt(a, b):
    return lax.dot_general(a, b, (((1,), (1,)), ((), ())), preferred_element_type=F32)


def _dot_tn(a, b):
    return lax.dot_general(a, b, (((0,), (0,)), ((), ())), preferred_element_type=F32)


def _split3(x):
    a = x.astype(BF16)
    r = x - a.astype(F32)
    b = r.astype(BF16)
    c = (r - b.astype(F32)).astype(BF16)
    return a, b, c


def _params(sem):
    return pltpu.CompilerParams(dimension_semantics=sem, vmem_limit_bytes=VMEM_LIMIT)


def _qk_prep(x, gain, cos, sin, same_map):
    x2 = (x * x).astype(BF16)
    width = x.shape[1]
    mw = same_map.shape[0]
    ms = jnp.concatenate([_dot(x2[:, c:c + mw], same_map) for c in range(0, width, mw)], axis=1)
    y = x * lax.rsqrt(ms * (1.0 / DIFF_DH) + RMS_EPS) * gain
    half = DIFF_DH // 2
    first = (lax.broadcasted_iota(I32, (1, width), 1) & half) == 0
    swapped = jnp.where(first, pltpu.roll(y, width - half, 1), pltpu.roll(y, half, 1))
    return y * cos + swapped * sin


def _proj_kernel(x_ref, g_ref, w_ref, cos_ref, sin_ref, qg_ref, kg_ref, map_ref,
                 gqk_ref, gv_ref, gog_ref, dq_ref, dk_ref, dv_ref, lr_ref):
    x = x_ref[...]
    ms = jnp.mean(x * x, axis=-1, keepdims=True)
    h = (x * lax.rsqrt(ms + RMS_EPS) * g_ref[...]).astype(BF16)
    gw = 4 * HEAD_W
    group = lambda i: _dot(h, w_ref[:, i * gw:(i + 1) * gw])
    for i, o in ((0, gqk_ref), (1, gv_ref), (2, gog_ref), (5, dv_ref)):
        o[...] = group(i).astype(o.dtype)
    cos = jnp.concatenate([cos_ref[...]] * DIFF_HEADS, axis=1)
    sin = jnp.concatenate([sin_ref[...]] * DIFF_HEADS, axis=1)
    q = _qk_prep(group(3), qg_ref[...], cos, sin, map_ref[...])
    dq_ref[...] = (q * (DIFF_DH ** -0.5 * LOG2_E)).astype(dq_ref.dtype)
    dk_ref[...] = _qk_prep(group(4), kg_ref[...], cos, sin, map_ref[...]).astype(dk_ref.dtype)
    lr_ref[...] = _dot(h, w_ref[:, PROJ_GROUPS * gw:])


def _proj(x2, g1, w_r, cos, sin, qg, kg, tm):
    n = x2.shape[0]
    T = cos.shape[0]
    gw = 4 * HEAD_W
    row = lambda i: (i, 0)
    const = lambda i: (0, 0)
    pos = lambda i: (i % (T // tm), 0)
    ri = lax.broadcasted_iota(I32, (MXU_W, MXU_W), 0)
    ci = lax.broadcasted_iota(I32, (MXU_W, MXU_W), 1)
    same_map = ((ri // DIFF_DH) == (ci // DIFF_DH)).astype(BF16)
    outs = [jax.ShapeDtypeStruct((n, gw), BF16)] * PROJ_GROUPS + [jax.ShapeDtypeStruct((n, LANES), F32)]
    return pl.pallas_call(
        _proj_kernel,
        grid=(n // tm,),
        in_specs=[pl.BlockSpec((tm, D_MODEL), row),
                  pl.BlockSpec((1, D_MODEL), const),
                  pl.BlockSpec((D_MODEL, PROJ_W), const),
                  pl.BlockSpec((tm, HEAD_W), pos), pl.BlockSpec((tm, HEAD_W), pos),
                  pl.BlockSpec((1, gw), const), pl.BlockSpec((1, gw), const),
                  pl.BlockSpec((MXU_W, MXU_W), const)],
        out_specs=[pl.BlockSpec((tm, gw), row)] * PROJ_GROUPS + [pl.BlockSpec((tm, LANES), row)],
        out_shape=outs,
        compiler_params=_params(("arbitrary",)),
        name="proj",
    )(x2, g1, w_r, cos, sin, qg, kg, same_map)


GLA_BLK = 256


def _gla_kernel(gqk_ref, gv_ref, gog_ref, lr_ref, wg_ref, bg_ref, ng_ref, o_ref,
                qh_ref, u_ref, dec_ref, acc_ref, b_ref, tot_ref, *, T):
    C = GLA_CHUNK
    nblk = T // GLA_BLK
    nch = T // C
    lane = lax.broadcasted_iota(I32, (1, HEAD_W), 1)
    fwd = lane < GLA_DK
    ri = lax.broadcasted_iota(I32, (GLA_BLK, GLA_BLK), 0)
    ci = lax.broadcasted_iota(I32, (GLA_BLK, GLA_BLK), 1)
    same = (ri // C) == (ci // C)
    prefix_total = jnp.concatenate([jnp.where(same & (ci <= ri), 1.0, 0.0),
                                    jnp.where(same, 1.0, 0.0)], axis=0).astype(BF16)
    m_f = same & (ci <= ri)
    m_b = same & (ci > ri)
    wg = wg_ref[...]
    bg = bg_ref[...]
    cpb = GLA_BLK // C
    chunk_of_row = lax.broadcasted_iota(I32, (GLA_BLK, 1), 0) // C

    def gates(i, carry):
        rows = pl.ds(pl.multiple_of(i * GLA_BLK, GLA_BLK), GLA_BLK)
        z = _dot(lr_ref[rows, :].astype(BF16), wg) + bg
        la = (jnp.minimum(z, 0.0) - jnp.log(1.0 + jnp.exp(-jnp.abs(z)))) * (1.0 / GLA_TAU)
        hi = la.astype(BF16)
        lo = (la - hi.astype(F32)).astype(BF16)
        pt = _dot(prefix_total, jnp.concatenate([hi, lo], axis=1))
        pt = pt[:, :HEAD_W] + pt[:, HEAD_W:]
        pre, tot = pt[:GLA_BLK, :], pt[GLA_BLK:, :]
        b_ref[rows, :] = jnp.where(fwd, pre, tot - pre + la)
        tot_ref[rows, :] = tot
        return carry

    lax.fori_loop(0, nblk, gates, 0, unroll=8)

    def block(i, carry):
        rows = pl.ds(pl.multiple_of(i * GLA_BLK, GLA_BLK), GLA_BLK)
        b = b_ref[rows, :]
        tot = tot_ref[rows, :]
        mid = 0.5 * tot
        blk = gqk_ref[rows, :].astype(F32)
        rot = pltpu.roll(blk, GLA_DK, 1)
        qq = jnp.where(fwd, blk, rot) * (GLA_DK ** -0.5)
        kk = jnp.where(fwd, rot, blk)
        qt = qq * jnp.exp(b - mid)
        kt = (kk * jnp.exp(mid - b)).astype(BF16)
        kh = (kk * jnp.exp(tot - b)).astype(BF16)
        qh_ref[rows, :] = (qq * jnp.exp(b)).astype(BF16)
        qfb = jnp.concatenate([jnp.where(fwd, qt, 0.0), jnp.where(fwd, 0.0, qt)], axis=0).astype(BF16)
        sfb = _dot_nt(qfb, kt)
        s = jnp.where(m_f, sfb[:GLA_BLK, :], jnp.where(m_b, sfb[GLA_BLK:, :], 0.0)).astype(BF16)
        v = gv_ref[rows, :]
        acc_ref[rows, :] = _dot(s, v)
        zero = jnp.zeros_like(kh)
        kx = jnp.concatenate([jnp.where(chunk_of_row == c, kh, zero) for c in range(cpb)], axis=1)
        ux = _dot_tn(v, kx)
        dec = jnp.exp(tot)
        for c in range(cpb):
            u_ref[i * cpb + c] = ux[:, c * HEAD_W:(c + 1) * HEAD_W]
            dec_ref[i * cpb + c] = jnp.broadcast_to(dec[c * C:c * C + 1, :], (8, HEAD_W))
        return carry

    lax.fori_loop(0, nblk, block, 0, unroll=8)

    def step(t, st):
        nf = t
        nb = nch - 1 - t
        stb = st.astype(BF16)
        rf = pl.ds(pl.multiple_of(nf * C, C), C)
        rb = pl.ds(pl.multiple_of(nb * C, C), C)
        qf = qh_ref[rf, :]
        qb = qh_ref[rb, :]
        zero = jnp.zeros_like(qf)
        o = _dot_nt(jnp.concatenate([jnp.where(fwd, qf, zero), jnp.where(fwd, zero, qb)], axis=0), stb)
        acc_ref[rf, :] += o[:C, :]
        acc_ref[rb, :] += o[C:, :]
        d = jnp.where(fwd, dec_ref[nf][0:1, :], dec_ref[nb][0:1, :])
        u = jnp.where(fwd, u_ref[nf], u_ref[nb])
        return st * d + u

    lax.fori_loop(0, nch, step, jnp.zeros((GLA_DV, HEAD_W), F32), unroll=16)

    ng = ng_ref[...]
    eb = min(T, 512)

    def epi(i, carry):
        rows = pl.ds(pl.multiple_of(i * eb, eb), eb)
        o = acc_ref[rows, :]
        y = o * lax.rsqrt(jnp.mean(o * o, axis=-1, keepdims=True) + RMS_EPS) * ng
        og = gog_ref[rows, :].astype(F32)
        o_ref[rows, :] = (y * (og * jax.nn.sigmoid(og))).astype(o_ref.dtype)
        return carry

    lax.fori_loop(0, T // eb, epi, 0)


def _gla(gqk, gv, gog, lr, wg, bg, ng, B, T):
    n = B * T
    head = pl.BlockSpec((T, HEAD_W), lambda b, h: (b, h))
    return pl.pallas_call(
        functools.partial(_gla_kernel, T=T),
        grid=(B, GLA_HEADS),
        in_specs=[head, head, head,
                  pl.BlockSpec((T, LANES), lambda b, h: (b, 0)),
                  pl.BlockSpec((None, LANES, HEAD_W), lambda b, h: (h, 0, 0)),
                  pl.BlockSpec((None, 1, HEAD_W), lambda b, h: (h, 0, 0)),
                  pl.BlockSpec((1, HEAD_W), lambda b, h: (0, 0))],
        out_specs=head,
        out_shape=jax.ShapeDtypeStruct((n, GLA_HEADS * HEAD_W), BF16),
        scratch_shapes=[pltpu.VMEM((T, HEAD_W), BF16),
                        pltpu.VMEM((T // GLA_CHUNK, GLA_DV, HEAD_W), F32),
                        pltpu.VMEM((T // GLA_CHUNK, 8, HEAD_W), F32),
                        pltpu.VMEM((T, GLA_DV), F32),
                        pltpu.VMEM((T, HEAD_W), F32),
                        pltpu.VMEM((T, HEAD_W), F32)],
        compiler_params=_params(("arbitrary", "arbitrary")),
        name="gla",
    )(gqk, gv, gog, lr, wg, bg, ng)


DATTN_KSPLIT = 2
DATTN_SCORE_BYTES = 32 * 1024 * 1024


def _dattn_kernel(dq_ref, dk_ref, dv_ref, ng_ref, lam_ref, o_ref,
                  s_ref, mp_ref, lp_ref, acc_ref, *, T, QB, TK, lam_init):
    lane = lax.broadcasted_iota(I32, (1, HEAD_W), 1)
    map0 = lane < DIFF_DH
    nk = T // TK
    q = dq_ref[...]
    zero = jnp.zeros_like(q)
    qs = jnp.concatenate([jnp.where(map0, q, zero), jnp.where(map0, zero, q)], axis=0)
    ntile = TK // LANES

    def lane_tiles(x):
        return [x[:, j * LANES:(j + 1) * LANES] for j in range(ntile)]

    mp_ref[...] = jnp.full(mp_ref.shape, -jnp.inf, F32)
    lp_ref[...] = jnp.zeros(lp_ref.shape, F32)
    acc_ref[...] = jnp.zeros(acc_ref.shape, F32)

    def scores(i, carry):
        cols = pl.ds(pl.multiple_of(i * TK, TK), TK)
        s = _dot_nt(qs, dk_ref[cols, :])
        s_ref[:, cols] = s
        mp_ref[...] = jnp.maximum(mp_ref[...], functools.reduce(jnp.maximum, lane_tiles(s)))
        return carry

    lax.fori_loop(0, nk, scores, 0)
    m = jnp.max(mp_ref[...], axis=-1, keepdims=True)

    def accum(i, carry):
        cols = pl.ds(pl.multiple_of(i * TK, TK), TK)
        p = jnp.exp2(s_ref[:, cols] - m)
        lp_ref[...] += functools.reduce(jnp.add, lane_tiles(p))
        acc_ref[...] += _dot(p.astype(BF16), dv_ref[cols, :])
        return carry

    lax.fori_loop(0, nk, accum, 0)
    on = acc_ref[...] / jnp.sum(lp_ref[...], axis=-1, keepdims=True)
    lam = lam_ref[...]
    o = on[:QB, :] - lam * on[QB:, :]
    y = o * lax.rsqrt(jnp.mean(o * o, axis=-1, keepdims=True) + RMS_EPS) * ng_ref[...]
    o_ref[...] = (y * (1.0 - lam_init)).astype(o_ref.dtype)


def _lambda_kernel(q1_ref, k1_ref, q2_ref, k2_ref, o_ref, *, lam_init):
    a = jnp.exp(jnp.sum(q1_ref[...] * k1_ref[...], axis=-1, keepdims=True))
    b = jnp.exp(jnp.sum(q2_ref[...] * k2_ref[...], axis=-1, keepdims=True))
    o_ref[...] = a - b + lam_init


def _dattn(dq, dk, dv, ng, lam, B, T, QB, lam_init):
    n = B * T
    TK = T // DATTN_KSPLIT
    nq = T // QB
    full = lambda shape: pl.BlockSpec(shape, lambda b, h, q: (0, 0))
    kv = pl.BlockSpec((T, HEAD_W), lambda b, h, q: (b, h))
    qo = pl.BlockSpec((QB, HEAD_W), lambda b, h, q: (b * nq + q, h))
    return pl.pallas_call(
        functools.partial(_dattn_kernel, T=T, QB=QB, TK=TK, lam_init=lam_init),
        grid=(B, DIFF_HEADS, nq),
        in_specs=[qo, kv, kv, full((1, HEAD_W)), full((1, 1))],
        out_specs=qo,
        out_shape=jax.ShapeDtypeStruct((n, DIFF_HEADS * HEAD_W), BF16),
        scratch_shapes=[pltpu.VMEM((2 * QB, T), F32),
                        pltpu.VMEM((2 * QB, LANES), F32),
                        pltpu.VMEM((2 * QB, LANES), F32),
                        pltpu.VMEM((2 * QB, DIFF_DV), F32)],
        compiler_params=_params(("arbitrary", "arbitrary", "arbitrary")),
        name="dattn",
    )(dq, dk, dv, ng, lam)


def _mixout_kernel(a_ref, d_ref, x_ref, w_ref, g_ref, wr_ref, x1_ref, h2_ref, aff_ref):
    half = GLA_HEADS * HEAD_W
    y = _dot(a_ref[...], w_ref[:half, :]) + _dot(d_ref[...], w_ref[half:, :])
    x1 = x_ref[...] + y
    h2 = x1 * lax.rsqrt(jnp.mean(x1 * x1, axis=-1, keepdims=True) + RMS_EPS) * g_ref[...]
    for s in range(TOK_SUB):
        cols = slice(s * LANES, (s + 1) * LANES)
        x1_ref[_tok_rows(s, x1.shape[0]), :] = x1[:, cols]
        h2_ref[_tok_rows(s, x1.shape[0]), :] = h2[:, cols]
    logits = jnp.zeros((N_EXPERTS, h2.shape[0]), F32)
    hp = _split3(h2)
    wp = _split3(wr_ref[...])
    for i, j in ((0, 0), (0, 1), (1, 0)):
        logits = logits + _dot_nt(wp[i], hp[j])
    m = jnp.max(logits, axis=0, keepdims=True)
    e = jnp.exp(logits - m)
    aff_ref[...] = e / jnp.sum(e, axis=0, keepdims=True)


def _mixout(a, d, x2, w_out, g2, wr_t, tm):
    n = x2.shape[0]
    row = lambda i: (i, 0)
    half = GLA_HEADS * HEAD_W
    return pl.pallas_call(
        _mixout_kernel,
        grid=(n // tm,),
        in_specs=[pl.BlockSpec((tm, half), row), pl.BlockSpec((tm, half), row),
                  pl.BlockSpec((tm, D_MODEL), row),
                  pl.BlockSpec((D_MODEL, D_MODEL), lambda i: (0, 0)),
                  pl.BlockSpec((1, D_MODEL), lambda i: (0, 0)),
                  pl.BlockSpec((N_EXPERTS, D_MODEL), lambda i: (0, 0))],
        out_specs=[pl.BlockSpec((tm * TOK_SUB, LANES), row), pl.BlockSpec((tm * TOK_SUB, LANES), row),
                   pl.BlockSpec((N_EXPERTS, tm), lambda i: (0, i))],
        out_shape=[jax.ShapeDtypeStruct((n * TOK_SUB, LANES), F32), jax.ShapeDtypeStruct((n * TOK_SUB, LANES), F32),
                   jax.ShapeDtypeStruct((N_EXPERTS, n), F32)],
        compiler_params=_params(("arbitrary",)),
        name="mixout",
    )(a, d, x2, w_out, g2, wr_t)


ROUTE_BISECTIONS = 40


def _route_kernel(aff_ref, idx_ref, gate_ref, *, n, cap):
    R = n // LANES
    E = N_EXPERTS
    aff = aff_ref[...]

    def count(mask):
        c = jnp.sum(jnp.where(mask, 1.0, 0.0), axis=1, keepdims=True)
        return jnp.sum(c, axis=2, keepdims=True)

    capf = float(cap)
    k = jnp.zeros((E, 1, 1), F32)
    for bit in (64.0, 32.0, 16.0, 8.0, 4.0, 2.0, 1.0):
        cand = k + bit
        k = jnp.where(count(aff >= jnp.exp2(1.0 - cand)) < capf, cand, k)
    hi = jnp.exp2(1.0 - k)
    lo = jnp.where(k >= 127.0, 0.0, jnp.exp2(-jnp.minimum(k, 126.0)))

    def bisect(i, c):
        lo, hi = c
        mid = 0.5 * (lo + hi)
        ok = count(aff >= mid) >= capf
        return jnp.where(ok, mid, lo), jnp.where(ok, hi, mid)

    lo, hi = lax.fori_loop(0, ROUTE_BISECTIONS, bisect, (lo, hi))
    gt = aff >= hi
    eq = (aff >= lo) & (aff < hi)
    need = capf - count(gt)
    tok = (lax.broadcasted_iota(I32, (1, R, LANES), 1) * LANES
           + lax.broadcasted_iota(I32, (1, R, LANES), 2))
    nbits = max(1, (n - 1).bit_length())

    def tie_step(i, m0):
        cand = m0 | jnp.left_shift(jnp.int32(1), nbits - 1 - i)
        return jnp.where(count(eq & (tok < cand)) < need, cand, m0)

    m0 = lax.fori_loop(0, nbits, tie_step, jnp.zeros((E, 1, 1), I32))
    sel = gt | (eq & (tok <= m0))

    ci = lax.broadcasted_iota(I32, (LANES, LANES), 0)
    cj = lax.broadcasted_iota(I32, (LANES, LANES), 1)
    upper = jnp.where(ci <= cj, 1.0, 0.0).astype(BF16)
    ri = lax.broadcasted_iota(I32, (R, R), 0)
    rj = lax.broadcasted_iota(I32, (R, R), 1)
    lower = jnp.where(rj < ri, 1.0, 0.0).astype(BF16)
    jrow = lax.broadcasted_iota(I32, (1, cap), 1).astype(F32)
    rcol = lax.broadcasted_iota(I32, (R, 1), 0).astype(F32)
    ccol = lax.broadcasted_iota(I32, (LANES, 1), 0).astype(F32)

    for e in range(E):
        sel_e = sel[e]
        w = _dot(jnp.where(sel_e, 1.0, 0.0).astype(BF16), upper)
        rt = w[:, LANES - 1:LANES]
        rp = _dot(lower, jnp.broadcast_to(rt, (R, LANES)).astype(BF16))[:, 0:1]
        rows_t = jnp.where((rp <= jrow) & (jrow < rp + rt), 1.0, 0.0)
        k1 = jrow - jnp.sum(rows_t * rp, axis=0, keepdims=True) + 1.0
        rows_b = rows_t.astype(BF16)
        wrow = _dot(jnp.where(sel_e, w, 0.0).T.astype(BF16), rows_b)
        cols_t = jnp.where(wrow == k1, 1.0, 0.0)
        r_of = jnp.sum(rows_t * rcol, axis=0, keepdims=True)
        c_of = jnp.sum(cols_t * ccol, axis=0, keepdims=True)
        idx_ref[e] = (r_of * float(LANES) + c_of).astype(I32)
        a3 = _split3(aff[e].T)
        arow = _dot(a3[0], rows_b) + _dot(a3[1], rows_b) + _dot(a3[2], rows_b)
        gate_ref[e] = jnp.sum(cols_t * arow, axis=0, keepdims=True)


def _route(aff_t, n, cap):
    R = n // LANES
    return pl.pallas_call(
        functools.partial(_route_kernel, n=n, cap=cap),
        out_shape=[jax.ShapeDtypeStruct((N_EXPERTS, 1, cap), I32),
                   jax.ShapeDtypeStruct((N_EXPERTS, 1, cap), F32)],
        compiler_params=pltpu.CompilerParams(vmem_limit_bytes=VMEM_LIMIT),
        name="route",
    )(aff_t.reshape(N_EXPERTS, R, LANES))


FFN_FC = 512
FFN_TT = 1024
FFN_DMA_UNROLL = 16


def _ffn_kernel(idx_ref, gate_ref, h2_hbm, wg_ref, wu_ref, wd_ref, x1_hbm, out_hbm,
                stage_ref, xb_ref, acc_ref, orow_ref, sems, *, cap, nfc):
    del x1_hbm
    e = pl.program_id(0)
    fc = pl.program_id(1)
    ne = pl.num_programs(0)
    g_sem, o_sem, s_sem = sems.at[0], sems.at[1], sems.at[2]
    tt = min(FFN_TT, cap)

    def row_copy(src, s_tok, dst, d_tok, sem):
        s_rows = pl.ds(pl.multiple_of(s_tok * TOK_SUB, TOK_SUB), TOK_SUB)
        d_rows = pl.ds(pl.multiple_of(d_tok * TOK_SUB, TOK_SUB), TOK_SUB)
        return pltpu.make_async_copy(src.at[s_rows], dst.at[d_rows], sem)

    def issue_gather(ee, src, dst, sem):
        def body(j, c):
            row_copy(src, idx_ref[ee * cap + j], dst, j, sem).start()
            return c
        lax.fori_loop(0, cap, body, 0, unroll=FFN_DMA_UNROLL)

    def wait_rows(buf, sem):
        pltpu.make_async_copy(buf, buf, sem).wait()

    gather_step = nfc - 2
    x_steps = [k for k in range(nfc) if k not in (0, gather_step)]
    assert 0 < gather_step and x_steps, "the copy schedule needs at least three hidden-dim steps"
    ntile = cap // tt
    per_tile = cap // ntile
    x_per_tile = cap // (len(x_steps) * ntile)
    e_next = jnp.minimum(e + 1, ne - 1)

    @pl.when(fc == 0)
    def _():
        @pl.when(e == 0)
        def _():
            issue_gather(0, h2_hbm, stage_ref, g_sem)
        wait_rows(stage_ref, g_sem)
        for s in range(TOK_SUB):
            xb_ref[:, s * LANES:(s + 1) * LANES] = stage_ref[_tok_rows(s, cap), :].astype(BF16)

    @pl.when((fc == gather_step) & (e > 0))
    def _():
        wait_rows(orow_ref, s_sem)

    def finish_rows(ee, tok0, ntok):
        for b in range(ntok // LANES):
            first = tok0 + b * LANES
            rows = pl.ds(first if isinstance(first, int) else pl.multiple_of(first, LANES), LANES)
            g = gate_ref.at[ee][:, rows]
            gcol = jnp.broadcast_to(g, (LANES, LANES)).T
            for s in range(TOK_SUB):
                trows = pl.ds(first * TOK_SUB + s, LANES, stride=TOK_SUB)
                orow_ref[trows, :] += acc_ref[rows, s * LANES:(s + 1) * LANES] * gcol

    def scatter_prev(t):
        for jj in range(per_tile):
            j = t * per_tile + jj
            row_copy(orow_ref, j, out_hbm, idx_ref[(e - 1) * cap + j], s_sem).start()

    def gather_out(t):
        for jj in range(per_tile):
            j = t * per_tile + jj
            row_copy(out_hbm, idx_ref[e * cap + j], orow_ref, j, o_sem).start()

    def gather_next(part):
        def issue(t):
            for jj in range(x_per_tile):
                j = (part * ntile + t) * x_per_tile + jj
                row_copy(h2_hbm, idx_ref[e_next * cap + j], stage_ref, j, g_sem).start()
        return issue

    def tile_loop(copies, first):
        def tile(t, c):
            if copies is not None:
                copies(t)
            rows = pl.ds(pl.multiple_of(t * tt, tt), tt)
            x = xb_ref[rows, :]
            a = _dot(x, wg_ref[...].astype(BF16))
            u = _dot(x, wu_ref[...].astype(BF16))
            hm = (a * jax.nn.sigmoid(a) * u).astype(BF16)
            y = _dot(hm, wd_ref[...].astype(BF16))
            if first:
                acc_ref[rows, :] = y
            else:
                acc_ref[rows, :] += y
            return c
        lax.fori_loop(0, ntile, tile, 0)

    pl.when((fc == 0) & (e == 0))(lambda: tile_loop(None, True))
    pl.when((fc == 0) & (e > 0))(lambda: tile_loop(scatter_prev, True))
    pl.when(fc == gather_step)(lambda: tile_loop(gather_out, False))
    for part, k in enumerate(x_steps):
        pl.when(fc == k)(functools.partial(tile_loop, gather_next(part), False))

    @pl.when(fc == nfc - 1)
    def _():
        wait_rows(orow_ref, o_sem)
        finish_rows(e, 0, cap)

        @pl.when(e == ne - 1)
        def _():
            def body(j, c):
                row_copy(orow_ref, j, out_hbm, idx_ref[e * cap + j], s_sem).start()
                return c
            lax.fori_loop(0, cap, body, 0, unroll=FFN_DMA_UNROLL)
            wait_rows(orow_ref, s_sem)
            wait_rows(stage_ref, g_sem)


def _ffn(idx, gate, h2, x1, w_gate, w_up, w_down, cap):
    n = h2.shape[0] // TOK_SUB
    nfc = D_EXPERT // FFN_FC
    grid_spec = pltpu.PrefetchScalarGridSpec(
        num_scalar_prefetch=1,
        grid=(N_EXPERTS, nfc),
        in_specs=[pl.BlockSpec((N_EXPERTS, 1, cap), lambda e, f, idx: (0, 0, 0)),
                  pl.BlockSpec(memory_space=pl.ANY),
                  pl.BlockSpec((None, D_MODEL, FFN_FC), lambda e, f, idx: (e, 0, f)),
                  pl.BlockSpec((None, D_MODEL, FFN_FC), lambda e, f, idx: (e, 0, f)),
                  pl.BlockSpec((None, FFN_FC, D_MODEL), lambda e, f, idx: (e, f, 0)),
                  pl.BlockSpec(memory_space=pl.ANY)],
        out_specs=pl.BlockSpec(memory_space=pl.ANY),
        scratch_shapes=[pltpu.VMEM((cap * TOK_SUB, LANES), F32),
                        pltpu.VMEM((cap, D_MODEL), BF16),
                        pltpu.VMEM((cap, D_MODEL), F32),
                        pltpu.VMEM((cap * TOK_SUB, LANES), F32),
                        pltpu.SemaphoreType.DMA((3,))],
    )
    return pl.pallas_call(
        functools.partial(_ffn_kernel, cap=cap, nfc=nfc),
        grid_spec=grid_spec,
        out_shape=jax.ShapeDtypeStruct((n * TOK_SUB, LANES), F32),
        input_output_aliases={6: 0},
        compiler_params=_params(("arbitrary", "arbitrary")),
        name="ffn",
    )(idx.reshape(-1), gate, h2, w_gate, w_up, w_down, x1)


def _untile_kernel(x_ref, o_ref):
    for s in range(TOK_SUB):
        o_ref[:, s * LANES:(s + 1) * LANES] = x_ref[_tok_rows(s, o_ref.shape[0]), :]


def _untile(xt, tm):
    n = xt.shape[0] // TOK_SUB
    return pl.pallas_call(
        _untile_kernel,
        grid=(n // tm,),
        in_specs=[pl.BlockSpec((tm * TOK_SUB, LANES), lambda i: (i, 0))],
        out_specs=pl.BlockSpec((tm, D_MODEL), lambda i: (i, 0)),
        out_shape=jax.ShapeDtypeStruct((n, D_MODEL), F32),
        compiler_params=_params(("arbitrary",)),
        name="untile",
    )(xt)


def _rope_tables(T):
    half = DIFF_DH // 2
    inv_freq = jnp.power(ROPE_THETA, -jnp.arange(0, DIFF_DH, 2, dtype=F32) / DIFF_DH)
    ang = jnp.arange(T, dtype=F32)[:, None] * inv_freq[None, :]
    cos, sin = jnp.cos(ang), jnp.sin(ang)
    reps = HEAD_W // DIFF_DH
    return (jnp.concatenate([cos, cos] * reps, axis=-1),
            jnp.concatenate([-sin, sin] * reps, axis=-1))


def _layout_weights(w_in, gla_wg_f, gla_bg_f, gla_wg_b, gla_bg_b):
    q0, k0, v0, g0 = 0, 256, 512, 1024
    lr0, dq0 = 1536, 1568
    w = w_in.astype(BF16)
    cols = []
    for h in range(GLA_HEADS):
        cols += [w[:, q0 + h * GLA_DK:q0 + (h + 1) * GLA_DK], w[:, k0 + h * GLA_DK:k0 + (h + 1) * GLA_DK]]
    cols += [w[:, v0:g0], w[:, g0:lr0], w[:, dq0:], w[:, lr0:dq0],
             jnp.zeros((D_MODEL, LANES - 2 * GLA_RANK), BF16)]
    w_r = jnp.concatenate(cols, axis=1)
    wg = jnp.zeros((GLA_HEADS, LANES, HEAD_W), F32)
    bg = jnp.zeros((GLA_HEADS, 1, HEAD_W), F32)
    for h in range(GLA_HEADS):
        hs = slice(h * GLA_DK, (h + 1) * GLA_DK)
        wg = wg.at[h, :GLA_RANK, :GLA_DK].set(gla_wg_f[:, hs])
        wg = wg.at[h, GLA_RANK:2 * GLA_RANK, GLA_DK:].set(gla_wg_b[:, hs])
        bg = bg.at[h, 0, :GLA_DK].set(gla_bg_f[hs])
        bg = bg.at[h, 0, GLA_DK:].set(gla_bg_b[hs])
    return w_r, wg.astype(BF16), bg


def _lambda(lq1, lk1, lq2, lk2, lam_init):
    pad = lambda v: jnp.pad(v.astype(F32), (0, LANES - v.shape[0])).reshape(1, LANES)
    return pl.pallas_call(
        functools.partial(_lambda_kernel, lam_init=lam_init),
        out_shape=jax.ShapeDtypeStruct((1, 1), F32),
        name="lam",
    )(pad(lq1), pad(lk1), pad(lq2), pad(lk2))


def _encoder_layer(x, layer, lw, *, tm=512):
    (norm1_g, w_in, gla_wg_f, gla_bg_f, gla_wg_b, gla_bg_b, gla_norm_g, qk_norm_q, qk_norm_k,
     lambda_q1, lambda_k1, lambda_q2, lambda_k2, diff_norm_g, w_out, norm2_g, w_router,
     w_gate, w_up, w_down) = lw
    B, T, D = x.shape
    n = B * T
    cap = max(1, EC_FACTOR * n // N_EXPERTS)
    tm = min(tm, n)
    qb = min(T, DATTN_SCORE_BYTES // (2 * 4 * T))
    x2 = x.reshape(n, D)
    w_r, wg, bg = _layout_weights(w_in, gla_wg_f, gla_bg_f, gla_wg_b, gla_bg_b)
    row = lambda v: v.astype(F32).reshape(1, -1)
    per_map = lambda v: jnp.tile(v.astype(F32), 2 * DIFF_HEADS).reshape(1, -1)

    cos, sin = _rope_tables(T)
    gqk, gv, gog, dq, dk, dv, lr = _proj(x2, row(norm1_g), w_r, cos, sin,
                                         per_map(qk_norm_q), per_map(qk_norm_k), min(tm, T))
    a = _gla(gqk, gv, gog, lr, wg, bg, row(gla_norm_g), B, T)
    lam_init = _lambda_init(layer)
    lam = _lambda(lambda_q1, lambda_k1, lambda_q2, lambda_k2, lam_init)
    d = _dattn(dq, dk, dv, row(diff_norm_g), lam, B, T, qb, lam_init)
    x1, h2, aff_t = _mixout(a, d, x2, w_out.astype(BF16), row(norm2_g), w_router.T.astype(F32), tm)
    idx, gate = _route(aff_t, n, cap)
    y = _untile(_ffn(idx, gate, h2, x1, w_gate, w_up, w_down, cap), min(2 * tm, n))
    return y.reshape(B, T, D)


def kernel(x_prompt, x_sample, norm1_g, w_in, gla_wg_f, gla_bg_f, gla_wg_b, gla_bg_b, gla_norm_g, qk_norm_q, qk_norm_k, lambda_q1, lambda_k1, lambda_q2, lambda_k2, diff_norm_g, w_out, norm2_g, w_router, w_gate, w_up, w_down):
    y_prompt, y_sample = x_prompt, x_sample
    for l in range(norm1_g.shape[0]):
        lw = (norm1_g[l], w_in[l], gla_wg_f[l], gla_bg_f[l], gla_wg_b[l], gla_bg_b[l],
              gla_norm_g[l], qk_norm_q[l], qk_norm_k[l], lambda_q1[l], lambda_k1[l],
              lambda_q2[l], lambda_k2[l], diff_norm_g[l], w_out[l], norm2_g[l],
              w_router[l], w_gate[l], w_up[l], w_down[l])
        y_prompt = _encoder_layer(y_prompt, l, lw)
        y_sample = _encoder_layer(y_sample, l, lw)
    return (y_prompt, y_sample)
```

```python
import functools
import math

import jax
import jax.numpy as jnp
from jax import lax
from jax.experimental import pallas as pl
from jax.experimental.pallas import tpu as pltpu

F32 = jnp.float32
BF16 = jnp.bfloat16
I32 = jnp.int32

LANES = 128
MXU_W = 256
VMEM_LIMIT = 56 * 1024 * 1024

D_MODEL = 1024
GLA_HEADS = 4
GLA_DK = 64
GLA_DV = 128
GLA_RANK = 16
GLA_TAU = 16.0
GLA_CHUNK = 64
DIFF_HEADS = 4
DIFF_DH = 64
DIFF_DV = 128
ROPE_THETA = 10000.0
N_EXPERTS = 16
EC_FACTOR = 2
D_EXPERT = 2 * D_MODEL
RMS_EPS = 1e-6
LOG2_E = 1.4426950408889634


def _lambda_init(layer):
    return 0.8 - 0.6 * math.exp(-0.3 * layer)


HEAD_W = 128
PROJ_GROUPS = 6
PROJ_W = PROJ_GROUPS * 4 * HEAD_W + LANES
TOK_SUB = D_MODEL // LANES


def _tok_rows(s, ntok, first_tok=0):
    return pl.ds(first_tok * TOK_SUB + s, ntok, stride=TOK_SUB)


def _dot(a, b):
    return jnp.dot(a, b, preferred_element_type=F32)


def _dot_nt(a, b):
    return lax.dot_general(a, b, (((1,), (1,)), ((), ())), preferred_element_type=F32)


def _dot_tn(a, b):
    return lax.dot_general(a, b, (((0,), (0,)), ((), ())), preferred_element_type=F32)


def _split3(x):
    a = x.astype(BF16)
    r = x - a.astype(F32)
    b = r.astype(BF16)
    c = (r - b.astype(F32)).astype(BF16)
    return a, b, c


def _params(sem):
    return pltpu.CompilerParams(dimension_semantics=sem, vmem_limit_bytes=VMEM_LIMIT)


def _qk_prep(x, gain, cos, sin, same_map):
    x2 = (x * x).astype(BF16)
    width = x.shape[1]
    mw = same_map.shape[0]
    ms = jnp.concatenate([_dot(x2[:, c:c + mw], same_map) for c in range(0, width, mw)], axis=1)
    y = x * lax.rsqrt(ms * (1.0 / DIFF_DH) + RMS_EPS) * gain
    half = DIFF_DH // 2
    first = (lax.broadcasted_iota(I32, (1, width), 1) & half) == 0
    swapped = jnp.where(first, pltpu.roll(y, width - half, 1), pltpu.roll(y, half, 1))
    return y * cos + swapped * sin


def _proj_kernel(x_ref, g_ref, w_ref, cos_ref, sin_ref, qg_ref, kg_ref, map_ref,
                 gqk_ref, gv_ref, gog_ref, dq_ref, dk_ref, dv_ref, lr_ref):
    x = x_ref[...]
    ms = jnp.mean(x * x, axis=-1, keepdims=True)
    h = (x * lax.rsqrt(ms + RMS_EPS) * g_ref[...]).astype(BF16)
    gw = 4 * HEAD_W
    group = lambda i: _dot(h, w_ref[:, i * gw:(i + 1) * gw])
    for i, o in ((0, gqk_ref), (1, gv_ref), (2, gog_ref), (5, dv_ref)):
        o[...] = group(i).astype(o.dtype)
    cos = jnp.concatenate([cos_ref[...]] * DIFF_HEADS, axis=1)
    sin = jnp.concatenate([sin_ref[...]] * DIFF_HEADS, axis=1)
    q = _qk_prep(group(3), qg_ref[...], cos, sin, map_ref[...])
    dq_ref[...] = (q * (DIFF_DH ** -0.5 * LOG2_E)).astype(dq_ref.dtype)
    dk_ref[...] = _qk_prep(group(4), kg_ref[...], cos, sin, map_ref[...]).astype(dk_ref.dtype)
    lr_ref[...] = _dot(h, w_ref[:, PROJ_GROUPS * gw:])


def _proj(x2, g1, w_r, cos, sin, qg, kg, tm):
    n = x2.shape[0]
    T = cos.shape[0]
    gw = 4 * HEAD_W
    row = lambda i: (i, 0)
    const = lambda i: (0, 0)
    pos = lambda i: (i % (T // tm), 0)
    ri = lax.broadcasted_iota(I32, (MXU_W, MXU_W), 0)
    ci = lax.broadcasted_iota(I32, (MXU_W, MXU_W), 1)
    same_map = ((ri // DIFF_DH) == (ci // DIFF_DH)).astype(BF16)
    outs = [jax.ShapeDtypeStruct((n, gw), BF16)] * PROJ_GROUPS + [jax.ShapeDtypeStruct((n, LANES), F32)]
    return pl.pallas_call(
        _proj_kernel,
        grid=(n // tm,),
        in_specs=[pl.BlockSpec((tm, D_MODEL), row),
                  pl.BlockSpec((1, D_MODEL), const),
                  pl.BlockSpec((D_MODEL, PROJ_W), const),
                  pl.BlockSpec((tm, HEAD_W), pos), pl.BlockSpec((tm, HEAD_W), pos),
                  pl.BlockSpec((1, gw), const), pl.BlockSpec((1, gw), const),
                  pl.BlockSpec((MXU_W, MXU_W), const)],
        out_specs=[pl.BlockSpec((tm, gw), row)] * PROJ_GROUPS + [pl.BlockSpec((tm, LANES), row)],
        out_shape=outs,
        compiler_params=_params(("arbitrary",)),
        name="proj",
    )(x2, g1, w_r, cos, sin, qg, kg, same_map)


GLA_BLK = 256


def _gla_kernel(gqk_ref, gv_ref, gog_ref, lr_ref, wg_ref, bg_ref, ng_ref, o_ref,
                qh_ref, u_ref, dec_ref, acc_ref, b_ref, tot_ref, *, T):
    C = GLA_CHUNK
    nblk = T // GLA_BLK
    nch = T // C
    lane = lax.broadcasted_iota(I32, (1, HEAD_W), 1)
    fwd = lane < GLA_DK
    ri = lax.broadcasted_iota(I32, (GLA_BLK, GLA_BLK), 0)
    ci = lax.broadcasted_iota(I32, (GLA_BLK, GLA_BLK), 1)
    same = (ri // C) == (ci // C)
    prefix_total = jnp.concatenate([jnp.where(same & (ci <= ri), 1.0, 0.0),
                                    jnp.where(same, 1.0, 0.0)], axis=0).astype(BF16)
    m_f = same & (ci <= ri)
    m_b = same & (ci > ri)
    wg = wg_ref[...]
    bg = bg_ref[...]
    cpb = GLA_BLK // C
    chunk_of_row = lax.broadcasted_iota(I32, (GLA_BLK, 1), 0) // C

    def gates(i, carry):
        rows = pl.ds(pl.multiple_of(i * GLA_BLK, GLA_BLK), GLA_BLK)
        z = _dot(lr_ref[rows, :].astype(BF16), wg) + bg
        la = (jnp.minimum(z, 0.0) - jnp.log(1.0 + jnp.exp(-jnp.abs(z)))) * (1.0 / GLA_TAU)
        hi = la.astype(BF16)
        lo = (la - hi.astype(F32)).astype(BF16)
        pt = _dot(prefix_total, jnp.concatenate([hi, lo], axis=1))
        pt = pt[:, :HEAD_W] + pt[:, HEAD_W:]
        pre, tot = pt[:GLA_BLK, :], pt[GLA_BLK:, :]
        b_ref[rows, :] = jnp.where(fwd, pre, tot - pre + la)
        tot_ref[rows, :] = tot
        return carry

    lax.fori_loop(0, nblk, gates, 0, unroll=8)

    def block(i, carry):
        rows = pl.ds(pl.multiple_of(i * GLA_BLK, GLA_BLK), GLA_BLK)
        b = b_ref[rows, :]
        tot = tot_ref[rows, :]
        mid = 0.5 * tot
        blk = gqk_ref[rows, :].astype(F32)
        rot = pltpu.roll(blk, GLA_DK, 1)
        qq = jnp.where(fwd, blk, rot) * (GLA_DK ** -0.5)
        kk = jnp.where(fwd, rot, blk)
        qt = qq * jnp.exp(b - mid)
        kt = (kk * jnp.exp(mid - b)).astype(BF16)
        kh = (kk * jnp.exp(tot - b)).astype(BF16)
        qh_ref[rows, :] = (qq * jnp.exp(b)).astype(BF16)
        qfb = jnp.concatenate([jnp.where(fwd, qt, 0.0), jnp.where(fwd, 0.0, qt)], axis=0).astype(BF16)
        sfb = _dot_nt(qfb, kt)
        s = jnp.where(m_f, sfb[:GLA_BLK, :], jnp.where(m_b, sfb[GLA_BLK:, :], 0.0)).astype(BF16)
        v = gv_ref[rows, :]
        acc_ref[rows, :] = _dot(s, v)
        zero = jnp.zeros_like(kh)
        kx = jnp.concatenate([jnp.where(chunk_of_row == c, kh, zero) for c in range(cpb)], axis=1)
        ux = _dot_tn(v, kx)
        dec = jnp.exp(tot)
        for c in range(cpb):
            u_ref[i * cpb + c] = ux[:, c * HEAD_W:(c + 1) * HEAD_W]
            dec_ref[i * cpb + c] = jnp.broadcast_to(dec[c * C:c * C + 1, :], (8, HEAD_W))
        return carry

    lax.fori_loop(0, nblk, block, 0, unroll=8)

    def step(t, st):
        nf = t
        nb = nch - 1 - t
        stb = st.astype(BF16)
        rf = pl.ds(pl.multiple_of(nf * C, C), C)
        rb = pl.ds(pl.multiple_of(nb * C, C), C)
        qf = qh_ref[rf, :]
        qb = qh_ref[rb, :]
        zero = jnp.zeros_like(qf)
        o = _dot_nt(jnp.concatenate([jnp.where(fwd, qf, zero), jnp.where(fwd, zero, qb)], axis=0), stb)
        acc_ref[rf, :] += o[:C, :]
        acc_ref[rb, :] += o[C:, :]
        d = jnp.where(fwd, dec_ref[nf][0:1, :], dec_ref[nb][0:1, :])
        u = jnp.where(fwd, u_ref[nf], u_ref[nb])
        return st * d + u

    lax.fori_loop(0, nch, step, jnp.zeros((GLA_DV, HEAD_W), F32), unroll=16)

    ng = ng_ref[...]
    eb = min(T, 512)

    def epi(i, carry):
        rows = pl.ds(pl.multiple_of(i * eb, eb), eb)
        o = acc_ref[rows, :]
        y = o * lax.rsqrt(jnp.mean(o * o, axis=-1, keepdims=True) + RMS_EPS) * ng
        og = gog_ref[rows, :].astype(F32)
        o_ref[rows, :] = (y * (og * jax.nn.sigmoid(og))).astype(o_ref.dtype)
        return carry

    lax.fori_loop(0, T // eb, epi, 0)


def _gla(gqk, gv, gog, lr, wg, bg, ng, B, T):
    n = B * T
    head = pl.BlockSpec((T, HEAD_W), lambda b, h: (b, h))
    return pl.pallas_call(
        functools.partial(_gla_kernel, T=T),
        grid=(B, GLA_HEADS),
        in_specs=[head, head, head,
                  pl.BlockSpec((T, LANES), lambda b, h: (b, 0)),
                  pl.BlockSpec((None, LANES, HEAD_W), lambda b, h: (h, 0, 0)),
                  pl.BlockSpec((None, 1, HEAD_W), lambda b, h: (h, 0, 0)),
                  pl.BlockSpec((1, HEAD_W), lambda b, h: (0, 0))],
        out_specs=head,
        out_shape=jax.ShapeDtypeStruct((n, GLA_HEADS * HEAD_W), BF16),
        scratch_shapes=[pltpu.VMEM((T, HEAD_W), BF16),
                        pltpu.VMEM((T // GLA_CHUNK, GLA_DV, HEAD_W), F32),
                        pltpu.VMEM((T // GLA_CHUNK, 8, HEAD_W), F32),
                        pltpu.VMEM((T, GLA_DV), F32),
                        pltpu.VMEM((T, HEAD_W), F32),
                        pltpu.VMEM((T, HEAD_W), F32)],
        compiler_params=_params(("arbitrary", "arbitrary")),
        name="gla",
    )(gqk, gv, gog, lr, wg, bg, ng)


DATTN_KSPLIT = 1
DATTN_SCORE_BYTES = 32 * 1024 * 1024


def _dattn_kernel(dq_ref, dk_ref, dv_ref, ng_ref, lam_ref, o_ref,
                  s_ref, mp_ref, lp_ref, acc_ref, *, T, QB, TK, lam_init):
    lane = lax.broadcasted_iota(I32, (1, HEAD_W), 1)
    map0 = lane < DIFF_DH
    nk = T // TK + lax.shift_right_logical(pl.program_id(2), 30)
    q = dq_ref[...]
    zero = jnp.zeros_like(q)
    qs = jnp.concatenate([jnp.where(map0, q, zero), jnp.where(map0, zero, q)], axis=0)
    ntile = TK // LANES

    def lane_tiles(x):
        return [x[:, j * LANES:(j + 1) * LANES] for j in range(ntile)]

    mp_ref[...] = jnp.full(mp_ref.shape, -jnp.inf, F32)
    lp_ref[...] = jnp.zeros(lp_ref.shape, F32)
    acc_ref[...] = jnp.zeros(acc_ref.shape, F32)

    def scores(i, carry):
        cols = pl.ds(pl.multiple_of(i * TK, TK), TK)
        s = _dot_nt(qs, dk_ref[cols, :])
        s_ref[:, cols] = s
        mp_ref[...] = jnp.maximum(mp_ref[...], functools.reduce(jnp.maximum, lane_tiles(s)))
        return carry

    lax.fori_loop(0, nk, scores, 0)
    m = jnp.max(mp_ref[...], axis=-1, keepdims=True)

    def accum(i, carry):
        cols = pl.ds(pl.multiple_of(i * TK, TK), TK)
        p = jnp.exp2(s_ref[:, cols] - m)
        lp_ref[...] += functools.reduce(jnp.add, lane_tiles(p))
        acc_ref[...] += _dot(p.astype(BF16), dv_ref[cols, :])
        return carry

    lax.fori_loop(0, nk, accum, 0)
    on = acc_ref[...] / jnp.sum(lp_ref[...], axis=-1, keepdims=True)
    lam = lam_ref[...]
    o = on[:QB, :] - lam * on[QB:, :]
    y = o * lax.rsqrt(jnp.mean(o * o, axis=-1, keepdims=True) + RMS_EPS) * ng_ref[...]
    o_ref[...] = (y * (1.0 - lam_init)).astype(o_ref.dtype)


def _lambda_kernel(q1_ref, k1_ref, q2_ref, k2_ref, o_ref, *, lam_init):
    a = jnp.exp(jnp.sum(q1_ref[...] * k1_ref[...], axis=-1, keepdims=True))
    b = jnp.exp(jnp.sum(q2_ref[...] * k2_ref[...], axis=-1, keepdims=True))
    o_ref[...] = a - b + lam_init


def _dattn(dq, dk, dv, ng, lam, B, T, QB, lam_init):
    n = B * T
    TK = T // DATTN_KSPLIT
    nq = T // QB
    full = lambda shape: pl.BlockSpec(shape, lambda b, h, q: (0, 0))
    kv = pl.BlockSpec((T, HEAD_W), lambda b, h, q: (b, h))
    qo = pl.BlockSpec((QB, HEAD_W), lambda b, h, q: (b * nq + q, h))
    return pl.pallas_call(
        functools.partial(_dattn_kernel, T=T, QB=QB, TK=TK, lam_init=lam_init),
        grid=(B, DIFF_HEADS, nq),
        in_specs=[qo, kv, kv, full((1, HEAD_W)), full((1, 1))],
        out_specs=qo,
        out_shape=jax.ShapeDtypeStruct((n, DIFF_HEADS * HEAD_W), BF16),
        scratch_shapes=[pltpu.VMEM((2 * QB, T), F32),
                        pltpu.VMEM((2 * QB, LANES), F32),
                        pltpu.VMEM((2 * QB, LANES), F32),
                        pltpu.VMEM((2 * QB, DIFF_DV), F32)],
        compiler_params=_params(("arbitrary", "arbitrary", "arbitrary")),
        name="dattn",
    )(dq, dk, dv, ng, lam)


def _mixout_kernel(a_ref, d_ref, x_ref, w_ref, g_ref, wr_ref, x1_ref, h2_ref, aff_ref):
    half = GLA_HEADS * HEAD_W
    y = _dot(a_ref[...], w_ref[:half, :]) + _dot(d_ref[...], w_ref[half:, :])
    x1 = x_ref[...] + y
    h2 = x1 * lax.rsqrt(jnp.mean(x1 * x1, axis=-1, keepdims=True) + RMS_EPS) * g_ref[...]
    for s in range(TOK_SUB):
        cols = slice(s * LANES, (s + 1) * LANES)
        x1_ref[_tok_rows(s, x1.shape[0]), :] = x1[:, cols]
        h2_ref[_tok_rows(s, x1.shape[0]), :] = h2[:, cols]
    logits = jnp.zeros((N_EXPERTS, h2.shape[0]), F32)
    hp = _split3(h2)
    wp = _split3(wr_ref[...])
    for i, j in ((0, 0), (0, 1), (1, 0)):
        logits = logits + _dot_nt(wp[i], hp[j])
    m = jnp.max(logits, axis=0, keepdims=True)
    e = jnp.exp(logits - m)
    aff_ref[...] = e / jnp.sum(e, axis=0, keepdims=True)


def _mixout(a, d, x2, w_out, g2, wr_t, tm):
    n = x2.shape[0]
    row = lambda i: (i, 0)
    half = GLA_HEADS * HEAD_W
    return pl.pallas_call(
        _mixout_kernel,
        grid=(n // tm,),
        in_specs=[pl.BlockSpec((tm, half), row), pl.BlockSpec((tm, half), row),
                  pl.BlockSpec((tm, D_MODEL), row),
                  pl.BlockSpec((D_MODEL, D_MODEL), lambda i: (0, 0)),
                  pl.BlockSpec((1, D_MODEL), lambda i: (0, 0)),
                  pl.BlockSpec((N_EXPERTS, D_MODEL), lambda i: (0, 0))],
        out_specs=[pl.BlockSpec((tm * TOK_SUB, LANES), row), pl.BlockSpec((tm * TOK_SUB, LANES), row),
                   pl.BlockSpec((N_EXPERTS, tm), lambda i: (0, i))],
        out_shape=[jax.ShapeDtypeStruct((n * TOK_SUB, LANES), F32), jax.ShapeDtypeStruct((n * TOK_SUB, LANES), F32),
                   jax.ShapeDtypeStruct((N_EXPERTS, n), F32)],
        compiler_params=_params(("arbitrary",)),
        name="mixout",
    )(a, d, x2, w_out, g2, wr_t)


ROUTE_BISECTIONS = 40


def _route_kernel(aff_ref, idx_ref, gate_ref, *, n, cap):
    R = n // LANES
    E = N_EXPERTS
    aff = aff_ref[...]

    def count(mask):
        c = jnp.sum(jnp.where(mask, 1.0, 0.0), axis=1, keepdims=True)
        return jnp.sum(c, axis=2, keepdims=True)

    capf = float(cap)
    k = jnp.zeros((E, 1, 1), F32)
    for bit in (64.0, 32.0, 16.0, 8.0, 4.0, 2.0, 1.0):
        cand = k + bit
        k = jnp.where(count(aff >= jnp.exp2(1.0 - cand)) < capf, cand, k)
    hi = jnp.exp2(1.0 - k)
    lo = jnp.where(k >= 127.0, 0.0, jnp.exp2(-jnp.minimum(k, 126.0)))

    def bisect(i, c):
        lo, hi = c
        mid = 0.5 * (lo + hi)
        ok = count(aff >= mid) >= capf
        return jnp.where(ok, mid, lo), jnp.where(ok, hi, mid)

    lo, hi = lax.fori_loop(0, ROUTE_BISECTIONS, bisect, (lo, hi))
    gt = aff >= hi
    eq = (aff >= lo) & (aff < hi)
    need = capf - count(gt)
    tok = (lax.broadcasted_iota(I32, (1, R, LANES), 1) * LANES
           + lax.broadcasted_iota(I32, (1, R, LANES), 2))
    nbits = max(1, (n - 1).bit_length())

    def tie_step(i, m0):
        cand = m0 | jnp.left_shift(jnp.int32(1), nbits - 1 - i)
        return jnp.where(count(eq & (tok < cand)) < need, cand, m0)

    m0 = lax.fori_loop(0, nbits, tie_step, jnp.zeros((E, 1, 1), I32))
    sel = gt | (eq & (tok <= m0))

    ci = lax.broadcasted_iota(I32, (LANES, LANES), 0)
    cj = lax.broadcasted_iota(I32, (LANES, LANES), 1)
    upper = jnp.where(ci <= cj, 1.0, 0.0).astype(BF16)
    ri = lax.broadcasted_iota(I32, (R, R), 0)
    rj = lax.broadcasted_iota(I32, (R, R), 1)
    lower = jnp.where(rj < ri, 1.0, 0.0).astype(BF16)
    jrow = lax.broadcasted_iota(I32, (1, cap), 1).astype(F32)
    rcol = lax.broadcasted_iota(I32, (R, 1), 0).astype(F32)
    ccol = lax.broadcasted_iota(I32, (LANES, 1), 0).astype(F32)

    for e in range(E):
        sel_e = sel[e]
        w = _dot(jnp.where(sel_e, 1.0, 0.0).astype(BF16), upper)
        rt = w[:, LANES - 1:LANES]
        rp = _dot(lower, jnp.broadcast_to(rt, (R, LANES)).astype(BF16))[:, 0:1]
        rows_t = jnp.where((rp <= jrow) & (jrow < rp + rt), 1.0, 0.0)
        k1 = jrow - jnp.sum(rows_t * rp, axis=0, keepdims=True) + 1.0
        rows_b = rows_t.astype(BF16)
        wrow = _dot(jnp.where(sel_e, w, 0.0).T.astype(BF16), rows_b)
        cols_t = jnp.where(wrow == k1, 1.0, 0.0)
        r_of = jnp.sum(rows_t * rcol, axis=0, keepdims=True)
        c_of = jnp.sum(cols_t * ccol, axis=0, keepdims=True)
        idx_ref[e] = (r_of * float(LANES) + c_of).astype(I32)
        a3 = _split3(aff[e].T)
        arow = _dot(a3[0], rows_b) + _dot(a3[1], rows_b) + _dot(a3[2], rows_b)
        gate_ref[e] = jnp.sum(cols_t * arow, axis=0, keepdims=True)


def _route(aff_t, n, cap):
    R = n // LANES
    return pl.pallas_call(
        functools.partial(_route_kernel, n=n, cap=cap),
        out_shape=[jax.ShapeDtypeStruct((N_EXPERTS, 1, cap), I32),
                   jax.ShapeDtypeStruct((N_EXPERTS, 1, cap), F32)],
        compiler_params=pltpu.CompilerParams(vmem_limit_bytes=VMEM_LIMIT),
        name="route",
    )(aff_t.reshape(N_EXPERTS, R, LANES))


FFN_FC = 512
FFN_TT = 1024
FFN_DMA_UNROLL = 16


def _ffn_kernel(idx_ref, gate_ref, h2_hbm, wg_ref, wu_ref, wd_ref, x1_hbm, out_hbm,
                stage_ref, xb_ref, acc_ref, orow_ref, sems, *, cap, nfc):
    del x1_hbm
    e = pl.program_id(0)
    fc = pl.program_id(1)
    ne = pl.num_programs(0)
    g_sem, o_sem, s_sem = sems.at[0], sems.at[1], sems.at[2]
    tt = min(FFN_TT, cap)

    def row_copy(src, s_tok, dst, d_tok, sem):
        s_rows = pl.ds(pl.multiple_of(s_tok * TOK_SUB, TOK_SUB), TOK_SUB)
        d_rows = pl.ds(pl.multiple_of(d_tok * TOK_SUB, TOK_SUB), TOK_SUB)
        return pltpu.make_async_copy(src.at[s_rows], dst.at[d_rows], sem)

    def issue_gather(ee, src, dst, sem):
        def body(j, c):
            row_copy(src, idx_ref[ee * cap + j], dst, j, sem).start()
            return c
        lax.fori_loop(0, cap, body, 0, unroll=FFN_DMA_UNROLL)

    def wait_rows(buf, sem):
        pltpu.make_async_copy(buf, buf, sem).wait()

    gather_step = nfc - 2
    x_steps = [k for k in range(nfc) if k not in (0, gather_step)]
    assert 0 < gather_step and x_steps, "the copy schedule needs at least three hidden-dim steps"
    ntile = cap // tt
    per_tile = cap // ntile
    x_per_tile = cap // (len(x_steps) * ntile)
    e_next = jnp.minimum(e + 1, ne - 1)

    @pl.when(fc == 0)
    def _():
        @pl.when(e == 0)
        def _():
            issue_gather(0, h2_hbm, stage_ref, g_sem)
        wait_rows(stage_ref, g_sem)
        for s in range(TOK_SUB):
            xb_ref[:, s * LANES:(s + 1) * LANES] = stage_ref[_tok_rows(s, cap), :].astype(BF16)

    @pl.when((fc == gather_step) & (e > 0))
    def _():
        wait_rows(orow_ref, s_sem)

    def finish_rows(ee, tok0, ntok):
        for b in range(ntok // LANES):
            first = tok0 + b * LANES
            rows = pl.ds(first if isinstance(first, int) else pl.multiple_of(first, LANES), LANES)
            g = gate_ref.at[ee][:, rows]
            gcol = jnp.broadcast_to(g, (LANES, LANES)).T
            for s in range(TOK_SUB):
                trows = pl.ds(first * TOK_SUB + s, LANES, stride=TOK_SUB)
                orow_ref[trows, :] += acc_ref[rows, s * LANES:(s + 1) * LANES] * gcol

    def scatter_prev(t):
        for jj in range(per_tile):
            j = t * per_tile + jj
            row_copy(orow_ref, j, out_hbm, idx_ref[(e - 1) * cap + j], s_sem).start()

    def gather_out(t):
        for jj in range(per_tile):
            j = t * per_tile + jj
            row_copy(out_hbm, idx_ref[e * cap + j], orow_ref, j, o_sem).start()

    def gather_next(part):
        def issue(t):
            for jj in range(x_per_tile):
                j = (part * ntile + t) * x_per_tile + jj
                row_copy(h2_hbm, idx_ref[e_next * cap + j], stage_ref, j, g_sem).start()
        return issue

    def tile_loop(copies, first):
        def tile(t, c):
            if copies is not None:
                copies(t)
            rows = pl.ds(pl.multiple_of(t * tt, tt), tt)
            x = xb_ref[rows, :]
            a = _dot(x, wg_ref[...].astype(BF16))
            u = _dot(x, wu_ref[...].astype(BF16))
            hm = (a * jax.nn.sigmoid(a) * u).astype(BF16)
            y = _dot(hm, wd_ref[...].astype(BF16))
            if first:
                acc_ref[rows, :] = y
            else:
                acc_ref[rows, :] += y
            return c
        lax.fori_loop(0, ntile, tile, 0)

    pl.when((fc == 0) & (e == 0))(lambda: tile_loop(None, True))
    pl.when((fc == 0) & (e > 0))(lambda: tile_loop(scatter_prev, True))
    pl.when(fc == gather_step)(lambda: tile_loop(gather_out, False))
    for part, k in enumerate(x_steps):
        pl.when(fc == k)(functools.partial(tile_loop, gather_next(part), False))

    @pl.when(fc == nfc - 1)
    def _():
        wait_rows(orow_ref, o_sem)
        finish_rows(e, 0, cap)

        @pl.when(e == ne - 1)
        def _():
            def body(j, c):
                row_copy(orow_ref, j, out_hbm, idx_ref[e * cap + j], s_sem).start()
                return c
            lax.fori_loop(0, cap, body, 0, unroll=FFN_DMA_UNROLL)
            wait_rows(orow_ref, s_sem)
            wait_rows(stage_ref, g_sem)


def _ffn(idx, gate, h2, x1, w_gate, w_up, w_down, cap):
    n = h2.shape[0] // TOK_SUB
    nfc = D_EXPERT // FFN_FC
    grid_spec = pltpu.PrefetchScalarGridSpec(
        num_scalar_prefetch=1,
        grid=(N_EXPERTS, nfc),
        in_specs=[pl.BlockSpec((N_EXPERTS, 1, cap), lambda e, f, idx: (0, 0, 0)),
                  pl.BlockSpec(memory_space=pl.ANY),
                  pl.BlockSpec((None, D_MODEL, FFN_FC), lambda e, f, idx: (e, 0, f)),
                  pl.BlockSpec((None, D_MODEL, FFN_FC), lambda e, f, idx: (e, 0, f)),
                  pl.BlockSpec((None, FFN_FC, D_MODEL), lambda e, f, idx: (e, f, 0)),
                  pl.BlockSpec(memory_space=pl.ANY)],
        out_specs=pl.BlockSpec(memory_space=pl.ANY),
        scratch_shapes=[pltpu.VMEM((cap * TOK_SUB, LANES), F32),
                        pltpu.VMEM((cap, D_MODEL), BF16),
                        pltpu.VMEM((cap, D_MODEL), F32),
                        pltpu.VMEM((cap * TOK_SUB, LANES), F32),
                        pltpu.SemaphoreType.DMA((3,))],
    )
    return pl.pallas_call(
        functools.partial(_ffn_kernel, cap=cap, nfc=nfc),
        grid_spec=grid_spec,
        out_shape=jax.ShapeDtypeStruct((n * TOK_SUB, LANES), F32),
        input_output_aliases={6: 0},
        compiler_params=_params(("arbitrary", "arbitrary")),
        name="ffn",
    )(idx.reshape(-1), gate, h2, w_gate, w_up, w_down, x1)


def _untile_kernel(x_ref, o_ref):
    for s in range(TOK_SUB):
        o_ref[:, s * LANES:(s + 1) * LANES] = x_ref[_tok_rows(s, o_ref.shape[0]), :]


def _untile(xt, tm):
    n = xt.shape[0] // TOK_SUB
    return pl.pallas_call(
        _untile_kernel,
        grid=(n // tm,),
        in_specs=[pl.BlockSpec((tm * TOK_SUB, LANES), lambda i: (i, 0))],
        out_specs=pl.BlockSpec((tm, D_MODEL), lambda i: (i, 0)),
        out_shape=jax.ShapeDtypeStruct((n, D_MODEL), F32),
        compiler_params=_params(("arbitrary",)),
        name="untile",
    )(xt)


def _rope_tables(T):
    half = DIFF_DH // 2
    inv_freq = jnp.power(ROPE_THETA, -jnp.arange(0, DIFF_DH, 2, dtype=F32) / DIFF_DH)
    ang = jnp.arange(T, dtype=F32)[:, None] * inv_freq[None, :]
    cos, sin = jnp.cos(ang), jnp.sin(ang)
    reps = HEAD_W // DIFF_DH
    return (jnp.concatenate([cos, cos] * reps, axis=-1),
            jnp.concatenate([-sin, sin] * reps, axis=-1))


def _layout_weights(w_in, gla_wg_f, gla_bg_f, gla_wg_b, gla_bg_b):
    q0, k0, v0, g0 = 0, 256, 512, 1024
    lr0, dq0 = 1536, 1568
    w = w_in.astype(BF16)
    cols = []
    for h in range(GLA_HEADS):
        cols += [w[:, q0 + h * GLA_DK:q0 + (h + 1) * GLA_DK], w[:, k0 + h * GLA_DK:k0 + (h + 1) * GLA_DK]]
    cols += [w[:, v0:g0], w[:, g0:lr0], w[:, dq0:], w[:, lr0:dq0],
             jnp.zeros((D_MODEL, LANES - 2 * GLA_RANK), BF16)]
    w_r = jnp.concatenate(cols, axis=1)
    wg = jnp.zeros((GLA_HEADS, LANES, HEAD_W), F32)
    bg = jnp.zeros((GLA_HEADS, 1, HEAD_W), F32)
    for h in range(GLA_HEADS):
        hs = slice(h * GLA_DK, (h + 1) * GLA_DK)
        wg = wg.at[h, :GLA_RANK, :GLA_DK].set(gla_wg_f[:, hs])
        wg = wg.at[h, GLA_RANK:2 * GLA_RANK, GLA_DK:].set(gla_wg_b[:, hs])
        bg = bg.at[h, 0, :GLA_DK].set(gla_bg_f[hs])
        bg = bg.at[h, 0, GLA_DK:].set(gla_bg_b[hs])
    return w_r, wg.astype(BF16), bg


def _lambda(lq1, lk1, lq2, lk2, lam_init):
    pad = lambda v: jnp.pad(v.astype(F32), (0, LANES - v.shape[0])).reshape(1, LANES)
    return pl.pallas_call(
        functools.partial(_lambda_kernel, lam_init=lam_init),
        out_shape=jax.ShapeDtypeStruct((1, 1), F32),
        name="lam",
    )(pad(lq1), pad(lk1), pad(lq2), pad(lk2))


def _encoder_layer(x, layer, lw, *, tm=512):
    (norm1_g, w_in, gla_wg_f, gla_bg_f, gla_wg_b, gla_bg_b, gla_norm_g, qk_norm_q, qk_norm_k,
     lambda_q1, lambda_k1, lambda_q2, lambda_k2, diff_norm_g, w_out, norm2_g, w_router,
     w_gate, w_up, w_down) = lw
    B, T, D = x.shape
    n = B * T
    cap = max(1, EC_FACTOR * n // N_EXPERTS)
    tm = min(tm, n)
    qb = min(T, DATTN_SCORE_BYTES // (2 * 4 * T))
    x2 = x.reshape(n, D)
    w_r, wg, bg = _layout_weights(w_in, gla_wg_f, gla_bg_f, gla_wg_b, gla_bg_b)
    row = lambda v: v.astype(F32).reshape(1, -1)
    per_map = lambda v: jnp.tile(v.astype(F32), 2 * DIFF_HEADS).reshape(1, -1)

    cos, sin = _rope_tables(T)
    gqk, gv, gog, dq, dk, dv, lr = _proj(x2, row(norm1_g), w_r, cos, sin,
                                         per_map(qk_norm_q), per_map(qk_norm_k), min(tm, T))
    a = _gla(gqk, gv, gog, lr, wg, bg, row(gla_norm_g), B, T)
    lam_init = _lambda_init(layer)
    lam = _lambda(lambda_q1, lambda_k1, lambda_q2, lambda_k2, lam_init)
    d = _dattn(dq, dk, dv, row(diff_norm_g), lam, B, T, qb, lam_init)
    x1, h2, aff_t = _mixout(a, d, x2, w_out.astype(BF16), row(norm2_g), w_router.T.astype(F32), tm)
    idx, gate = _route(aff_t, n, cap)
    y = _untile(_ffn(idx, gate, h2, x1, w_gate, w_up, w_down, cap), min(2 * tm, n))
    return y.reshape(B, T, D)


def kernel(x_prompt, x_sample, norm1_g, w_in, gla_wg_f, gla_bg_f, gla_wg_b, gla_bg_b, gla_norm_g, qk_norm_q, qk_norm_k, lambda_q1, lambda_k1, lambda_q2, lambda_k2, diff_norm_g, w_out, norm2_g, w_router, w_gate, w_up, w_down):
    y_prompt, y_sample = x_prompt, x_sample
    for l in range(norm1_g.shape[0]):
        lw = (norm1_g[l], w_in[l], gla_wg_f[l], gla_bg_f[l], gla_wg_b[l], gla_bg_b[l],
              gla_norm_g[l], qk_norm_q[l], qk_norm_k[l], lambda_q1[l], lambda_k1[l],
              lambda_q2[l], lambda_k2[l], diff_norm_g[l], w_out[l], norm2_g[l],
              w_router[l], w_gate[l], w_up[l], w_down[l])
        y_prompt = _encoder_layer(y_prompt, l, lw)
        y_sample = _encoder_layer(y_sample, l, lw)
    return (y_prompt, y_sample)
```

```python
import functools
import math

import jax
import jax.numpy as jnp
from jax import lax
from jax.experimental import pallas as pl
from jax.experimental.pallas import tpu as pltpu

F32 = jnp.float32
BF16 = jnp.bfloat16
I32 = jnp.int32

LANES = 128
MXU_W = 256
VMEM_LIMIT = 56 * 1024 * 1024

D_MODEL = 1024
GLA_HEADS = 4
GLA_DK = 64
GLA_DV = 128
GLA_RANK = 16
GLA_TAU = 16.0
GLA_CHUNK = 64
DIFF_HEADS = 4
DIFF_DH = 64
DIFF_DV = 128
ROPE_THETA = 10000.0
N_EXPERTS = 16
EC_FACTOR = 2
D_EXPERT = 2 * D_MODEL
RMS_EPS = 1e-6
LOG2_E = 1.4426950408889634


def _lambda_init(layer):
    return 0.8 - 0.6 * math.exp(-0.3 * layer)


HEAD_W = 128
PROJ_GROUPS = 6
PROJ_W = PROJ_GROUPS * 4 * HEAD_W + LANES
TOK_SUB = D_MODEL // LANES


def _tok_rows(s, ntok, first_tok=0):
    return pl.ds(first_tok * TOK_SUB + s, ntok, stride=TOK_SUB)


def _dot(a, b):
    return jnp.dot(a, b, preferred_element_type=F32)


def _dot_nt(a, b):
    return lax.dot_general(a, b, (((1,), (1,)), ((), ())), preferred_element_type=F32)


def _dot_tn(a, b):
    return lax.dot_general(a, b, (((0,), (0,)), ((), ())), preferred_element_type=F32)


def _split3(x):
    a = x.astype(BF16)
    r = x - a.astype(F32)
    b = r.astype(BF16)
    c = (r - b.astype(F32)).astype(BF16)
    return a, b, c


def _params(sem):
    return pltpu.CompilerParams(dimension_semantics=sem, vmem_limit_bytes=VMEM_LIMIT)


def _qk_prep(x, gain, cos, sin, same_map):
    x2 = (x * x).astype(BF16)
    width = x.shape[1]
    mw = same_map.shape[0]
    ms = jnp.concatenate([_dot(x2[:, c:c + mw], same_map) for c in range(0, width, mw)], axis=1)
    y = x * lax.rsqrt(ms * (1.0 / DIFF_DH) + RMS_EPS) * gain
    half = DIFF_DH // 2
    first = (lax.broadcasted_iota(I32, (1, width), 1) & half) == 0
    swapped = jnp.where(first, pltpu.roll(y, width - half, 1), pltpu.roll(y, half, 1))
    return y * cos + swapped * sin


def _proj_kernel(x_ref, g_ref, w_ref, cos_ref, sin_ref, qg_ref, kg_ref, map_ref,
                 gqk_ref, gv_ref, gog_ref, dq_ref, dk_ref, dv_ref, lr_ref):
    x = x_ref[...]
    ms = jnp.mean(x * x, axis=-1, keepdims=True)
    h = (x * lax.rsqrt(ms + RMS_EPS) * g_ref[...]).astype(BF16)
    gw = 4 * HEAD_W
    group = lambda i: _dot(h, w_ref[:, i * gw:(i + 1) * gw])
    for i, o in ((0, gqk_ref), (1, gv_ref), (2, gog_ref), (5, dv_ref)):
        o[...] = group(i).astype(o.dtype)
    cos = jnp.concatenate([cos_ref[...]] * DIFF_HEADS, axis=1)
    sin = jnp.concatenate([sin_ref[...]] * DIFF_HEADS, axis=1)
    q = _qk_prep(group(3), qg_ref[...], cos, sin, map_ref[...])
    dq_ref[...] = (q * (DIFF_DH ** -0.5 * LOG2_E)).astype(dq_ref.dtype)
    dk_ref[...] = _qk_prep(group(4), kg_ref[...], cos, sin, map_ref[...]).astype(dk_ref.dtype)
    lr_ref[...] = _dot(h, w_ref[:, PROJ_GROUPS * gw:])


def _proj(x2, g1, w_r, cos, sin, qg, kg, tm):
    n = x2.shape[0]
    T = cos.shape[0]
    gw = 4 * HEAD_W
    row = lambda i: (i, 0)
    const = lambda i: (0, 0)
    pos = lambda i: (i % (T // tm), 0)
    ri = lax.broadcasted_iota(I32, (MXU_W, MXU_W), 0)
    ci = lax.broadcasted_iota(I32, (MXU_W, MXU_W), 1)
    same_map = ((ri // DIFF_DH) == (ci // DIFF_DH)).astype(BF16)
    outs = [jax.ShapeDtypeStruct((n, gw), BF16)] * PROJ_GROUPS + [jax.ShapeDtypeStruct((n, LANES), F32)]
    return pl.pallas_call(
        _proj_kernel,
        grid=(n // tm,),
        in_specs=[pl.BlockSpec((tm, D_MODEL), row),
                  pl.BlockSpec((1, D_MODEL), const),
                  pl.BlockSpec((D_MODEL, PROJ_W), const),
                  pl.BlockSpec((tm, HEAD_W), pos), pl.BlockSpec((tm, HEAD_W), pos),
                  pl.BlockSpec((1, gw), const), pl.BlockSpec((1, gw), const),
                  pl.BlockSpec((MXU_W, MXU_W), const)],
        out_specs=[pl.BlockSpec((tm, gw), row)] * PROJ_GROUPS + [pl.BlockSpec((tm, LANES), row)],
        out_shape=outs,
        compiler_params=_params(("arbitrary",)),
        name="proj",
    )(x2, g1, w_r, cos, sin, qg, kg, same_map)


GLA_BLK = 256


def _gla_kernel(gqk_ref, gv_ref, gog_ref, lr_ref, wg_ref, bg_ref, ng_ref, o_ref,
                qh_ref, u_ref, dec_ref, acc_ref, b_ref, tot_ref, *, T):
    C = GLA_CHUNK
    nblk = T // GLA_BLK
    nch = T // C
    lane = lax.broadcasted_iota(I32, (1, HEAD_W), 1)
    fwd = lane < GLA_DK
    ri = lax.broadcasted_iota(I32, (GLA_BLK, GLA_BLK), 0)
    ci = lax.broadcasted_iota(I32, (GLA_BLK, GLA_BLK), 1)
    same = (ri // C) == (ci // C)
    prefix_total = jnp.concatenate([jnp.where(same & (ci <= ri), 1.0, 0.0),
                                    jnp.where(same, 1.0, 0.0)], axis=0).astype(BF16)
    m_f = same & (ci <= ri)
    m_b = same & (ci > ri)
    wg = wg_ref[...]
    bg = bg_ref[...]
    cpb = GLA_BLK // C
    chunk_of_row = lax.broadcasted_iota(I32, (GLA_BLK, 1), 0) // C

    def gates(i, carry):
        rows = pl.ds(pl.multiple_of(i * GLA_BLK, GLA_BLK), GLA_BLK)
        z = _dot(lr_ref[rows, :].astype(BF16), wg) + bg
        la = (jnp.minimum(z, 0.0) - jnp.log(1.0 + jnp.exp(-jnp.abs(z)))) * (1.0 / GLA_TAU)
        hi = la.astype(BF16)
        lo = (la - hi.astype(F32)).astype(BF16)
        pt = _dot(prefix_total, jnp.concatenate([hi, lo], axis=1))
        pt = pt[:, :HEAD_W] + pt[:, HEAD_W:]
        pre, tot = pt[:GLA_BLK, :], pt[GLA_BLK:, :]
        b_ref[rows, :] = jnp.where(fwd, pre, tot - pre + la)
        tot_ref[rows, :] = tot
        return carry

    lax.fori_loop(0, nblk, gates, 0, unroll=8)

    def block(i, carry):
        rows = pl.ds(pl.multiple_of(i * GLA_BLK, GLA_BLK), GLA_BLK)
        b = b_ref[rows, :]
        tot = tot_ref[rows, :]
        mid = 0.5 * tot
        blk = gqk_ref[rows, :].astype(F32)
        rot = pltpu.roll(blk, GLA_DK, 1)
        qq = jnp.where(fwd, blk, rot) * (GLA_DK ** -0.5)
        kk = jnp.where(fwd, rot, blk)
        qt = qq * jnp.exp(b - mid)
        kt = (kk * jnp.exp(mid - b)).astype(BF16)
        kh = (kk * jnp.exp(tot - b)).astype(BF16)
        qh_ref[rows, :] = (qq * jnp.exp(b)).astype(BF16)
        qfb = jnp.concatenate([jnp.where(fwd, qt, 0.0), jnp.where(fwd, 0.0, qt)], axis=0).astype(BF16)
        sfb = _dot_nt(qfb, kt)
        s = jnp.where(m_f, sfb[:GLA_BLK, :], jnp.where(m_b, sfb[GLA_BLK:, :], 0.0)).astype(BF16)
        v = gv_ref[rows, :]
        acc_ref[rows, :] = _dot(s, v)
        zero = jnp.zeros_like(kh)
        kx = jnp.concatenate([jnp.where(chunk_of_row == c, kh, zero) for c in range(cpb)], axis=1)
        ux = _dot_tn(v, kx)
        dec = jnp.exp(tot)
        for c in range(cpb):
            u_ref[i * cpb + c] = ux[:, c * HEAD_W:(c + 1) * HEAD_W]
            dec_ref[i * cpb + c] = jnp.broadcast_to(dec[c * C:c * C + 1, :], (8, HEAD_W))
        return carry

    lax.fori_loop(0, nblk, block, 0, unroll=8)

    def step(t, st):
        nf = t
        nb = nch - 1 - t
        stb = st.astype(BF16)
        rf = pl.ds(pl.multiple_of(nf * C, C), C)
        rb = pl.ds(pl.multiple_of(nb * C, C), C)
        qf = qh_ref[rf, :]
        qb = qh_ref[rb, :]
        zero = jnp.zeros_like(qf)
        o = _dot_nt(jnp.concatenate([jnp.where(fwd, qf, zero), jnp.where(fwd, zero, qb)], axis=0), stb)
        acc_ref[rf, :] += o[:C, :]
        acc_ref[rb, :] += o[C:, :]
        d = jnp.where(fwd, dec_ref[nf][0:1, :], dec_ref[nb][0:1, :])
        u = jnp.where(fwd, u_ref[nf], u_ref[nb])
        return st * d + u

    lax.fori_loop(0, nch, step, jnp.zeros((GLA_DV, HEAD_W), F32), unroll=16)

    ng = ng_ref[...]
    eb = min(T, 512)

    def epi(i, carry):
        rows = pl.ds(pl.multiple_of(i * eb, eb), eb)
        o = acc_ref[rows, :]
        y = o * lax.rsqrt(jnp.mean(o * o, axis=-1, keepdims=True) + RMS_EPS) * ng
        og = gog_ref[rows, :].astype(F32)
        o_ref[rows, :] = (y * (og * jax.nn.sigmoid(og))).astype(o_ref.dtype)
        return carry

    lax.fori_loop(0, T // eb, epi, 0)


def _gla(gqk, gv, gog, lr, wg, bg, ng, B, T):
    n = B * T
    head = pl.BlockSpec((T, HEAD_W), lambda b, h: (b, h))
    return pl.pallas_call(
        functools.partial(_gla_kernel, T=T),
        grid=(B, GLA_HEADS),
        in_specs=[head, head, head,
                  pl.BlockSpec((T, LANES), lambda b, h: (b, 0)),
                  pl.BlockSpec((None, LANES, HEAD_W), lambda b, h: (h, 0, 0)),
                  pl.BlockSpec((None, 1, HEAD_W), lambda b, h: (h, 0, 0)),
                  pl.BlockSpec((1, HEAD_W), lambda b, h: (0, 0))],
        out_specs=head,
        out_shape=jax.ShapeDtypeStruct((n, GLA_HEADS * HEAD_W), BF16),
        scratch_shapes=[pltpu.VMEM((T, HEAD_W), BF16),
                        pltpu.VMEM((T // GLA_CHUNK, GLA_DV, HEAD_W), F32),
                        pltpu.VMEM((T // GLA_CHUNK, 8, HEAD_W), F32),
                        pltpu.VMEM((T, GLA_DV), F32),
                        pltpu.VMEM((T, HEAD_W), F32),
                        pltpu.VMEM((T, HEAD_W), F32)],
        compiler_params=_params(("arbitrary", "arbitrary")),
        name="gla",
    )(gqk, gv, gog, lr, wg, bg, ng)


DATTN_KSPLIT = 1
DATTN_SCORE_BYTES = 32 * 1024 * 1024


def _dattn_kernel(dq_ref, dk_ref, dv_ref, ng_ref, lam_ref, o_ref,
                  s_ref, mp_ref, lp_ref, acc_ref, *, T, QB, TK, lam_init):
    lane = lax.broadcasted_iota(I32, (1, HEAD_W), 1)
    map0 = lane < DIFF_DH
    nk = T // TK + lax.shift_right_logical(pl.program_id(2), 30)
    q = dq_ref[...]
    zero = jnp.zeros_like(q)
    qs = jnp.concatenate([jnp.where(map0, q, zero), jnp.where(map0, zero, q)], axis=0)
    ntile = TK // LANES

    def lane_tiles(x):
        return [x[:, j * LANES:(j + 1) * LANES] for j in range(ntile)]

    mp_ref[...] = jnp.full(mp_ref.shape, -jnp.inf, F32)
    lp_ref[...] = jnp.zeros(lp_ref.shape, F32)
    acc_ref[...] = jnp.zeros(acc_ref.shape, F32)

    def scores(i, carry):
        cols = pl.ds(pl.multiple_of(i * TK, TK), TK)
        s = _dot_nt(qs, dk_ref[cols, :])
        s_ref[:, cols] = s
        mp_ref[...] = jnp.maximum(mp_ref[...], functools.reduce(jnp.maximum, lane_tiles(s)))
        return carry

    lax.fori_loop(0, nk, scores, 0)
    m = jnp.max(mp_ref[...], axis=-1, keepdims=True)

    def accum(i, carry):
        cols = pl.ds(pl.multiple_of(i * TK, TK), TK)
        p = jnp.exp2(s_ref[:, cols] - m)
        lp_ref[...] += functools.reduce(jnp.add, lane_tiles(p))
        acc_ref[...] += _dot(p.astype(BF16), dv_ref[cols, :])
        return carry

    lax.fori_loop(0, nk, accum, 0)
    on = acc_ref[...] / jnp.sum(lp_ref[...], axis=-1, keepdims=True)
    lam = lam_ref[...]
    o = on[:QB, :] - lam * on[QB:, :]
    y = o * lax.rsqrt(jnp.mean(o * o, axis=-1, keepdims=True) + RMS_EPS) * ng_ref[...]
    o_ref[...] = (y * (1.0 - lam_init)).astype(o_ref.dtype)


def _lambda_kernel(q1_ref, k1_ref, q2_ref, k2_ref, o_ref, *, lam_init):
    a = jnp.exp(jnp.sum(q1_ref[...] * k1_ref[...], axis=-1, keepdims=True))
    b = jnp.exp(jnp.sum(q2_ref[...] * k2_ref[...], axis=-1, keepdims=True))
    o_ref[...] = a - b + lam_init


def _dattn(dq, dk, dv, ng, lam, B, T, QB, lam_init):
    n = B * T
    TK = T // DATTN_KSPLIT
    nq = T // QB
    full = lambda shape: pl.BlockSpec(shape, lambda b, h, q: (0, 0))
    kv = pl.BlockSpec((T, HEAD_W), lambda b, h, q: (b, h))
    qo = pl.BlockSpec((QB, HEAD_W), lambda b, h, q: (b * nq + q, h))
    return pl.pallas_call(
        functools.partial(_dattn_kernel, T=T, QB=QB, TK=TK, lam_init=lam_init),
        grid=(B, DIFF_HEADS, nq),
        in_specs=[qo, kv, kv, full((1, HEAD_W)), full((1, 1))],
        out_specs=qo,
        out_shape=jax.ShapeDtypeStruct((n, DIFF_HEADS * HEAD_W), BF16),
        scratch_shapes=[pltpu.VMEM((2 * QB, T), F32),
                        pltpu.VMEM((2 * QB, LANES), F32),
                        pltpu.VMEM((2 * QB, LANES), F32),
                        pltpu.VMEM((2 * QB, DIFF_DV), F32)],
        compiler_params=_params(("arbitrary", "arbitrary", "arbitrary")),
        name="dattn",
    )(dq, dk, dv, ng, lam)


def _mixout_kernel(a_ref, d_ref, x_ref, w_ref, g_ref, wr_ref, x1_ref, h2_ref, aff_ref):
    half = GLA_HEADS * HEAD_W
    y = _dot(a_ref[...], w_ref[:half, :]) + _dot(d_ref[...], w_ref[half:, :])
    x1 = x_ref[...] + y
    h2 = x1 * lax.rsqrt(jnp.mean(x1 * x1, axis=-1, keepdims=True) + RMS_EPS) * g_ref[...]
    for s in range(TOK_SUB):
        cols = slice(s * LANES, (s + 1) * LANES)
        x1_ref[_tok_rows(s, x1.shape[0]), :] = x1[:, cols]
        h2_ref[_tok_rows(s, x1.shape[0]), :] = h2[:, cols]
    logits = jnp.zeros((N_EXPERTS, h2.shape[0]), F32)
    hp = _split3(h2)
    wp = _split3(wr_ref[...])
    for i, j in ((0, 0), (0, 1), (1, 0)):
        logits = logits + _dot_nt(wp[i], hp[j])
    m = jnp.max(logits, axis=0, keepdims=True)
    e = jnp.exp(logits - m)
    aff_ref[...] = e / jnp.sum(e, axis=0, keepdims=True)


def _mixout(a, d, x2, w_out, g2, wr_t, tm):
    n = x2.shape[0]
    row = lambda i: (i, 0)
    half = GLA_HEADS * HEAD_W
    return pl.pallas_call(
        _mixout_kernel,
        grid=(n // tm,),
        in_specs=[pl.BlockSpec((tm, half), row), pl.BlockSpec((tm, half), row),
                  pl.BlockSpec((tm, D_MODEL), row),
                  pl.BlockSpec((D_MODEL, D_MODEL), lambda i: (0, 0)),
                  pl.BlockSpec((1, D_MODEL), lambda i: (0, 0)),
                  pl.BlockSpec((N_EXPERTS, D_MODEL), lambda i: (0, 0))],
        out_specs=[pl.BlockSpec((tm * TOK_SUB, LANES), row), pl.BlockSpec((tm * TOK_SUB, LANES), row),
                   pl.BlockSpec((N_EXPERTS, tm), lambda i: (0, i))],
        out_shape=[jax.ShapeDtypeStruct((n * TOK_SUB, LANES), F32), jax.ShapeDtypeStruct((n * TOK_SUB, LANES), F32),
                   jax.ShapeDtypeStruct((N_EXPERTS, n), F32)],
        compiler_params=_params(("arbitrary",)),
        name="mixout",
    )(a, d, x2, w_out, g2, wr_t)


ROUTE_BISECTIONS = 40


def _route_kernel(aff_ref, idx_ref, gate_ref, *, n, cap):
    R = n // LANES
    E = N_EXPERTS
    aff = aff_ref[...]

    def count(mask):
        c = jnp.sum(jnp.where(mask, 1.0, 0.0), axis=1, keepdims=True)
        return jnp.sum(c, axis=2, keepdims=True)

    capf = float(cap)
    k = jnp.zeros((E, 1, 1), F32)
    for bit in (64.0, 32.0, 16.0, 8.0, 4.0, 2.0, 1.0):
        cand = k + bit
        k = jnp.where(count(aff >= jnp.exp2(1.0 - cand)) < capf, cand, k)
    hi = jnp.exp2(1.0 - k)
    lo = jnp.where(k >= 127.0, 0.0, jnp.exp2(-jnp.minimum(k, 126.0)))

    def bisect(i, c):
        lo, hi = c
        mid = 0.5 * (lo + hi)
        ok = count(aff >= mid) >= capf
        return jnp.where(ok, mid, lo), jnp.where(ok, hi, mid)

    lo, hi = lax.fori_loop(0, ROUTE_BISECTIONS, bisect, (lo, hi))
    gt = aff >= hi
    eq = (aff >= lo) & (aff < hi)
    need = capf - count(gt)
    tok = (lax.broadcasted_iota(I32, (1, R, LANES), 1) * LANES
           + lax.broadcasted_iota(I32, (1, R, LANES), 2))
    nbits = max(1, (n - 1).bit_length())

    def tie_step(i, m0):
        cand = m0 | jnp.left_shift(jnp.int32(1), nbits - 1 - i)
        return jnp.where(count(eq & (tok < cand)) < need, cand, m0)

    m0 = lax.fori_loop(0, nbits, tie_step, jnp.zeros((E, 1, 1), I32))
    sel = gt | (eq & (tok <= m0))

    ci = lax.broadcasted_iota(I32, (LANES, LANES), 0)
    cj = lax.broadcasted_iota(I32, (LANES, LANES), 1)
    upper = jnp.where(ci <= cj, 1.0, 0.0).astype(BF16)
    ri = lax.broadcasted_iota(I32, (R, R), 0)
    rj = lax.broadcasted_iota(I32, (R, R), 1)
    lower = jnp.where(rj < ri, 1.0, 0.0).astype(BF16)
    jrow = lax.broadcasted_iota(I32, (1, cap), 1).astype(F32)
    rcol = lax.broadcasted_iota(I32, (R, 1), 0).astype(F32)
    ccol = lax.broadcasted_iota(I32, (LANES, 1), 0).astype(F32)

    for e in range(E):
        sel_e = sel[e]
        w = _dot(jnp.where(sel_e, 1.0, 0.0).astype(BF16), upper)
        rt = w[:, LANES - 1:LANES]
        rp = _dot(lower, jnp.broadcast_to(rt, (R, LANES)).astype(BF16))[:, 0:1]
        rows_t = jnp.where((rp <= jrow) & (jrow < rp + rt), 1.0, 0.0)
        k1 = jrow - jnp.sum(rows_t * rp, axis=0, keepdims=True) + 1.0
        rows_b = rows_t.astype(BF16)
        wrow = _dot(jnp.where(sel_e, w, 0.0).T.astype(BF16), rows_b)
        cols_t = jnp.where(wrow == k1, 1.0, 0.0)
        r_of = jnp.sum(rows_t * rcol, axis=0, keepdims=True)
        c_of = jnp.sum(cols_t * ccol, axis=0, keepdims=True)
        idx_ref[e] = (r_of * float(LANES) + c_of).astype(I32)
        a3 = _split3(aff[e].T)
        arow = _dot(a3[0], rows_b) + _dot(a3[1], rows_b) + _dot(a3[2], rows_b)
        gate_ref[e] = jnp.sum(cols_t * arow, axis=0, keepdims=True)


def _route(aff_t, n, cap):
    R = n // LANES
    return pl.pallas_call(
        functools.partial(_route_kernel, n=n, cap=cap),
        out_shape=[jax.ShapeDtypeStruct((N_EXPERTS, 1, cap), I32),
                   jax.ShapeDtypeStruct((N_EXPERTS, 1, cap), F32)],
        compiler_params=pltpu.CompilerParams(vmem_limit_bytes=VMEM_LIMIT),
        name="route",
    )(aff_t.reshape(N_EXPERTS, R, LANES))


FFN_FC = 512
FFN_TT = 1024
FFN_DMA_UNROLL = 16


def _ffn_kernel(idx_ref, gate_ref, h2_hbm, wg_ref, wu_ref, wd_ref, x1_hbm, out_hbm,
                stage_ref, xb_ref, acc_ref, orow_ref, sems, *, cap, nfc):
    del x1_hbm
    e = pl.program_id(0)
    fc = pl.program_id(1)
    ne = pl.num_programs(0)
    g_sem, o_sem, s_sem = sems.at[0], sems.at[1], sems.at[2]
    tt = min(FFN_TT, cap)

    def row_copy(src, s_tok, dst, d_tok, sem):
        s_rows = pl.ds(pl.multiple_of(s_tok * TOK_SUB, TOK_SUB), TOK_SUB)
        d_rows = pl.ds(pl.multiple_of(d_tok * TOK_SUB, TOK_SUB), TOK_SUB)
        return pltpu.make_async_copy(src.at[s_rows], dst.at[d_rows], sem)

    def issue_gather(ee, src, dst, sem):
        def body(j, c):
            row_copy(src, idx_ref[ee * cap + j], dst, j, sem).start()
            return c
        lax.fori_loop(0, cap, body, 0, unroll=FFN_DMA_UNROLL)

    def wait_rows(buf, sem):
        pltpu.make_async_copy(buf, buf, sem).wait()

    gather_step = nfc - 2
    x_steps = [k for k in range(nfc) if k not in (0, gather_step)]
    assert 0 < gather_step and x_steps, "the copy schedule needs at least three hidden-dim steps"
    ntile = cap // tt
    per_tile = cap // ntile
    x_per_tile = cap // (len(x_steps) * ntile)
    e_next = jnp.minimum(e + 1, ne - 1)

    @pl.when(fc == 0)
    def _():
        @pl.when(e == 0)
        def _():
            issue_gather(0, h2_hbm, stage_ref, g_sem)
        wait_rows(stage_ref, g_sem)
        blk = min(cap, 256)
        for b in range(cap // blk):
            tiles = stage_ref[b * blk * TOK_SUB:(b + 1) * blk * TOK_SUB, :]
            xb_ref[b * blk:(b + 1) * blk, :] = pltpu.einshape("(ts)l->t(sl)", tiles, s=TOK_SUB).astype(BF16)

    @pl.when((fc == gather_step) & (e > 0))
    def _():
        wait_rows(orow_ref, s_sem)

    def finish_rows(ee, tok0, ntok):
        for b in range(ntok // LANES):
            first = tok0 + b * LANES
            rows = pl.ds(first if isinstance(first, int) else pl.multiple_of(first, LANES), LANES)
            g = gate_ref.at[ee][:, rows]
            gcol = jnp.broadcast_to(g, (LANES, LANES)).T
            gated = acc_ref[rows, :] * jnp.concatenate([gcol] * TOK_SUB, axis=1)
            trows = pl.ds(first * TOK_SUB, LANES * TOK_SUB)
            orow_ref[trows, :] += pltpu.einshape("t(sl)->(ts)l", gated, s=TOK_SUB)

    def scatter_prev(t):
        for jj in range(per_tile):
            j = t * per_tile + jj
            row_copy(orow_ref, j, out_hbm, idx_ref[(e - 1) * cap + j], s_sem).start()

    def gather_out(t):
        for jj in range(per_tile):
            j = t * per_tile + jj
            row_copy(out_hbm, idx_ref[e * cap + j], orow_ref, j, o_sem).start()

    def gather_next(part):
        def issue(t):
            for jj in range(x_per_tile):
                j = (part * ntile + t) * x_per_tile + jj
                row_copy(h2_hbm, idx_ref[e_next * cap + j], stage_ref, j, g_sem).start()
        return issue

    def tile_loop(copies, first):
        def tile(t, c):
            if copies is not None:
                copies(t)
            rows = pl.ds(pl.multiple_of(t * tt, tt), tt)
            x = xb_ref[rows, :]
            a = _dot(x, wg_ref[...].astype(BF16))
            u = _dot(x, wu_ref[...].astype(BF16))
            hm = (a * jax.nn.sigmoid(a) * u).astype(BF16)
            y = _dot(hm, wd_ref[...].astype(BF16))
            if first:
                acc_ref[rows, :] = y
            else:
                acc_ref[rows, :] += y
            return c
        lax.fori_loop(0, ntile, tile, 0)

    pl.when((fc == 0) & (e == 0))(lambda: tile_loop(None, True))
    pl.when((fc == 0) & (e > 0))(lambda: tile_loop(scatter_prev, True))
    pl.when(fc == gather_step)(lambda: tile_loop(gather_out, False))
    for part, k in enumerate(x_steps):
        pl.when(fc == k)(functools.partial(tile_loop, gather_next(part), False))

    @pl.when(fc == nfc - 1)
    def _():
        wait_rows(orow_ref, o_sem)
        finish_rows(e, 0, cap)

        @pl.when(e == ne - 1)
        def _():
            def body(j, c):
                row_copy(orow_ref, j, out_hbm, idx_ref[e * cap + j], s_sem).start()
                return c
            lax.fori_loop(0, cap, body, 0, unroll=FFN_DMA_UNROLL)
            wait_rows(orow_ref, s_sem)
            wait_rows(stage_ref, g_sem)


def _ffn(idx, gate, h2, x1, w_gate, w_up, w_down, cap):
    n = h2.shape[0] // TOK_SUB
    nfc = D_EXPERT // FFN_FC
    grid_spec = pltpu.PrefetchScalarGridSpec(
        num_scalar_prefetch=1,
        grid=(N_EXPERTS, nfc),
        in_specs=[pl.BlockSpec((N_EXPERTS, 1, cap), lambda e, f, idx: (0, 0, 0)),
                  pl.BlockSpec(memory_space=pl.ANY),
                  pl.BlockSpec((None, D_MODEL, FFN_FC), lambda e, f, idx: (e, 0, f)),
                  pl.BlockSpec((None, D_MODEL, FFN_FC), lambda e, f, idx: (e, 0, f)),
                  pl.BlockSpec((None, FFN_FC, D_MODEL), lambda e, f, idx: (e, f, 0)),
                  pl.BlockSpec(memory_space=pl.ANY)],
        out_specs=pl.BlockSpec(memory_space=pl.ANY),
        scratch_shapes=[pltpu.VMEM((cap * TOK_SUB, LANES), F32),
                        pltpu.VMEM((cap, D_MODEL), BF16),
                        pltpu.VMEM((cap, D_MODEL), F32),
                        pltpu.VMEM((cap * TOK_SUB, LANES), F32),
                        pltpu.SemaphoreType.DMA((3,))],
    )
    return pl.pallas_call(
        functools.partial(_ffn_kernel, cap=cap, nfc=nfc),
        grid_spec=grid_spec,
        out_shape=jax.ShapeDtypeStruct((n * TOK_SUB, LANES), F32),
        input_output_aliases={6: 0},
        compiler_params=_params(("arbitrary", "arbitrary")),
        name="ffn",
    )(idx.reshape(-1), gate, h2, w_gate, w_up, w_down, x1)


def _untile_kernel(x_ref, o_ref):
    for s in range(TOK_SUB):
        o_ref[:, s * LANES:(s + 1) * LANES] = x_ref[_tok_rows(s, o_ref.shape[0]), :]


def _untile(xt, tm):
    n = xt.shape[0] // TOK_SUB
    return pl.pallas_call(
        _untile_kernel,
        grid=(n // tm,),
        in_specs=[pl.BlockSpec((tm * TOK_SUB, LANES), lambda i: (i, 0))],
        out_specs=pl.BlockSpec((tm, D_MODEL), lambda i: (i, 0)),
        out_shape=jax.ShapeDtypeStruct((n, D_MODEL), F32),
        compiler_params=_params(("arbitrary",)),
        name="untile",
    )(xt)


def _rope_tables(T):
    half = DIFF_DH // 2
    inv_freq = jnp.power(ROPE_THETA, -jnp.arange(0, DIFF_DH, 2, dtype=F32) / DIFF_DH)
    ang = jnp.arange(T, dtype=F32)[:, None] * inv_freq[None, :]
    cos, sin = jnp.cos(ang), jnp.sin(ang)
    reps = HEAD_W // DIFF_DH
    return (jnp.concatenate([cos, cos] * reps, axis=-1),
            jnp.concatenate([-sin, sin] * reps, axis=-1))


def _layout_weights(w_in, gla_wg_f, gla_bg_f, gla_wg_b, gla_bg_b):
    q0, k0, v0, g0 = 0, 256, 512, 1024
    lr0, dq0 = 1536, 1568
    w = w_in.astype(BF16)
    cols = []
    for h in range(GLA_HEADS):
        cols += [w[:, q0 + h * GLA_DK:q0 + (h + 1) * GLA_DK], w[:, k0 + h * GLA_DK:k0 + (h + 1) * GLA_DK]]
    cols += [w[:, v0:g0], w[:, g0:lr0], w[:, dq0:], w[:, lr0:dq0],
             jnp.zeros((D_MODEL, LANES - 2 * GLA_RANK), BF16)]
    w_r = jnp.concatenate(cols, axis=1)
    wg = jnp.zeros((GLA_HEADS, LANES, HEAD_W), F32)
    bg = jnp.zeros((GLA_HEADS, 1, HEAD_W), F32)
    for h in range(GLA_HEADS):
        hs = slice(h * GLA_DK, (h + 1) * GLA_DK)
        wg = wg.at[h, :GLA_RANK, :GLA_DK].set(gla_wg_f[:, hs])
        wg = wg.at[h, GLA_RANK:2 * GLA_RANK, GLA_DK:].set(gla_wg_b[:, hs])
        bg = bg.at[h, 0, :GLA_DK].set(gla_bg_f[hs])
        bg = bg.at[h, 0, GLA_DK:].set(gla_bg_b[hs])
    return w_r, wg.astype(BF16), bg


def _lambda(lq1, lk1, lq2, lk2, lam_init):
    pad = lambda v: jnp.pad(v.astype(F32), (0, LANES - v.shape[0])).reshape(1, LANES)
    return pl.pallas_call(
        functools.partial(_lambda_kernel, lam_init=lam_init),
        out_shape=jax.ShapeDtypeStruct((1, 1), F32),
        name="lam",
    )(pad(lq1), pad(lk1), pad(lq2), pad(lk2))


def _encoder_layer(x, layer, lw, *, tm=512):
    (norm1_g, w_in, gla_wg_f, gla_bg_f, gla_wg_b, gla_bg_b, gla_norm_g, qk_norm_q, qk_norm_k,
     lambda_q1, lambda_k1, lambda_q2, lambda_k2, diff_norm_g, w_out, norm2_g, w_router,
     w_gate, w_up, w_down) = lw
    B, T, D = x.shape
    n = B * T
    cap = max(1, EC_FACTOR * n // N_EXPERTS)
    tm = min(tm, n)
    qb = min(T, DATTN_SCORE_BYTES // (2 * 4 * T))
    x2 = x.reshape(n, D)
    w_r, wg, bg = _layout_weights(w_in, gla_wg_f, gla_bg_f, gla_wg_b, gla_bg_b)
    row = lambda v: v.astype(F32).reshape(1, -1)
    per_map = lambda v: jnp.tile(v.astype(F32), 2 * DIFF_HEADS).reshape(1, -1)

    cos, sin = _rope_tables(T)
    gqk, gv, gog, dq, dk, dv, lr = _proj(x2, row(norm1_g), w_r, cos, sin,
                                         per_map(qk_norm_q), per_map(qk_norm_k), min(tm, T))
    a = _gla(gqk, gv, gog, lr, wg, bg, row(gla_norm_g), B, T)
    lam_init = _lambda_init(layer)
    lam = _lambda(lambda_q1, lambda_k1, lambda_q2, lambda_k2, lam_init)
    d = _dattn(dq, dk, dv, row(diff_norm_g), lam, B, T, qb, lam_init)
    x1, h2, aff_t = _mixout(a, d, x2, w_out.astype(BF16), row(norm2_g), w_router.T.astype(F32), tm)
    idx, gate = _route(aff_t, n, cap)
    y = _untile(_ffn(idx, gate, h2, x1, w_gate, w_up, w_down, cap), min(2 * tm, n))
    return y.reshape(B, T, D)


def kernel(x_prompt, x_sample, norm1_g, w_in, gla_wg_f, gla_bg_f, gla_wg_b, gla_bg_b, gla_norm_g, qk_norm_q, qk_norm_k, lambda_q1, lambda_k1, lambda_q2, lambda_k2, diff_norm_g, w_out, norm2_g, w_router, w_gate, w_up, w_down):
    y_prompt, y_sample = x_prompt, x_sample
    for l in range(norm1_g.shape[0]):
        lw = (norm1_g[l], w_in[l], gla_wg_f[l], gla_bg_f[l], gla_wg_b[l], gla_bg_b[l],
              gla_norm_g[l], qk_norm_q[l], qk_norm_k[l], lambda_q1[l], lambda_k1[l],
              lambda_q2[l], lambda_k2[l], diff_norm_g[l], w_out[l], norm2_g[l],
              w_router[l], w_gate[l], w_up[l], w_down[l])
        y_prompt = _encoder_layer(y_prompt, l, lw)
        y_sample = _encoder_layer(y_sample, l, lw)
    return (y_prompt, y_sample)
```

```python
import functools
import math

import jax
import jax.numpy as jnp
from jax import lax
from jax.experimental import pallas as pl
from jax.experimental.pallas import tpu as pltpu

F32 = jnp.float32
BF16 = jnp.bfloat16
I32 = jnp.int32

LANES = 128
MXU_W = 256
VMEM_LIMIT = 56 * 1024 * 1024

D_MODEL = 1024
GLA_HEADS = 4
GLA_DK = 64
GLA_DV = 128
GLA_RANK = 16
GLA_TAU = 16.0
GLA_CHUNK = 64
DIFF_HEADS = 4
DIFF_DH = 64
DIFF_DV = 128
ROPE_THETA = 10000.0
N_EXPERTS = 16
EC_FACTOR = 2
D_EXPERT = 2 * D_MODEL
RMS_EPS = 1e-6
LOG2_E = 1.4426950408889634


def _lambda_init(layer):
    return 0.8 - 0.6 * math.exp(-0.3 * layer)


HEAD_W = 128
PROJ_GROUPS = 6
PROJ_W = PROJ_GROUPS * 4 * HEAD_W + LANES
TOK_SUB = D_MODEL // LANES


def _tok_rows(s, ntok, first_tok=0):
    return pl.ds(first_tok * TOK_SUB + s, ntok, stride=TOK_SUB)


def _dot(a, b):
    return jnp.dot(a, b, preferred_element_type=F32)


def _dot_nt(a, b):
    return lax.dot_general(a, b, (((1,), (1,)), ((), ())), preferred_element_type=F32)


def _dot_tn(a, b):
    return lax.dot_general(a, b, (((0,), (0,)), ((), ())), preferred_element_type=F32)


def _split3(x):
    a = x.astype(BF16)
    r = x - a.astype(F32)
    b = r.astype(BF16)
    c = (r - b.astype(F32)).astype(BF16)
    return a, b, c


def _params(sem):
    return pltpu.CompilerParams(dimension_semantics=sem, vmem_limit_bytes=VMEM_LIMIT)


def _qk_prep(x, gain, cos, sin, same_map):
    x2 = (x * x).astype(BF16)
    width = x.shape[1]
    mw = same_map.shape[0]
    ms = jnp.concatenate([_dot(x2[:, c:c + mw], same_map) for c in range(0, width, mw)], axis=1)
    y = x * lax.rsqrt(ms * (1.0 / DIFF_DH) + RMS_EPS) * gain
    half = DIFF_DH // 2
    first = (lax.broadcasted_iota(I32, (1, width), 1) & half) == 0
    swapped = jnp.where(first, pltpu.roll(y, width - half, 1), pltpu.roll(y, half, 1))
    return y * cos + swapped * sin


def _proj_kernel(x_ref, g_ref, w_ref, cos_ref, sin_ref, qg_ref, kg_ref, map_ref,
                 gqk_ref, gv_ref, gog_ref, dq_ref, dk_ref, dv_ref, lr_ref):
    x = x_ref[...]
    ms = jnp.mean(x * x, axis=-1, keepdims=True)
    h = (x * lax.rsqrt(ms + RMS_EPS) * g_ref[...]).astype(BF16)
    gw = 4 * HEAD_W
    group = lambda i: _dot(h, w_ref[:, i * gw:(i + 1) * gw])
    for i, o in ((0, gqk_ref), (1, gv_ref), (2, gog_ref), (5, dv_ref)):
        o[...] = group(i).astype(o.dtype)
    cos = jnp.concatenate([cos_ref[...]] * DIFF_HEADS, axis=1)
    sin = jnp.concatenate([sin_ref[...]] * DIFF_HEADS, axis=1)
    q = _qk_prep(group(3), qg_ref[...], cos, sin, map_ref[...])
    dq_ref[...] = (q * (DIFF_DH ** -0.5 * LOG2_E)).astype(dq_ref.dtype)
    dk_ref[...] = _qk_prep(group(4), kg_ref[...], cos, sin, map_ref[...]).astype(dk_ref.dtype)
    lr_ref[...] = _dot(h, w_ref[:, PROJ_GROUPS * gw:])


def _proj(x2, g1, w_r, cos, sin, qg, kg, tm):
    n = x2.shape[0]
    T = cos.shape[0]
    gw = 4 * HEAD_W
    row = lambda i: (i, 0)
    const = lambda i: (0, 0)
    pos = lambda i: (i % (T // tm), 0)
    ri = lax.broadcasted_iota(I32, (MXU_W, MXU_W), 0)
    ci = lax.broadcasted_iota(I32, (MXU_W, MXU_W), 1)
    same_map = ((ri // DIFF_DH) == (ci // DIFF_DH)).astype(BF16)
    outs = [jax.ShapeDtypeStruct((n, gw), BF16)] * PROJ_GROUPS + [jax.ShapeDtypeStruct((n, LANES), F32)]
    return pl.pallas_call(
        _proj_kernel,
        grid=(n // tm,),
        in_specs=[pl.BlockSpec((tm, D_MODEL), row),
                  pl.BlockSpec((1, D_MODEL), const),
                  pl.BlockSpec((D_MODEL, PROJ_W), const),
                  pl.BlockSpec((tm, HEAD_W), pos), pl.BlockSpec((tm, HEAD_W), pos),
                  pl.BlockSpec((1, gw), const), pl.BlockSpec((1, gw), const),
                  pl.BlockSpec((MXU_W, MXU_W), const)],
        out_specs=[pl.BlockSpec((tm, gw), row)] * PROJ_GROUPS + [pl.BlockSpec((tm, LANES), row)],
        out_shape=outs,
        compiler_params=_params(("arbitrary",)),
        name="proj",
    )(x2, g1, w_r, cos, sin, qg, kg, same_map)


GLA_BLK = 256


def _gla_kernel(gqk_ref, gv_ref, gog_ref, lr_ref, wg_ref, bg_ref, ng_ref, o_ref,
                qh_ref, u_ref, dec_ref, acc_ref, b_ref, tot_ref, *, T):
    C = GLA_CHUNK
    nblk = T // GLA_BLK
    nch = T // C
    lane = lax.broadcasted_iota(I32, (1, HEAD_W), 1)
    fwd = lane < GLA_DK
    ri = lax.broadcasted_iota(I32, (GLA_BLK, GLA_BLK), 0)
    ci = lax.broadcasted_iota(I32, (GLA_BLK, GLA_BLK), 1)
    same = (ri // C) == (ci // C)
    prefix_total = jnp.concatenate([jnp.where(same & (ci <= ri), 1.0, 0.0),
                                    jnp.where(same, 1.0, 0.0)], axis=0).astype(BF16)
    m_f = same & (ci <= ri)
    m_b = same & (ci > ri)
    wg = wg_ref[...]
    bg = bg_ref[...]
    cpb = GLA_BLK // C
    chunk_of_row = lax.broadcasted_iota(I32, (GLA_BLK, 1), 0) // C

    def gates(i, carry):
        rows = pl.ds(pl.multiple_of(i * GLA_BLK, GLA_BLK), GLA_BLK)
        z = _dot(lr_ref[rows, :].astype(BF16), wg) + bg
        la = (jnp.minimum(z, 0.0) - jnp.log(1.0 + jnp.exp(-jnp.abs(z)))) * (1.0 / GLA_TAU)
        hi = la.astype(BF16)
        lo = (la - hi.astype(F32)).astype(BF16)
        pt = _dot(prefix_total, jnp.concatenate([hi, lo], axis=1))
        pt = pt[:, :HEAD_W] + pt[:, HEAD_W:]
        pre, tot = pt[:GLA_BLK, :], pt[GLA_BLK:, :]
        b_ref[rows, :] = jnp.where(fwd, pre, tot - pre + la)
        tot_ref[rows, :] = tot
        return carry

    lax.fori_loop(0, nblk, gates, 0, unroll=8)

    def block(i, carry):
        rows = pl.ds(pl.multiple_of(i * GLA_BLK, GLA_BLK), GLA_BLK)
        b = b_ref[rows, :]
        tot = tot_ref[rows, :]
        mid = 0.5 * tot
        blk = gqk_ref[rows, :].astype(F32)
        rot = pltpu.roll(blk, GLA_DK, 1)
        qq = jnp.where(fwd, blk, rot) * (GLA_DK ** -0.5)
        kk = jnp.where(fwd, rot, blk)
        qt = qq * jnp.exp(b - mid)
        kt = (kk * jnp.exp(mid - b)).astype(BF16)
        kh = (kk * jnp.exp(tot - b)).astype(BF16)
        qh_ref[rows, :] = (qq * jnp.exp(b)).astype(BF16)
        qfb = jnp.concatenate([jnp.where(fwd, qt, 0.0), jnp.where(fwd, 0.0, qt)], axis=0).astype(BF16)
        sfb = _dot_nt(qfb, kt)
        s = jnp.where(m_f, sfb[:GLA_BLK, :], jnp.where(m_b, sfb[GLA_BLK:, :], 0.0)).astype(BF16)
        v = gv_ref[rows, :]
        acc_ref[rows, :] = _dot(s, v)
        zero = jnp.zeros_like(kh)
        kx = jnp.concatenate([jnp.where(chunk_of_row == c, kh, zero) for c in range(cpb)], axis=1)
        ux = _dot_tn(v, kx)
        dec = jnp.exp(tot)
        for c in range(cpb):
            u_ref[i * cpb + c] = ux[:, c * HEAD_W:(c + 1) * HEAD_W]
            dec_ref[i * cpb + c] = jnp.broadcast_to(dec[c * C:c * C + 1, :], (8, HEAD_W))
        return carry

    lax.fori_loop(0, nblk, block, 0, unroll=8)

    def step(t, st):
        nf = t
        nb = nch - 1 - t
        stb = st.astype(BF16)
        rf = pl.ds(pl.multiple_of(nf * C, C), C)
        rb = pl.ds(pl.multiple_of(nb * C, C), C)
        qf = qh_ref[rf, :]
        qb = qh_ref[rb, :]
        zero = jnp.zeros_like(qf)
        o = _dot_nt(jnp.concatenate([jnp.where(fwd, qf, zero), jnp.where(fwd, zero, qb)], axis=0), stb)
        acc_ref[rf, :] += o[:C, :]
        acc_ref[rb, :] += o[C:, :]
        d = jnp.where(fwd, dec_ref[nf][0:1, :], dec_ref[nb][0:1, :])
        u = jnp.where(fwd, u_ref[nf], u_ref[nb])
        return st * d + u

    lax.fori_loop(0, nch, step, jnp.zeros((GLA_DV, HEAD_W), F32), unroll=16)

    ng = ng_ref[...]
    eb = min(T, 512)

    def epi(i, carry):
        rows = pl.ds(pl.multiple_of(i * eb, eb), eb)
        o = acc_ref[rows, :]
        y = o * lax.rsqrt(jnp.mean(o * o, axis=-1, keepdims=True) + RMS_EPS) * ng
        og = gog_ref[rows, :].astype(F32)
        o_ref[rows, :] = (y * (og * jax.nn.sigmoid(og))).astype(o_ref.dtype)
        return carry

    lax.fori_loop(0, T // eb, epi, 0)


def _gla(gqk, gv, gog, lr, wg, bg, ng, B, T):
    n = B * T
    head = pl.BlockSpec((T, HEAD_W), lambda b, h: (b, h))
    return pl.pallas_call(
        functools.partial(_gla_kernel, T=T),
        grid=(B, GLA_HEADS),
        in_specs=[head, head, head,
                  pl.BlockSpec((T, LANES), lambda b, h: (b, 0)),
                  pl.BlockSpec((None, LANES, HEAD_W), lambda b, h: (h, 0, 0)),
                  pl.BlockSpec((None, 1, HEAD_W), lambda b, h: (h, 0, 0)),
                  pl.BlockSpec((1, HEAD_W), lambda b, h: (0, 0))],
        out_specs=head,
        out_shape=jax.ShapeDtypeStruct((n, GLA_HEADS * HEAD_W), BF16),
        scratch_shapes=[pltpu.VMEM((T, HEAD_W), BF16),
                        pltpu.VMEM((T // GLA_CHUNK, GLA_DV, HEAD_W), F32),
                        pltpu.VMEM((T // GLA_CHUNK, 8, HEAD_W), F32),
                        pltpu.VMEM((T, GLA_DV), F32),
                        pltpu.VMEM((T, HEAD_W), F32),
                        pltpu.VMEM((T, HEAD_W), F32)],
        compiler_params=_params(("arbitrary", "arbitrary")),
        name="gla",
    )(gqk, gv, gog, lr, wg, bg, ng)


DATTN_KSPLIT = 1
DATTN_SCORE_BYTES = 32 * 1024 * 1024


def _dattn_kernel(dq_ref, dk_ref, dv_ref, ng_ref, lam_ref, o_ref,
                  s_ref, mp_ref, lp_ref, acc_ref, *, T, QB, TK, lam_init):
    lane = lax.broadcasted_iota(I32, (1, HEAD_W), 1)
    map0 = lane < DIFF_DH
    nk = T // TK + lax.shift_right_logical(pl.program_id(2), 30)
    q = dq_ref[...]
    zero = jnp.zeros_like(q)
    qs = jnp.concatenate([jnp.where(map0, q, zero), jnp.where(map0, zero, q)], axis=0)
    ntile = TK // LANES

    def lane_tiles(x):
        return [x[:, j * LANES:(j + 1) * LANES] for j in range(ntile)]

    mp_ref[...] = jnp.full(mp_ref.shape, -jnp.inf, F32)
    lp_ref[...] = jnp.zeros(lp_ref.shape, F32)
    acc_ref[...] = jnp.zeros(acc_ref.shape, F32)

    def scores(i, carry):
        cols = pl.ds(pl.multiple_of(i * TK, TK), TK)
        s = _dot_nt(qs, dk_ref[cols, :])
        s_ref[:, cols] = s
        mp_ref[...] = jnp.maximum(mp_ref[...], functools.reduce(jnp.maximum, lane_tiles(s)))
        return carry

    lax.fori_loop(0, nk, scores, 0)
    m = jnp.max(mp_ref[...], axis=-1, keepdims=True)

    def accum(i, carry):
        cols = pl.ds(pl.multiple_of(i * TK, TK), TK)
        p = jnp.exp2(s_ref[:, cols] - m)
        lp_ref[...] += functools.reduce(jnp.add, lane_tiles(p))
        acc_ref[...] += _dot(p.astype(BF16), dv_ref[cols, :])
        return carry

    lax.fori_loop(0, nk, accum, 0)
    on = acc_ref[...] / jnp.sum(lp_ref[...], axis=-1, keepdims=True)
    lam = lam_ref[...]
    o = on[:QB, :] - lam * on[QB:, :]
    y = o * lax.rsqrt(jnp.mean(o * o, axis=-1, keepdims=True) + RMS_EPS) * ng_ref[...]
    o_ref[...] = (y * (1.0 - lam_init)).astype(o_ref.dtype)


def _lambda_kernel(q1_ref, k1_ref, q2_ref, k2_ref, o_ref, *, lam_init):
    a = jnp.exp(jnp.sum(q1_ref[...] * k1_ref[...], axis=-1, keepdims=True))
    b = jnp.exp(jnp.sum(q2_ref[...] * k2_ref[...], axis=-1, keepdims=True))
    o_ref[...] = a - b + lam_init


def _dattn(dq, dk, dv, ng, lam, B, T, QB, lam_init):
    n = B * T
    TK = T // DATTN_KSPLIT
    nq = T // QB
    full = lambda shape: pl.BlockSpec(shape, lambda b, h, q: (0, 0))
    kv = pl.BlockSpec((T, HEAD_W), lambda b, h, q: (b, h))
    qo = pl.BlockSpec((QB, HEAD_W), lambda b, h, q: (b * nq + q, h))
    return pl.pallas_call(
        functools.partial(_dattn_kernel, T=T, QB=QB, TK=TK, lam_init=lam_init),
        grid=(B, DIFF_HEADS, nq),
        in_specs=[qo, kv, kv, full((1, HEAD_W)), full((1, 1))],
        out_specs=qo,
        out_shape=jax.ShapeDtypeStruct((n, DIFF_HEADS * HEAD_W), BF16),
        scratch_shapes=[pltpu.VMEM((2 * QB, T), F32),
                        pltpu.VMEM((2 * QB, LANES), F32),
                        pltpu.VMEM((2 * QB, LANES), F32),
                        pltpu.VMEM((2 * QB, DIFF_DV), F32)],
        compiler_params=_params(("arbitrary", "arbitrary", "arbitrary")),
        name="dattn",
    )(dq, dk, dv, ng, lam)


def _mixout_kernel(a_ref, d_ref, x_ref, w_ref, g_ref, wr_ref, x1_ref, h2_ref, aff_ref):
    half = GLA_HEADS * HEAD_W
    y = _dot(a_ref[...], w_ref[:half, :]) + _dot(d_ref[...], w_ref[half:, :])
    x1 = x_ref[...] + y
    h2 = x1 * lax.rsqrt(jnp.mean(x1 * x1, axis=-1, keepdims=True) + RMS_EPS) * g_ref[...]
    for s in range(TOK_SUB):
        cols = slice(s * LANES, (s + 1) * LANES)
        x1_ref[_tok_rows(s, x1.shape[0]), :] = x1[:, cols]
        h2_ref[_tok_rows(s, x1.shape[0]), :] = h2[:, cols]
    logits = jnp.zeros((N_EXPERTS, h2.shape[0]), F32)
    hp = _split3(h2)
    wp = _split3(wr_ref[...])
    for i, j in ((0, 0), (0, 1), (1, 0)):
        logits = logits + _dot_nt(wp[i], hp[j])
    m = jnp.max(logits, axis=0, keepdims=True)
    e = jnp.exp(logits - m)
    aff_ref[...] = e / jnp.sum(e, axis=0, keepdims=True)


def _mixout(a, d, x2, w_out, g2, wr_t, tm):
    n = x2.shape[0]
    row = lambda i: (i, 0)
    half = GLA_HEADS * HEAD_W
    return pl.pallas_call(
        _mixout_kernel,
        grid=(n // tm,),
        in_specs=[pl.BlockSpec((tm, half), row), pl.BlockSpec((tm, half), row),
                  pl.BlockSpec((tm, D_MODEL), row),
                  pl.BlockSpec((D_MODEL, D_MODEL), lambda i: (0, 0)),
                  pl.BlockSpec((1, D_MODEL), lambda i: (0, 0)),
                  pl.BlockSpec((N_EXPERTS, D_MODEL), lambda i: (0, 0))],
        out_specs=[pl.BlockSpec((tm * TOK_SUB, LANES), row), pl.BlockSpec((tm * TOK_SUB, LANES), row),
                   pl.BlockSpec((N_EXPERTS, tm), lambda i: (0, i))],
        out_shape=[jax.ShapeDtypeStruct((n * TOK_SUB, LANES), F32), jax.ShapeDtypeStruct((n * TOK_SUB, LANES), F32),
                   jax.ShapeDtypeStruct((N_EXPERTS, n), F32)],
        compiler_params=_params(("arbitrary",)),
        name="mixout",
    )(a, d, x2, w_out, g2, wr_t)


ROUTE_BISECTIONS = 40


def _route_kernel(aff_ref, idx_ref, gate_ref, *, n, cap):
    R = n // LANES
    E = N_EXPERTS
    aff = aff_ref[...]

    def count(mask):
        c = jnp.sum(jnp.where(mask, 1.0, 0.0), axis=1, keepdims=True)
        return jnp.sum(c, axis=2, keepdims=True)

    capf = float(cap)
    k = jnp.zeros((E, 1, 1), F32)
    for bit in (64.0, 32.0, 16.0, 8.0, 4.0, 2.0, 1.0):
        cand = k + bit
        k = jnp.where(count(aff >= jnp.exp2(1.0 - cand)) < capf, cand, k)
    hi = jnp.exp2(1.0 - k)
    lo = jnp.where(k >= 127.0, 0.0, jnp.exp2(-jnp.minimum(k, 126.0)))

    def bisect(i, c):
        lo, hi = c
        mid = 0.5 * (lo + hi)
        ok = count(aff >= mid) >= capf
        return jnp.where(ok, mid, lo), jnp.where(ok, hi, mid)

    lo, hi = lax.fori_loop(0, ROUTE_BISECTIONS, bisect, (lo, hi))
    gt = aff >= hi
    eq = (aff >= lo) & (aff < hi)
    need = capf - count(gt)
    tok = (lax.broadcasted_iota(I32, (1, R, LANES), 1) * LANES
           + lax.broadcasted_iota(I32, (1, R, LANES), 2))
    nbits = max(1, (n - 1).bit_length())

    def tie_step(i, m0):
        cand = m0 | jnp.left_shift(jnp.int32(1), nbits - 1 - i)
        return jnp.where(count(eq & (tok < cand)) < need, cand, m0)

    m0 = lax.fori_loop(0, nbits, tie_step, jnp.zeros((E, 1, 1), I32))
    sel = gt | (eq & (tok <= m0))

    ci = lax.broadcasted_iota(I32, (LANES, LANES), 0)
    cj = lax.broadcasted_iota(I32, (LANES, LANES), 1)
    upper = jnp.where(ci <= cj, 1.0, 0.0).astype(BF16)
    ri = lax.broadcasted_iota(I32, (R, R), 0)
    rj = lax.broadcasted_iota(I32, (R, R), 1)
    lower = jnp.where(rj < ri, 1.0, 0.0).astype(BF16)
    jrow = lax.broadcasted_iota(I32, (1, cap), 1).astype(F32)
    rcol = lax.broadcasted_iota(I32, (R, 1), 0).astype(F32)
    ccol = lax.broadcasted_iota(I32, (LANES, 1), 0).astype(F32)

    for e in range(E):
        sel_e = sel[e]
        w = _dot(jnp.where(sel_e, 1.0, 0.0).astype(BF16), upper)
        rt = w[:, LANES - 1:LANES]
        rp = _dot(lower, jnp.broadcast_to(rt, (R, LANES)).astype(BF16))[:, 0:1]
        rows_t = jnp.where((rp <= jrow) & (jrow < rp + rt), 1.0, 0.0)
        k1 = jrow - jnp.sum(rows_t * rp, axis=0, keepdims=True) + 1.0
        rows_b = rows_t.astype(BF16)
        wrow = _dot(jnp.where(sel_e, w, 0.0).T.astype(BF16), rows_b)
        cols_t = jnp.where(wrow == k1, 1.0, 0.0)
        r_of = jnp.sum(rows_t * rcol, axis=0, keepdims=True)
        c_of = jnp.sum(cols_t * ccol, axis=0, keepdims=True)
        idx_ref[e] = (r_of * float(LANES) + c_of).astype(I32)
        a3 = _split3(aff[e].T)
        arow = _dot(a3[0], rows_b) + _dot(a3[1], rows_b) + _dot(a3[2], rows_b)
        gate_ref[e] = jnp.sum(cols_t * arow, axis=0, keepdims=True)


def _route(aff_t, n, cap):
    R = n // LANES
    return pl.pallas_call(
        functools.partial(_route_kernel, n=n, cap=cap),
        out_shape=[jax.ShapeDtypeStruct((N_EXPERTS, 1, cap), I32),
                   jax.ShapeDtypeStruct((N_EXPERTS, 1, cap), F32)],
        compiler_params=pltpu.CompilerParams(vmem_limit_bytes=VMEM_LIMIT),
        name="route",
    )(aff_t.reshape(N_EXPERTS, R, LANES))


FFN_FC = 512
FFN_TT = 1024
FFN_DMA_UNROLL = 16


def _ffn_kernel(idx_ref, gate_ref, h2_hbm, wg_ref, wu_ref, wd_ref, x1_hbm, out_hbm,
                stage_ref, xb_ref, acc_ref, orow_ref, sems, *, cap, nfc):
    del x1_hbm
    e = pl.program_id(0)
    fc = pl.program_id(1)
    ne = pl.num_programs(0)
    g_sem, o_sem, s_sem = sems.at[0], sems.at[1], sems.at[2]
    tt = min(FFN_TT, cap)

    def row_copy(src, s_tok, dst, d_tok, sem):
        s_rows = pl.ds(pl.multiple_of(s_tok * TOK_SUB, TOK_SUB), TOK_SUB)
        d_rows = pl.ds(pl.multiple_of(d_tok * TOK_SUB, TOK_SUB), TOK_SUB)
        return pltpu.make_async_copy(src.at[s_rows], dst.at[d_rows], sem)

    def issue_gather(ee, src, dst, sem):
        def body(j, c):
            row_copy(src, idx_ref[ee * cap + j], dst, j, sem).start()
            return c
        lax.fori_loop(0, cap, body, 0, unroll=FFN_DMA_UNROLL)

    def wait_rows(buf, sem):
        pltpu.make_async_copy(buf, buf, sem).wait()

    gather_step = nfc - 2
    x_steps = [k for k in range(nfc) if k not in (0, gather_step)]
    assert 0 < gather_step and x_steps, "the copy schedule needs at least three hidden-dim steps"
    ntile = cap // tt
    per_tile = cap // ntile
    x_per_tile = cap // (len(x_steps) * ntile)
    e_next = jnp.minimum(e + 1, ne - 1)

    @pl.when(fc == 0)
    def _():
        @pl.when(e == 0)
        def _():
            issue_gather(0, h2_hbm, stage_ref, g_sem)
        wait_rows(stage_ref, g_sem)
        blk = min(cap, 256)
        for b in range(cap // blk):
            tiles = stage_ref[b * blk * TOK_SUB:(b + 1) * blk * TOK_SUB, :]
            xb_ref[b * blk:(b + 1) * blk, :] = pltpu.einshape("(ts)l->t(sl)", tiles, s=TOK_SUB).astype(BF16)

    @pl.when((fc == gather_step) & (e > 0))
    def _():
        wait_rows(orow_ref, s_sem)

    def finish_rows(ee, tok0, ntok):
        for b in range(ntok // LANES):
            first = tok0 + b * LANES
            rows = pl.ds(first if isinstance(first, int) else pl.multiple_of(first, LANES), LANES)
            g = gate_ref.at[ee][:, rows]
            gcol = jnp.broadcast_to(g, (LANES, LANES)).T
            gated = acc_ref[rows, :] * jnp.concatenate([gcol] * TOK_SUB, axis=1)
            trows = pl.ds(first * TOK_SUB, LANES * TOK_SUB)
            orow_ref[trows, :] += pltpu.einshape("t(sl)->(ts)l", gated, s=TOK_SUB)

    def scatter_prev(t):
        for jj in range(per_tile):
            j = t * per_tile + jj
            row_copy(orow_ref, j, out_hbm, idx_ref[(e - 1) * cap + j], s_sem).start(priority=jj % 2)

    def gather_out(t):
        for jj in range(per_tile):
            j = t * per_tile + jj
            row_copy(out_hbm, idx_ref[e * cap + j], orow_ref, j, o_sem).start(priority=jj % 2)

    def gather_next(part):
        def issue(t):
            for jj in range(x_per_tile):
                j = (part * ntile + t) * x_per_tile + jj
                row_copy(h2_hbm, idx_ref[e_next * cap + j], stage_ref, j, g_sem).start(priority=jj % 2)
        return issue

    def tile_loop(copies, first):
        def tile(t, c):
            if copies is not None:
                copies(t)
            rows = pl.ds(pl.multiple_of(t * tt, tt), tt)
            x = xb_ref[rows, :]
            a = _dot(x, wg_ref[...].astype(BF16))
            u = _dot(x, wu_ref[...].astype(BF16))
            hm = (a * jax.nn.sigmoid(a) * u).astype(BF16)
            y = _dot(hm, wd_ref[...].astype(BF16))
            if first:
                acc_ref[rows, :] = y
            else:
                acc_ref[rows, :] += y
            return c
        lax.fori_loop(0, ntile, tile, 0)

    pl.when((fc == 0) & (e == 0))(lambda: tile_loop(None, True))
    pl.when((fc == 0) & (e > 0))(lambda: tile_loop(scatter_prev, True))
    pl.when(fc == gather_step)(lambda: tile_loop(gather_out, False))
    for part, k in enumerate(x_steps):
        pl.when(fc == k)(functools.partial(tile_loop, gather_next(part), False))

    @pl.when(fc == nfc - 1)
    def _():
        wait_rows(orow_ref, o_sem)
        finish_rows(e, 0, cap)

        @pl.when(e == ne - 1)
        def _():
            def body(j, c):
                row_copy(orow_ref, j, out_hbm, idx_ref[e * cap + j], s_sem).start()
                return c
            lax.fori_loop(0, cap, body, 0, unroll=FFN_DMA_UNROLL)
            wait_rows(orow_ref, s_sem)
            wait_rows(stage_ref, g_sem)


def _ffn(idx, gate, h2, x1, w_gate, w_up, w_down, cap):
    n = h2.shape[0] // TOK_SUB
    nfc = D_EXPERT // FFN_FC
    grid_spec = pltpu.PrefetchScalarGridSpec(
        num_scalar_prefetch=1,
        grid=(N_EXPERTS, nfc),
        in_specs=[pl.BlockSpec((N_EXPERTS, 1, cap), lambda e, f, idx: (0, 0, 0)),
                  pl.BlockSpec(memory_space=pl.ANY),
                  pl.BlockSpec((None, D_MODEL, FFN_FC), lambda e, f, idx: (e, 0, f)),
                  pl.BlockSpec((None, D_MODEL, FFN_FC), lambda e, f, idx: (e, 0, f)),
                  pl.BlockSpec((None, FFN_FC, D_MODEL), lambda e, f, idx: (e, f, 0)),
                  pl.BlockSpec(memory_space=pl.ANY)],
        out_specs=pl.BlockSpec(memory_space=pl.ANY),
        scratch_shapes=[pltpu.VMEM((cap * TOK_SUB, LANES), F32),
                        pltpu.VMEM((cap, D_MODEL), BF16),
                        pltpu.VMEM((cap, D_MODEL), F32),
                        pltpu.VMEM((cap * TOK_SUB, LANES), F32),
                        pltpu.SemaphoreType.DMA((3,))],
    )
    return pl.pallas_call(
        functools.partial(_ffn_kernel, cap=cap, nfc=nfc),
        grid_spec=grid_spec,
        out_shape=jax.ShapeDtypeStruct((n * TOK_SUB, LANES), F32),
        input_output_aliases={6: 0},
        compiler_params=_params(("arbitrary", "arbitrary")),
        name="ffn",
    )(idx.reshape(-1), gate, h2, w_gate, w_up, w_down, x1)


def _untile_kernel(x_ref, o_ref):
    for s in range(TOK_SUB):
        o_ref[:, s * LANES:(s + 1) * LANES] = x_ref[_tok_rows(s, o_ref.shape[0]), :]


def _untile(xt, tm):
    n = xt.shape[0] // TOK_SUB
    return pl.pallas_call(
        _untile_kernel,
        grid=(n // tm,),
        in_specs=[pl.BlockSpec((tm * TOK_SUB, LANES), lambda i: (i, 0))],
        out_specs=pl.BlockSpec((tm, D_MODEL), lambda i: (i, 0)),
        out_shape=jax.ShapeDtypeStruct((n, D_MODEL), F32),
        compiler_params=_params(("arbitrary",)),
        name="untile",
    )(xt)


def _rope_tables(T):
    half = DIFF_DH // 2
    inv_freq = jnp.power(ROPE_THETA, -jnp.arange(0, DIFF_DH, 2, dtype=F32) / DIFF_DH)
    ang = jnp.arange(T, dtype=F32)[:, None] * inv_freq[None, :]
    cos, sin = jnp.cos(ang), jnp.sin(ang)
    reps = HEAD_W // DIFF_DH
    return (jnp.concatenate([cos, cos] * reps, axis=-1),
            jnp.concatenate([-sin, sin] * reps, axis=-1))


def _layout_weights(w_in, gla_wg_f, gla_bg_f, gla_wg_b, gla_bg_b):
    q0, k0, v0, g0 = 0, 256, 512, 1024
    lr0, dq0 = 1536, 1568
    w = w_in.astype(BF16)
    cols = []
    for h in range(GLA_HEADS):
        cols += [w[:, q0 + h * GLA_DK:q0 + (h + 1) * GLA_DK], w[:, k0 + h * GLA_DK:k0 + (h + 1) * GLA_DK]]
    cols += [w[:, v0:g0], w[:, g0:lr0], w[:, dq0:], w[:, lr0:dq0],
             jnp.zeros((D_MODEL, LANES - 2 * GLA_RANK), BF16)]
    w_r = jnp.concatenate(cols, axis=1)
    wg = jnp.zeros((GLA_HEADS, LANES, HEAD_W), F32)
    bg = jnp.zeros((GLA_HEADS, 1, HEAD_W), F32)
    for h in range(GLA_HEADS):
        hs = slice(h * GLA_DK, (h + 1) * GLA_DK)
        wg = wg.at[h, :GLA_RANK, :GLA_DK].set(gla_wg_f[:, hs])
        wg = wg.at[h, GLA_RANK:2 * GLA_RANK, GLA_DK:].set(gla_wg_b[:, hs])
        bg = bg.at[h, 0, :GLA_DK].set(gla_bg_f[hs])
        bg = bg.at[h, 0, GLA_DK:].set(gla_bg_b[hs])
    return w_r, wg.astype(BF16), bg


def _lambda(lq1, lk1, lq2, lk2, lam_init):
    pad = lambda v: jnp.pad(v.astype(F32), (0, LANES - v.shape[0])).reshape(1, LANES)
    return pl.pallas_call(
        functools.partial(_lambda_kernel, lam_init=lam_init),
        out_shape=jax.ShapeDtypeStruct((1, 1), F32),
        name="lam",
    )(pad(lq1), pad(lk1), pad(lq2), pad(lk2))


def _encoder_layer(x, layer, lw, *, tm=512):
    (norm1_g, w_in, gla_wg_f, gla_bg_f, gla_wg_b, gla_bg_b, gla_norm_g, qk_norm_q, qk_norm_k,
     lambda_q1, lambda_k1, lambda_q2, lambda_k2, diff_norm_g, w_out, norm2_g, w_router,
     w_gate, w_up, w_down) = lw
    B, T, D = x.shape
    n = B * T
    cap = max(1, EC_FACTOR * n // N_EXPERTS)
    tm = min(tm, n)
    qb = min(T, DATTN_SCORE_BYTES // (2 * 4 * T))
    x2 = x.reshape(n, D)
    w_r, wg, bg = _layout_weights(w_in, gla_wg_f, gla_bg_f, gla_wg_b, gla_bg_b)
    row = lambda v: v.astype(F32).reshape(1, -1)
    per_map = lambda v: jnp.tile(v.astype(F32), 2 * DIFF_HEADS).reshape(1, -1)

    cos, sin = _rope_tables(T)
    gqk, gv, gog, dq, dk, dv, lr = _proj(x2, row(norm1_g), w_r, cos, sin,
                                         per_map(qk_norm_q), per_map(qk_norm_k), min(tm, T))
    a = _gla(gqk, gv, gog, lr, wg, bg, row(gla_norm_g), B, T)
    lam_init = _lambda_init(layer)
    lam = _lambda(lambda_q1, lambda_k1, lambda_q2, lambda_k2, lam_init)
    d = _dattn(dq, dk, dv, row(diff_norm_g), lam, B, T, qb, lam_init)
    x1, h2, aff_t = _mixout(a, d, x2, w_out.astype(BF16), row(norm2_g), w_router.T.astype(F32), tm)
    idx, gate = _route(aff_t, n, cap)
    y = _untile(_ffn(idx, gate, h2, x1, w_gate, w_up, w_down, cap), min(2 * tm, n))
    return y.reshape(B, T, D)


def kernel(x_prompt, x_sample, norm1_g, w_in, gla_wg_f, gla_bg_f, gla_wg_b, gla_bg_b, gla_norm_g, qk_norm_q, qk_norm_k, lambda_q1, lambda_k1, lambda_q2, lambda_k2, diff_norm_g, w_out, norm2_g, w_router, w_gate, w_up, w_down):
    y_prompt, y_sample = x_prompt, x_sample
    for l in range(norm1_g.shape[0]):
        lw = (norm1_g[l], w_in[l], gla_wg_f[l], gla_bg_f[l], gla_wg_b[l], gla_bg_b[l],
              gla_norm_g[l], qk_norm_q[l], qk_norm_k[l], lambda_q1[l], lambda_k1[l],
              lambda_q2[l], lambda_k2[l], diff_norm_g[l], w_out[l], norm2_g[l],
              w_router[l], w_gate[l], w_up[l], w_down[l])
        y_prompt = _encoder_layer(y_prompt, l, lw)
        y_sample = _encoder_layer(y_sample, l, lw)
    return (y_prompt, y_sample)
```
